```python
import math
import jax, jax.numpy as jnp
from jax import lax
import numpy as np

D_MODEL = 2048
BATCH = 1
SEQ = 16384
DEPTH = 4

MIX_WIDTH = D_MODEL
GMLP_WIDTH = D_MODEL // 4
GMLP_GROUPS = 4
GMLP_GROUP_DIM = GMLP_WIDTH // GMLP_GROUPS
GMLP_CHUNK = 128
DIFF_WIDTH = 3 * D_MODEL // 8
DIFF_HEAD_DIM = 64
DIFF_HEADS = DIFF_WIDTH // (2 * DIFF_HEAD_DIM)
ATTN_BLOCK = 128
GLA_WIDTH = MIX_WIDTH - GMLP_WIDTH - DIFF_WIDTH
GLA_HEADS = 4
GLA_KEY_WIDTH = GLA_WIDTH // 2
GLA_DK = GLA_KEY_WIDTH // GLA_HEADS
GLA_DV = GLA_WIDTH // GLA_HEADS
GLA_GATE_RANK = 16
GLA_TAU = 16.0
GLA_CHUNK = 64
FFN_HIDDEN = -(-8 * D_MODEL // (3 * 256)) * 256
EPS = 1e-6

IN_SPLITS = [GMLP_WIDTH, GMLP_WIDTH,
             DIFF_WIDTH, DIFF_WIDTH, DIFF_WIDTH,
             GLA_KEY_WIDTH, GLA_KEY_WIDTH,
             GLA_WIDTH, GLA_WIDTH,
             GLA_GATE_RANK]
IN_WIDTH = sum(IN_SPLITS)
IN_SPLIT_IDX = list(np.cumsum(IN_SPLITS)[:-1].tolist())

kernel_name = "hymba_style_gmlp_diffattn_gla_hybrid"


def rmsnorm(x, g):
    xf = x.astype(jnp.float32)
    y = xf * lax.rsqrt(jnp.mean(xf * xf, axis=-1, keepdims=True) + EPS) * g.astype(jnp.float32)
    return y.astype(x.dtype)


def layernorm(x, g):
    xf = x.astype(jnp.float32)
    mu = jnp.mean(xf, axis=-1, keepdims=True)
    var = jnp.mean(jnp.square(xf - mu), axis=-1, keepdims=True)
    return ((xf - mu) * lax.rsqrt(var + EPS) * g.astype(jnp.float32)).astype(x.dtype)


def gmlp_mixer(u, v, ln_g, spatial_w, spatial_b):
    u = jax.nn.gelu(u)
    v = layernorm(jax.nn.gelu(v), ln_g)
    B, T, _ = v.shape
    n = T // GMLP_CHUNK
    vc = v.reshape(B, n, GMLP_CHUNK, GMLP_GROUPS, GMLP_GROUP_DIM)
    causal = jnp.tril(jnp.ones((GMLP_CHUNK, GMLP_CHUNK), dtype=bool))
    w = jnp.where(causal[None], spatial_w, jnp.zeros_like(spatial_w))
    mixed = jnp.einsum('gts,bnsgc->bntgc', w, vc) + spatial_b.T[:, :, None]
    return u * mixed.reshape(B, T, GMLP_WIDTH)


def diff_attention(q, k, v, lambdas, norm_g, lambda_init):
    B, T, _ = q.shape
    q = q.reshape(B, T, DIFF_HEADS, 2, DIFF_HEAD_DIM)
    k = k.reshape(B, T, DIFF_HEADS, 2, DIFF_HEAD_DIM)
    v = v.reshape(B, T, DIFF_HEADS, 2 * DIFF_HEAD_DIM)
    lf = lambdas.astype(jnp.float32)
    lam = jnp.exp(jnp.sum(lf[0] * lf[1])) - jnp.exp(jnp.sum(lf[2] * lf[3])) + lambda_init
    nb = T // ATTN_BLOCK
    qb = q.reshape(B, nb, ATTN_BLOCK, DIFF_HEADS, 2, DIFF_HEAD_DIM).transpose(1, 0, 2, 3, 4, 5)
    k_pos = jnp.arange(T)
    scale = DIFF_HEAD_DIM ** -0.5

    def block(args):
        qi, i = args
        s = jnp.einsum('bqhmd,bkhmd->bhmqk', qi, k).astype(jnp.float32) * scale
        q_pos = i * ATTN_BLOCK + jnp.arange(ATTN_BLOCK)
        mask = k_pos[None, :] <= q_pos[:, None]
        s = jnp.where(mask, s, -jnp.inf)
        p = jax.nn.softmax(s, axis=-1)
        a = p[:, :, 0] - lam * p[:, :, 1]
        return jnp.einsum('bhqk,bkhe->bqhe', a.astype(v.dtype), v)

    o = lax.map(block, (qb, jnp.arange(nb)))
    o = o.transpose(1, 0, 2, 3, 4).reshape(B, T, DIFF_HEADS, 2 * DIFF_HEAD_DIM)
    o = rmsnorm(o, norm_g) * (1.0 - lambda_init)
    return o.reshape(B, T, DIFF_WIDTH)


def gla_mixer(q, k, v, r, gate_code, gate_w2, gate_b, norm_g):
    B, T, _ = q.shape
    z = gate_code @ gate_w2 + gate_b
    g = jax.nn.log_sigmoid(z.astype(jnp.float32)) / GLA_TAU
    q = q.reshape(B, T, GLA_HEADS, GLA_DK) * (GLA_DK ** -0.5)
    k = k.reshape(B, T, GLA_HEADS, GLA_DK)
    vv = v.reshape(B, T, GLA_HEADS, GLA_DV)
    g = g.reshape(B, T, GLA_HEADS, GLA_DK)
    n = T // GLA_CHUNK

    def to_chunks(t):
        return t.reshape(B, n, GLA_CHUNK, GLA_HEADS, -1).transpose(1, 0, 3, 2, 4)

    qc, kc, vc, gc = to_chunks(q), to_chunks(k), to_chunks(vv), to_chunks(g)
    causal = jnp.tril(jnp.ones((GLA_CHUNK, GLA_CHUNK), dtype=bool))

    def step(S, inp):
        qi, ki, vi, gi = inp
        qf, kf, vf = qi.astype(jnp.float32), ki.astype(jnp.float32), vi.astype(jnp.float32)
        b = jnp.cumsum(gi, axis=2)
        o_inter = jnp.einsum('bhcd,bhde->bhce', qf * jnp.exp(b), S)
        diff = b[:, :, :, None, :] - b[:, :, None, :, :]
        decay = jnp.exp(jnp.where(causal[:, :, None], diff, -jnp.inf))
        A = jnp.einsum('bhid,bhjd,bhijd->bhij', qf, kf, decay)
        o_intra = jnp.einsum('bhij,bhje->bhie', A, vf)
        b_last = b[:, :, -1:, :]
        S = jnp.exp(b_last[:, :, 0, :])[..., None] * S + \
            jnp.einsum('bhcd,bhce->bhde', kf * jnp.exp(b_last - b), vf)
        return S, o_inter + o_intra

    S0 = jnp.zeros((B, GLA_HEADS, GLA_DK, GLA_DV), jnp.float32)
    _, o = lax.scan(step, S0, (qc, kc, vc, gc))
    o = o.transpose(1, 0, 3, 2, 4).reshape(B, T, GLA_HEADS, GLA_DV)
    o = rmsnorm(o, norm_g).astype(v.dtype).reshape(B, T, GLA_WIDTH)
    return o * jax.nn.silu(r)


def swiglu(h, w_in, w_out):
    gate, up = jnp.split(h @ w_in, 2, axis=-1)
    return (jax.nn.silu(gate) * up) @ w_out


def setup_inputs(seed: int = 0) -> dict:
    key = jax.random.key(seed)
    ks = jax.random.split(key, 16)
    f32 = jnp.float32
    nrm = lambda k, shape, s: jax.random.normal(k, shape, f32) * s
    return {
        "x": nrm(ks[0], (BATCH, SEQ, D_MODEL), 1.0),
        "norm1_g": 1.0 + nrm(ks[1], (DEPTH, D_MODEL), 0.02),
        "w_in": nrm(ks[2], (DEPTH, D_MODEL, IN_WIDTH), D_MODEL ** -0.5),
        "gmlp_ln_g": 1.0 + nrm(ks[3], (DEPTH, GMLP_WIDTH), 0.02),
        "spatial_w": nrm(ks[4], (DEPTH, GMLP_GROUPS, GMLP_CHUNK, GMLP_CHUNK), GMLP_CHUNK ** -0.5),
        "spatial_b": 1.0 + nrm(ks[5], (DEPTH, GMLP_GROUPS, GMLP_CHUNK), 0.1),
        "diff_lambdas": nrm(ks[6], (DEPTH, 4, DIFF_HEAD_DIM), 0.1),
        "diff_norm_g": 1.0 + nrm(ks[7], (DEPTH, 2 * DIFF_HEAD_DIM), 0.02),
        "gla_gate_w2": nrm(ks[8], (DEPTH, GLA_GATE_RANK, GLA_KEY_WIDTH), GLA_GATE_RANK ** -0.5),
        "gla_gate_b": nrm(ks[9], (DEPTH, GLA_KEY_WIDTH), 0.1),
        "gla_norm_g": 1.0 + nrm(ks[10], (DEPTH, GLA_DV), 0.02),
        "w_out": nrm(ks[11], (DEPTH, MIX_WIDTH, D_MODEL), MIX_WIDTH ** -0.5),
        "norm2_g": 1.0 + nrm(ks[12], (DEPTH, D_MODEL), 0.02),
        "w_ffn_in": nrm(ks[13], (DEPTH, D_MODEL, 2 * FFN_HIDDEN), D_MODEL ** -0.5),
        "w_ffn_out": nrm(ks[14], (DEPTH, FFN_HIDDEN, D_MODEL), FFN_HIDDEN ** -0.5),
        "final_g": 1.0 + nrm(ks[15], (D_MODEL,), 0.02),
    }


def reference(x, norm1_g, w_in, gmlp_ln_g, spatial_w, spatial_b, diff_lambdas, diff_norm_g,
              gla_gate_w2, gla_gate_b, gla_norm_g, w_out, norm2_g, w_ffn_in, w_ffn_out, final_g):
    for l in range(DEPTH):
        lambda_init = 0.8 - 0.6 * math.exp(-0.3 * l)
        h = rmsnorm(x, norm1_g[l])
        (u, v, dq, dk, dv, gq, gk, gv, gr, gcode) = jnp.split(h @ w_in[l], IN_SPLIT_IDX, axis=-1)
        a_out = gmlp_mixer(u, v, gmlp_ln_g[l], spatial_w[l], spatial_b[l])
        b_out = diff_attention(dq, dk, dv, diff_lambdas[l], diff_norm_g[l], lambda_init)
        c_out = gla_mixer(gq, gk, gv, gr, gcode, gla_gate_w2[l], gla_gate_b[l], gla_norm_g[l])
        mix = jnp.concatenate([a_out, b_out, c_out], axis=-1) @ w_out[l]
        x = x + mix.astype(x.dtype)
        x = x + swiglu(rmsnorm(x, norm2_g[l]), w_ffn_in[l], w_ffn_out[l]).astype(x.dtype)
    return rmsnorm(x, final_g)
```

```python
import functools
import math

import jax
import jax.numpy as jnp
from jax import lax
from jax.experimental import pallas as pl
from jax.experimental.pallas import tpu as pltpu

F32 = jnp.float32
BF16 = jnp.bfloat16

D_MODEL = 2048
DEPTH = 4
EPS = 1e-6
LANES = 128

GMLP_WIDTH = 512
GMLP_GROUPS = 4
GMLP_CHUNK = 128
DIFF_WIDTH = 768
DIFF_HEADS = 6
DIFF_HEAD_DIM = 64
GLA_HEADS = 4
GLA_DK = 96
GLA_DV = 192
GLA_DK_PAD = 128
GLA_DV_PAD = 256
GLA_KEY_WIDTH = GLA_HEADS * GLA_DK
GLA_WIDTH = GLA_HEADS * GLA_DV
GLA_RANK = 16
GLA_TAU = 16.0
GLA_CHUNK = 64
FFN_HIDDEN = 5632

OFF_GV = 0
OFF_GR = OFF_GV + GLA_HEADS * GLA_DV_PAD
OFF_GQ = OFF_GR + GLA_HEADS * GLA_DV_PAD
OFF_GK = OFF_GQ + GLA_HEADS * GLA_DK_PAD
OFF_U = OFF_GK + GLA_HEADS * GLA_DK_PAD
OFF_V = OFF_U + GMLP_WIDTH
OFF_DQ = OFF_V + GMLP_WIDTH
OFF_DK = OFF_DQ + DIFF_WIDTH
OFF_DV = OFF_DK + DIFF_WIDTH
OFF_CODE = OFF_DV + DIFF_WIDTH
PROJ_TN = 512
PROJ_WIDTH = -(-(OFF_CODE + LANES) // PROJ_TN) * PROJ_TN
MIX_C_WIDTH = GLA_HEADS * GLA_DV_PAD

VMEM_LIMIT = 56 * 1024 * 1024


def _cparams(sem):
    return pltpu.CompilerParams(dimension_semantics=sem, vmem_limit_bytes=VMEM_LIMIT)


def _rmsnorm_rows(x_ref, g_ref, h_ref, rows):
    tm = x_ref.shape[0]
    g = g_ref[...]
    for r in range(0, tm, rows):
        x = x_ref[r:r + rows, :]
        ms = jnp.mean(x * x, axis=-1, keepdims=True)
        h_ref[r:r + rows, :] = (x * lax.rsqrt(ms + EPS) * g).astype(BF16)


def _norm_mm_kernel(x_ref, g_ref, w_ref, o_ref, h_ref):
    @pl.when(pl.program_id(1) == 0)
    def _():
        _rmsnorm_rows(x_ref, g_ref, h_ref, 256)

    o_ref[...] = jnp.dot(h_ref[...], w_ref[...], preferred_element_type=F32).astype(o_ref.dtype)


def _norm_swiglu_kernel(x_ref, g_ref, wg_ref, wu_ref, o_ref, h_ref):
    @pl.when(pl.program_id(1) == 0)
    def _():
        _rmsnorm_rows(x_ref, g_ref, h_ref, 256)

    h = h_ref[...]
    gate = jnp.dot(h, wg_ref[...], preferred_element_type=F32)
    up = jnp.dot(h, wu_ref[...], preferred_element_type=F32)
    o_ref[...] = (gate / (1.0 + jnp.exp(-gate)) * up).astype(o_ref.dtype)


def _norm_matmul(x, g, w, tm, tn):
    t, d = x.shape
    n = w.shape[1]
    return pl.pallas_call(
        _norm_mm_kernel,
        out_shape=jax.ShapeDtypeStruct((t, n), BF16),
        grid=(t // tm, n // tn),
        in_specs=[
            pl.BlockSpec((tm, d), lambda i, j: (i, 0)),
            pl.BlockSpec((1, d), lambda i, j: (0, 0)),
            pl.BlockSpec((d, tn), lambda i, j: (0, j)),
        ],
        out_specs=pl.BlockSpec((tm, tn), lambda i, j: (i, j)),
        scratch_shapes=[pltpu.VMEM((tm, d), BF16)],
        compiler_params=_cparams(("parallel", "arbitrary")),
        name="norm_in_proj",
    )(x, g, w)


def _norm_swiglu(x, g, w, tm, tn):
    t, d = x.shape
    hidden = w.shape[1] // 2
    nj = hidden // tn
    return pl.pallas_call(
        _norm_swiglu_kernel,
        out_shape=jax.ShapeDtypeStruct((t, hidden), BF16),
        grid=(t // tm, nj),
        in_specs=[
            pl.BlockSpec((tm, d), lambda i, j: (i, 0)),
            pl.BlockSpec((1, d), lambda i, j: (0, 0)),
            pl.BlockSpec((d, tn), lambda i, j: (0, j)),
            pl.BlockSpec((d, tn), lambda i, j: (0, j + nj)),
        ],
        out_specs=pl.BlockSpec((tm, tn), lambda i, j: (i, j)),
        scratch_shapes=[pltpu.VMEM((tm, d), BF16)],
        compiler_params=_cparams(("parallel", "arbitrary")),
        name="norm_ffn_in",
    )(x, g, w, w)


def _mm_res_kernel(*refs, n_a):
    a_refs, w_refs = refs[:n_a], refs[n_a:2 * n_a]
    r_ref, o_ref = refs[2 * n_a], refs[2 * n_a + 1]
    acc = r_ref[...]
    for a_ref, w_ref in zip(a_refs, w_refs):
        acc = acc + jnp.dot(a_ref[...], w_ref[...], preferred_element_type=F32)
    o_ref[...] = acc


def _matmul_residual(a_list, w_list, res, tm, tn, name):
    t, n = res.shape
    n_a = len(a_list)
    in_specs = [pl.BlockSpec((tm, a.shape[1]), lambda i, j: (i, 0)) for a in a_list]
    in_specs += [pl.BlockSpec((w.shape[0], tn), lambda i, j: (0, j)) for w in w_list]
    in_specs += [pl.BlockSpec((tm, tn), lambda i, j: (i, j))]
    return pl.pallas_call(
        functools.partial(_mm_res_kernel, n_a=n_a),
        out_shape=jax.ShapeDtypeStruct((t, n), F32),
        grid=(t // tm, n // tn),
        in_specs=in_specs,
        out_specs=pl.BlockSpec((tm, tn), lambda i, j: (i, j)),
        compiler_params=_cparams(("parallel", "arbitrary")),
        name=name,
    )(*a_list, *w_list, res)


def _gelu_tanh(x):
    return 0.5 * x * (1.0 + jnp.tanh(math.sqrt(2.0 / math.pi) * (x + 0.044715 * (x * x * x))))


def _gmlp_kernel(u_ref, v_ref, lng_ref, w_ref, b_ref, o_ref):
    tile = u_ref.shape[0]
    gd = GMLP_WIDTH // GMLP_GROUPS
    v = _gelu_tanh(v_ref[...].astype(F32))
    mu = jnp.mean(v, axis=-1, keepdims=True)
    vc = v - mu
    var = jnp.mean(vc * vc, axis=-1, keepdims=True)
    vn = (vc * lax.rsqrt(var + EPS) * lng_ref[...]).astype(BF16)
    row = lax.broadcasted_iota(jnp.int32, (GMLP_CHUNK, GMLP_CHUNK), 0)
    col = lax.broadcasted_iota(jnp.int32, (GMLP_CHUNK, GMLP_CHUNK), 1)
    causal = col <= row
    bias = b_ref[...]
    for g in range(GMLP_GROUPS):
        wg = jnp.where(causal, w_ref[g], 0.0).astype(BF16)
        bg = bias[:, g:g + 1]
        for c in range(tile // GMLP_CHUNK):
            rs = slice(c * GMLP_CHUNK, (c + 1) * GMLP_CHUNK)
            cs = slice(g * gd, (g + 1) * gd)
            mixed = jnp.dot(wg, vn[rs, cs], preferred_element_type=F32) + bg
            u = _gelu_tanh(u_ref[rs, cs].astype(F32))
            o_ref[rs, cs] = (u * mixed).astype(o_ref.dtype)


def _gmlp_mixer(proj, ln_g, spatial_w, spatial_b_t, tile):
    t = proj.shape[0]
    return pl.pallas_call(
        _gmlp_kernel,
        out_shape=jax.ShapeDtypeStruct((t, GMLP_WIDTH), BF16),
        grid=(t // tile,),
        in_specs=[
            pl.BlockSpec((tile, GMLP_WIDTH), lambda i: (i, OFF_U // GMLP_WIDTH)),
            pl.BlockSpec((tile, GMLP_WIDTH), lambda i: (i, OFF_V // GMLP_WIDTH)),
            pl.BlockSpec((1, GMLP_WIDTH), lambda i: (0, 0)),
            pl.BlockSpec((GMLP_GROUPS, GMLP_CHUNK, GMLP_CHUNK), lambda i: (0, 0, 0)),
            pl.BlockSpec((GMLP_CHUNK, GMLP_GROUPS), lambda i: (0, 0)),
        ],
        out_specs=pl.BlockSpec((tile, GMLP_WIDTH), lambda i: (i, 0)),
        compiler_params=_cparams(("parallel",)),
        name="gmlp_mixer",
    )(proj, proj, ln_g, spatial_w, spatial_b_t)


def _diff_attn_kernel(lam_ref, q_ref, k_ref, v_ref, g_ref, o_ref, m_ref, l_ref, acc_ref, *, blk, lambda_init):
    i = pl.program_id(1)
    lane = lax.broadcasted_iota(jnp.int32, (blk, LANES), 1)
    q = q_ref[...] * jnp.asarray(DIFF_HEAD_DIM ** -0.5, BF16)
    zero = jnp.zeros_like(q)
    q_halves = (jnp.where(lane < DIFF_HEAD_DIM, q, zero), jnp.where(lane >= DIFF_HEAD_DIM, q, zero))
    m_ref[...] = jnp.full(m_ref.shape, -jnp.inf, F32)
    l_ref[...] = jnp.zeros(l_ref.shape, F32)
    acc_ref[...] = jnp.zeros(acc_ref.shape, F32)

    def step(j, masked):
        start = pl.multiple_of(j * blk, blk)
        kb = k_ref[pl.ds(start, blk), :]
        vb = v_ref[pl.ds(start, blk), :]
        for h in range(2):
            s = lax.dot_general(q_halves[h], kb, (((1,), (1,)), ((), ())), preferred_element_type=F32)
            if masked:
                row = lax.broadcasted_iota(jnp.int32, (blk, blk), 0)
                col = lax.broadcasted_iota(jnp.int32, (blk, blk), 1)
                s = jnp.where(col <= row, s, -jnp.inf)
            m_prev = m_ref[h]
            m_new = jnp.maximum(m_prev, jnp.max(s, axis=1, keepdims=True))
            alpha = jnp.exp(m_prev - m_new)
            p = jnp.exp(s - pltpu.repeat(m_new, blk // LANES, axis=1))
            l_ref[h] = alpha * l_ref[h] + jnp.sum(p, axis=1, keepdims=True)
            acc_ref[h] = alpha * acc_ref[h] + jnp.dot(p.astype(BF16), vb, preferred_element_type=F32)
            m_ref[h] = m_new

    def body(j, carry):
        step(j, False)
        return carry

    lax.fori_loop(0, i, body, 0)
    step(i, True)

    lf = lam_ref[...]
    lam = (jnp.exp(jnp.sum(lf[0:1] * lf[1:2], axis=1, keepdims=True))
           - jnp.exp(jnp.sum(lf[2:3] * lf[3:4], axis=1, keepdims=True)) + lambda_init)
    o = acc_ref[0] / l_ref[0] - lam * (acc_ref[1] / l_ref[1])
    ms = jnp.mean(o * o, axis=-1, keepdims=True)
    o = o * lax.rsqrt(ms + EPS) * g_ref[...] * (1.0 - lambda_init)
    o_ref[...] = o.astype(o_ref.dtype)


def _diff_attention(proj, lambdas, norm_g, lambda_init, blk):
    t = proj.shape[0]
    qo, ko, vo = OFF_DQ // LANES, OFF_DK // LANES, OFF_DV // LANES
    return pl.pallas_call(
        functools.partial(_diff_attn_kernel, blk=blk, lambda_init=lambda_init),
        out_shape=jax.ShapeDtypeStruct((t, DIFF_WIDTH), BF16),
        grid=(DIFF_HEADS, t // blk),
        in_specs=[
            pl.BlockSpec((4, DIFF_HEAD_DIM), lambda h, i: (0, 0)),
            pl.BlockSpec((blk, LANES), lambda h, i: (i, qo + h)),
            pl.BlockSpec((t, LANES), lambda h, i: (0, ko + h)),
            pl.BlockSpec((t, LANES), lambda h, i: (0, vo + h)),
            pl.BlockSpec((1, LANES), lambda h, i: (0, 0)),
        ],
        out_specs=pl.BlockSpec((blk, LANES), lambda h, i: (i, h)),
        scratch_shapes=[
            pltpu.VMEM((2, blk, LANES), F32),
            pltpu.VMEM((2, blk, LANES), F32),
            pltpu.VMEM((2, blk, LANES), F32),
        ],
        compiler_params=_cparams(("parallel", "arbitrary")),
        name="diff_attention",
    )(lambdas, proj, proj, proj, norm_g)


def _gla_kernel(gv_ref, gr_ref, gq_ref, gk_ref, code_ref, w2_ref, gb_ref, ng_ref, o_ref, st_ref, g_ref):
    tile = gq_ref.shape[0]
    c_len = GLA_CHUNK

    @pl.when(pl.program_id(0) == 0)
    def _():
        st_ref[...] = jnp.zeros(st_ref.shape, F32)

    z = jnp.dot(code_ref[...], w2_ref[...], preferred_element_type=F32) + gb_ref[...]
    g_ref[...] = (jnp.minimum(z, 0.0) - jnp.log1p(jnp.exp(-jnp.abs(z)))) * (1.0 / GLA_TAU)

    row = lax.broadcasted_iota(jnp.int32, (c_len, c_len), 0)
    col = lax.broadcasted_iota(jnp.int32, (c_len, c_len), 1)
    causal = col <= row
    tri = jnp.where(causal, 1.0, 0.0).astype(BF16)
    ng = ng_ref[...]

    def chunk(c, carry):
        r0 = pl.multiple_of(c * c_len, c_len)
        g = g_ref[pl.ds(r0, c_len), :]
        g_hi = g.astype(BF16)
        g_lo = (g - g_hi.astype(F32)).astype(BF16)
        b = (jnp.dot(tri, g_hi, preferred_element_type=F32)
             + jnp.dot(tri, g_lo, preferred_element_type=F32))
        b_last = b[c_len - 1:c_len, :]
        q = gq_ref[pl.ds(r0, c_len), :].astype(F32) * (GLA_DK ** -0.5)
        k = gk_ref[pl.ds(r0, c_len), :].astype(F32)
        qd = (q * jnp.exp(b)).astype(BF16)
        kd = (k * jnp.exp(-b)).astype(BF16)
        kl = (k * jnp.exp(b_last - b)).astype(BF16)
        e_last = jnp.exp(b_last)
        for h in range(GLA_HEADS):
            ks = slice(h * GLA_DK_PAD, (h + 1) * GLA_DK_PAD)
            vs = slice(h * GLA_DV_PAD, (h + 1) * GLA_DV_PAD)
            vh = gv_ref[pl.ds(r0, c_len), vs]
            st = st_ref[h]
            a = lax.dot_general(qd[:, ks], kd[:, ks], (((1,), (1,)), ((), ())), preferred_element_type=F32)
            a = jnp.where(causal, a, 0.0).astype(BF16)
            o = jnp.dot(a, vh, preferred_element_type=F32)
            o = o + lax.dot_general(qd[:, ks], st.astype(BF16), (((1,), (1,)), ((), ())),
                                    preferred_element_type=F32)
            st_ref[h] = st * e_last[:, ks] + lax.dot_general(
                vh, kl[:, ks], (((0,), (0,)), ((), ())), preferred_element_type=F32)
            ms = jnp.sum(o * o, axis=-1, keepdims=True) * (1.0 / GLA_DV)
            on = o * lax.rsqrt(ms + EPS) * ng
            r = gr_ref[pl.ds(r0, c_len), vs].astype(F32)
            o_ref[pl.ds(r0, c_len), vs] = (on * (r / (1.0 + jnp.exp(-r)))).astype(o_ref.dtype)
        return carry

    lax.fori_loop(0, tile // c_len, chunk, 0)


def _gla_mixer(proj, w2p, gbp, ngp, tile):
    t = proj.shape[0]
    kw = GLA_HEADS * GLA_DK_PAD
    return pl.pallas_call(
        _gla_kernel,
        out_shape=jax.ShapeDtypeStruct((t, MIX_C_WIDTH), BF16),
        grid=(t // tile,),
        in_specs=[
            pl.BlockSpec((tile, MIX_C_WIDTH), lambda i: (i, OFF_GV // MIX_C_WIDTH)),
            pl.BlockSpec((tile, MIX_C_WIDTH), lambda i: (i, OFF_GR // MIX_C_WIDTH)),
            pl.BlockSpec((tile, kw), lambda i: (i, OFF_GQ // kw)),
            pl.BlockSpec((tile, kw), lambda i: (i, OFF_GK // kw)),
            pl.BlockSpec((tile, LANES), lambda i: (i, OFF_CODE // LANES)),
            pl.BlockSpec((LANES, kw), lambda i: (0, 0)),
            pl.BlockSpec((1, kw), lambda i: (0, 0)),
            pl.BlockSpec((1, GLA_DV_PAD), lambda i: (0, 0)),
        ],
        out_specs=pl.BlockSpec((tile, MIX_C_WIDTH), lambda i: (i, 0)),
        scratch_shapes=[
            pltpu.VMEM((GLA_HEADS, GLA_DV_PAD, GLA_DK_PAD), F32),
            pltpu.VMEM((tile, kw), F32),
        ],
        compiler_params=_cparams(("arbitrary",)),
        name="gla_mixer",
    )(proj, proj, proj, proj, proj, w2p, gbp, ngp)


def _final_norm_kernel(x_ref, g_ref, o_ref):
    x = x_ref[...]
    ms = jnp.mean(x * x, axis=-1, keepdims=True)
    o_ref[...] = x * lax.rsqrt(ms + EPS) * g_ref[...]


def _final_norm(x, g, tm):
    t, d = x.shape
    return pl.pallas_call(
        _final_norm_kernel,
        out_shape=jax.ShapeDtypeStruct((t, d), F32),
        grid=(t // tm,),
        in_specs=[pl.BlockSpec((tm, d), lambda i: (i, 0)), pl.BlockSpec((1, d), lambda i: (0, 0))],
        out_specs=pl.BlockSpec((tm, d), lambda i: (i, 0)),
        compiler_params=_cparams(("parallel",)),
        name="final_norm",
    )(x, g)


def _pad_heads(w, heads, d, d_pad):
    lead = w.shape[:-1]
    w = w.reshape(lead + (heads, d))
    w = jnp.pad(w, [(0, 0)] * len(lead) + [(0, 0), (0, d_pad - d)])
    return w.reshape(lead + (heads * d_pad,))


def _layout_w_in(w_in):
    splits = [GMLP_WIDTH, GMLP_WIDTH, DIFF_WIDTH, DIFF_WIDTH, DIFF_WIDTH,
              GLA_KEY_WIDTH, GLA_KEY_WIDTH, GLA_WIDTH, GLA_WIDTH, GLA_RANK]
    idx = [0]
    for s in splits:
        idx.append(idx[-1] + s)
    u, v, dq, dk, dv, gq, gk, gv, gr, code = (w_in[..., idx[n]:idx[n + 1]] for n in range(10))
    code = jnp.pad(code, [(0, 0)] * (code.ndim - 1) + [(0, PROJ_WIDTH - OFF_CODE - GLA_RANK)])
    parts = [_pad_heads(gv, GLA_HEADS, GLA_DV, GLA_DV_PAD), _pad_heads(gr, GLA_HEADS, GLA_DV, GLA_DV_PAD),
             _pad_heads(gq, GLA_HEADS, GLA_DK, GLA_DK_PAD), _pad_heads(gk, GLA_HEADS, GLA_DK, GLA_DK_PAD),
             u, v, dq, dk, dv, code]
    return jnp.concatenate(parts, axis=-1).astype(BF16)


def _layout_w_out(w_out):
    wa = w_out[:, :GMLP_WIDTH, :].astype(BF16)
    wb = w_out[:, GMLP_WIDTH:GMLP_WIDTH + DIFF_WIDTH, :].astype(BF16)
    wc = w_out[:, GMLP_WIDTH + DIFF_WIDTH:, :]
    depth, _, n = wc.shape
    wc = wc.reshape(depth, GLA_HEADS, GLA_DV, n)
    wc = jnp.pad(wc, [(0, 0), (0, 0), (0, GLA_DV_PAD - GLA_DV), (0, 0)])
    return wa, wb, wc.reshape(depth, MIX_C_WIDTH, n).astype(BF16)


def kernel(x, norm1_g, w_in, gmlp_ln_g, spatial_w, spatial_b, diff_lambdas, diff_norm_g, gla_gate_w2,
           gla_gate_b, gla_norm_g, w_out, norm2_g, w_ffn_in, w_ffn_out, final_g):
    b, t, d = x.shape
    assert b == 1 and d == D_MODEL
    xs = x.reshape(t, d)

    w1 = _layout_w_in(w_in)
    wa, wb, wc = _layout_w_out(w_out)
    wf_in = w_ffn_in.astype(BF16)
    wf_out = w_ffn_out.astype(BF16)
    w2p = jnp.pad(_pad_heads(gla_gate_w2, GLA_HEADS, GLA_DK, GLA_DK_PAD),
                  [(0, 0), (0, LANES - GLA_RANK), (0, 0)]).astype(BF16)
    gbp = _pad_heads(gla_gate_b, GLA_HEADS, GLA_DK, GLA_DK_PAD)[:, None, :]
    ngp = jnp.pad(gla_norm_g, [(0, 0), (0, GLA_DV_PAD - GLA_DV)])[:, None, :]
    sb_t = jnp.swapaxes(spatial_b, 1, 2)

    tm = min(1024, t)
    for l in range(DEPTH):
        lambda_init = 0.8 - 0.6 * math.exp(-0.3 * l)
        proj = _norm_matmul(xs, norm1_g[l][None, :], w1[l], tm, PROJ_TN)
        a_out = _gmlp_mixer(proj, gmlp_ln_g[l][None, :], spatial_w[l], sb_t[l], min(512, t))
        b_out = _diff_attention(proj, diff_lambdas[l], diff_norm_g[l][None, :], lambda_init, min(512, t))
        c_out = _gla_mixer(proj, w2p[l], gbp[l], ngp[l], min(256, t))
        xs = _matmul_residual([a_out, b_out, c_out], [wa[l], wb[l], wc[l]], xs, min(512, t), d, "out_proj")
        hid = _norm_swiglu(xs, norm2_g[l][None, :], wf_in[l], tm, 512)
        xs = _matmul_residual([hid], [wf_out[l]], xs, tm, 512, "ffn_out")
    out = _final_norm(xs, final_g[None, :], min(512, t))
    return out.reshape(b, t, d)
```

```python
import functools
import math

import jax
import jax.numpy as jnp
from jax import lax
from jax.experimental import pallas as pl
from jax.experimental.pallas import tpu as pltpu

F32 = jnp.float32
BF16 = jnp.bfloat16

D_MODEL = 2048
DEPTH = 4
EPS = 1e-6
LANES = 128

GMLP_WIDTH = 512
GMLP_GROUPS = 4
GMLP_CHUNK = 128
DIFF_WIDTH = 768
DIFF_HEADS = 6
DIFF_HEAD_DIM = 64
GLA_HEADS = 4
GLA_DK = 96
GLA_DV = 192
GLA_DK_PAD = 128
GLA_DV_PAD = 256
GLA_KEY_WIDTH = GLA_HEADS * GLA_DK
GLA_WIDTH = GLA_HEADS * GLA_DV
GLA_RANK = 16
GLA_TAU = 16.0
GLA_CHUNK = 64
FFN_HIDDEN = 5632

OFF_GV = 0
OFF_GR = OFF_GV + GLA_HEADS * GLA_DV_PAD
OFF_GQ = OFF_GR + GLA_HEADS * GLA_DV_PAD
OFF_GK = OFF_GQ + GLA_HEADS * GLA_DK_PAD
OFF_U = OFF_GK + GLA_HEADS * GLA_DK_PAD
OFF_V = OFF_U + GMLP_WIDTH
OFF_DQ = OFF_V + GMLP_WIDTH
OFF_DK = OFF_DQ + DIFF_WIDTH
OFF_DV = OFF_DK + DIFF_WIDTH
OFF_CODE = OFF_DV + DIFF_WIDTH
PROJ_TN = 512
PROJ_WIDTH = -(-(OFF_CODE + LANES) // PROJ_TN) * PROJ_TN
MIX_C_WIDTH = GLA_HEADS * GLA_DV_PAD

VMEM_LIMIT = 56 * 1024 * 1024


def _cparams(sem):
    return pltpu.CompilerParams(dimension_semantics=sem, vmem_limit_bytes=VMEM_LIMIT)


def _rmsnorm_rows(x_ref, g_ref, h_ref, rows):
    tm = x_ref.shape[0]
    g = g_ref[...]
    for r in range(0, tm, rows):
        x = x_ref[r:r + rows, :]
        ms = jnp.mean(x * x, axis=-1, keepdims=True)
        h_ref[r:r + rows, :] = (x * lax.rsqrt(ms + EPS) * g).astype(BF16)


def _norm_mm_kernel(x_ref, g_ref, w_ref, cs_ref, o_ref, h_ref):
    @pl.when(pl.program_id(1) == 0)
    def _():
        _rmsnorm_rows(x_ref, g_ref, h_ref, 256)

    acc = jnp.dot(h_ref[...], w_ref[...], preferred_element_type=F32)
    o_ref[...] = (acc * cs_ref[...]).astype(o_ref.dtype)


def _norm_swiglu_kernel(x_ref, g_ref, wg_ref, wu_ref, o_ref, h_ref):
    @pl.when(pl.program_id(1) == 0)
    def _():
        _rmsnorm_rows(x_ref, g_ref, h_ref, 256)

    h = h_ref[...]
    gate = jnp.dot(h, wg_ref[...], preferred_element_type=F32)
    up = jnp.dot(h, wu_ref[...], preferred_element_type=F32)
    o_ref[...] = (gate / (1.0 + jnp.exp(-gate)) * up).astype(o_ref.dtype)


def _norm_matmul(x, g, w, col_scale, tm, tn):
    t, d = x.shape
    n = w.shape[1]
    return pl.pallas_call(
        _norm_mm_kernel,
        out_shape=jax.ShapeDtypeStruct((t, n), BF16),
        grid=(t // tm, n // tn),
        in_specs=[
            pl.BlockSpec((tm, d), lambda i, j: (i, 0)),
            pl.BlockSpec((1, d), lambda i, j: (0, 0)),
            pl.BlockSpec((d, tn), lambda i, j: (0, j)),
            pl.BlockSpec((1, tn), lambda i, j: (0, j)),
        ],
        out_specs=pl.BlockSpec((tm, tn), lambda i, j: (i, j)),
        scratch_shapes=[pltpu.VMEM((tm, d), BF16)],
        compiler_params=_cparams(("parallel", "arbitrary")),
        name="norm_in_proj",
    )(x, g, w, col_scale)


def _norm_swiglu(x, g, w, tm, tn):
    t, d = x.shape
    hidden = w.shape[1] // 2
    nj = hidden // tn
    return pl.pallas_call(
        _norm_swiglu_kernel,
        out_shape=jax.ShapeDtypeStruct((t, hidden), BF16),
        grid=(t // tm, nj),
        in_specs=[
            pl.BlockSpec((tm, d), lambda i, j: (i, 0)),
            pl.BlockSpec((1, d), lambda i, j: (0, 0)),
            pl.BlockSpec((d, tn), lambda i, j: (0, j)),
            pl.BlockSpec((d, tn), lambda i, j: (0, j + nj)),
        ],
        out_specs=pl.BlockSpec((tm, tn), lambda i, j: (i, j)),
        scratch_shapes=[pltpu.VMEM((tm, d), BF16)],
        compiler_params=_cparams(("parallel", "arbitrary")),
        name="norm_ffn_in",
    )(x, g, w, w)


def _mm_res_kernel(*refs, n_a):
    a_refs, w_refs = refs[:n_a], refs[n_a:2 * n_a]
    r_ref, o_ref = refs[2 * n_a], refs[2 * n_a + 1]
    acc = r_ref[...]
    for a_ref, w_ref in zip(a_refs, w_refs):
        acc = acc + jnp.dot(a_ref[...], w_ref[...], preferred_element_type=F32)
    o_ref[...] = acc


def _matmul_residual(a_list, w_list, res, tm, tn, name):
    t, n = res.shape
    n_a = len(a_list)
    in_specs = [pl.BlockSpec((tm, a.shape[1]), lambda i, j: (i, 0)) for a in a_list]
    in_specs += [pl.BlockSpec((w.shape[0], tn), lambda i, j: (0, j)) for w in w_list]
    in_specs += [pl.BlockSpec((tm, tn), lambda i, j: (i, j))]
    return pl.pallas_call(
        functools.partial(_mm_res_kernel, n_a=n_a),
        out_shape=jax.ShapeDtypeStruct((t, n), F32),
        grid=(t // tm, n // tn),
        in_specs=in_specs,
        out_specs=pl.BlockSpec((tm, tn), lambda i, j: (i, j)),
        compiler_params=_cparams(("parallel", "arbitrary")),
        name=name,
    )(*a_list, *w_list, res)


def _gelu_tanh(x):
    return 0.5 * x * (1.0 + jnp.tanh(math.sqrt(2.0 / math.pi) * (x + 0.044715 * (x * x * x))))


def _gmlp_kernel(u_ref, v_ref, lng_ref, w_ref, b_ref, o_ref):
    tile = u_ref.shape[0]
    gd = GMLP_WIDTH // GMLP_GROUPS
    v = _gelu_tanh(v_ref[...].astype(F32))
    mu = jnp.mean(v, axis=-1, keepdims=True)
    vc = v - mu
    var = jnp.mean(vc * vc, axis=-1, keepdims=True)
    vn = (vc * lax.rsqrt(var + EPS) * lng_ref[...]).astype(BF16)
    row = lax.broadcasted_iota(jnp.int32, (GMLP_CHUNK, GMLP_CHUNK), 0)
    col = lax.broadcasted_iota(jnp.int32, (GMLP_CHUNK, GMLP_CHUNK), 1)
    causal = col <= row
    bias = b_ref[...]
    for g in range(GMLP_GROUPS):
        wg = jnp.where(causal, w_ref[g], 0.0).astype(BF16)
        bg = bias[:, g:g + 1]
        for c in range(tile // GMLP_CHUNK):
            rs = slice(c * GMLP_CHUNK, (c + 1) * GMLP_CHUNK)
            cs = slice(g * gd, (g + 1) * gd)
            mixed = jnp.dot(wg, vn[rs, cs], preferred_element_type=F32) + bg
            u = _gelu_tanh(u_ref[rs, cs].astype(F32))
            o_ref[rs, cs] = (u * mixed).astype(o_ref.dtype)


def _gmlp_mixer(proj, ln_g, spatial_w, spatial_b_t, tile):
    t = proj.shape[0]
    return pl.pallas_call(
        _gmlp_kernel,
        out_shape=jax.ShapeDtypeStruct((t, GMLP_WIDTH), BF16),
        grid=(t // tile,),
        in_specs=[
            pl.BlockSpec((tile, GMLP_WIDTH), lambda i: (i, OFF_U // GMLP_WIDTH)),
            pl.BlockSpec((tile, GMLP_WIDTH), lambda i: (i, OFF_V // GMLP_WIDTH)),
            pl.BlockSpec((1, GMLP_WIDTH), lambda i: (0, 0)),
            pl.BlockSpec((GMLP_GROUPS, GMLP_CHUNK, GMLP_CHUNK), lambda i: (0, 0, 0)),
            pl.BlockSpec((GMLP_CHUNK, GMLP_GROUPS), lambda i: (0, 0)),
        ],
        out_specs=pl.BlockSpec((tile, GMLP_WIDTH), lambda i: (i, 0)),
        compiler_params=_cparams(("parallel",)),
        name="gmlp_mixer",
    )(proj, proj, ln_g, spatial_w, spatial_b_t)


def _diff_attn_kernel(lam_ref, q_ref, k_ref, v_ref, g_ref, o_ref, vt_ref, qt_ref, sa_ref, sb_ref, m_ref, l_ref,
                      acc_ref, *, blk, lambda_init):
    i = pl.program_id(1)
    n_blk = vt_ref.shape[0]

    @pl.when(i == 0)
    def _():
        def transpose_v(c, carry):
            r0 = pl.multiple_of(c * blk, blk)
            vt_ref[c] = v_ref[pl.ds(r0, blk), :].astype(F32).T.astype(BF16)
            return carry
        lax.fori_loop(0, n_blk, transpose_v, 0)

    qt = q_ref[...].astype(F32).T
    sub = lax.broadcasted_iota(jnp.int32, (LANES, blk), 0)
    qt_ref[0] = jnp.where(sub < DIFF_HEAD_DIM, qt, 0.0).astype(BF16)
    qt_ref[1] = jnp.where(sub >= DIFF_HEAD_DIM, qt, 0.0).astype(BF16)
    m_ref[...] = jnp.full(m_ref.shape, -jnp.inf, F32)
    l_ref[...] = jnp.zeros(l_ref.shape, F32)
    acc_ref[...] = jnp.zeros(acc_ref.shape, F32)

    def scores(j, s_ref):
        kb = k_ref[pl.ds(pl.multiple_of(j * blk, blk), blk), :]
        for h in range(2):
            s_ref[h] = jnp.dot(kb, qt_ref[h], preferred_element_type=F32)

    def accumulate(j, s_ref, masked):
        vtb = vt_ref[j]
        for h in range(2):
            s = s_ref[h]
            if masked:
                row = lax.broadcasted_iota(jnp.int32, (blk, blk), 0)
                col = lax.broadcasted_iota(jnp.int32, (blk, blk), 1)
                s = jnp.where(row <= col, s, -jnp.inf)
            m_prev = m_ref[h]
            m_new = jnp.maximum(m_prev, jnp.max(s, axis=0, keepdims=True))
            alpha = jnp.exp2(m_prev - m_new)
            p = jnp.exp2(s - m_new)
            l_ref[h] = alpha * l_ref[h] + jnp.sum(p, axis=0, keepdims=True)
            acc_ref[h] = alpha * acc_ref[h] + jnp.dot(vtb, p.astype(BF16), preferred_element_type=F32)
            m_ref[h] = m_new

    scores(0, sa_ref)

    def pair(p, carry):
        j = 2 * p
        scores(j + 1, sb_ref)
        accumulate(j, sa_ref, False)
        scores(j + 2, sa_ref)
        accumulate(j + 1, sb_ref, False)
        return carry

    lax.fori_loop(0, i // 2, pair, 0)

    @pl.when(i % 2 == 1)
    def _():
        scores(i, sb_ref)
        accumulate(i - 1, sa_ref, False)
        accumulate(i, sb_ref, True)

    @pl.when(i % 2 == 0)
    def _():
        accumulate(i, sa_ref, True)

    lf = lam_ref[...]
    lam = (jnp.exp(jnp.sum(lf[0:1] * lf[1:2], axis=1, keepdims=True))
           - jnp.exp(jnp.sum(lf[2:3] * lf[3:4], axis=1, keepdims=True)) + lambda_init)
    o = (acc_ref[0] / l_ref[0] - lam * (acc_ref[1] / l_ref[1])).T
    ms = jnp.mean(o * o, axis=-1, keepdims=True)
    o = o * lax.rsqrt(ms + EPS) * g_ref[...] * (1.0 - lambda_init)
    o_ref[...] = o.astype(o_ref.dtype)


def _diff_attention(proj, lambdas, norm_g, lambda_init, blk):
    t = proj.shape[0]
    qo, ko, vo = OFF_DQ // LANES, OFF_DK // LANES, OFF_DV // LANES
    return pl.pallas_call(
        functools.partial(_diff_attn_kernel, blk=blk, lambda_init=lambda_init),
        out_shape=jax.ShapeDtypeStruct((t, DIFF_WIDTH), BF16),
        grid=(DIFF_HEADS, t // blk),
        in_specs=[
            pl.BlockSpec((4, DIFF_HEAD_DIM), lambda h, i: (0, 0)),
            pl.BlockSpec((blk, LANES), lambda h, i: (i, qo + h)),
            pl.BlockSpec((t, LANES), lambda h, i: (0, ko + h)),
            pl.BlockSpec((t, LANES), lambda h, i: (0, vo + h)),
            pl.BlockSpec((1, LANES), lambda h, i: (0, 0)),
        ],
        out_specs=pl.BlockSpec((blk, LANES), lambda h, i: (i, h)),
        scratch_shapes=[
            pltpu.VMEM((t // blk, LANES, blk), BF16),
            pltpu.VMEM((2, LANES, blk), BF16),
            pltpu.VMEM((2, blk, blk), F32),
            pltpu.VMEM((2, blk, blk), F32),
            pltpu.VMEM((2, 1, blk), F32),
            pltpu.VMEM((2, 1, blk), F32),
            pltpu.VMEM((2, LANES, blk), F32),
        ],
        compiler_params=_cparams(("arbitrary", "arbitrary")),
        name="diff_attention",
    )(lambdas, proj, proj, proj, norm_g)


def _gla_kernel(gv_ref, gr_ref, gq_ref, gk_ref, code_ref, w2_ref, gb_ref, ng_ref, o_ref, st_ref, g_ref):
    tile = gq_ref.shape[0]
    c_len = GLA_CHUNK

    @pl.when(pl.program_id(0) == 0)
    def _():
        st_ref[...] = jnp.zeros(st_ref.shape, F32)

    z = jnp.dot(code_ref[...], w2_ref[...], preferred_element_type=F32) + gb_ref[...]
    g_ref[...] = (jnp.minimum(z, 0.0) - jnp.log1p(jnp.exp(-jnp.abs(z)))) * (1.0 / GLA_TAU)

    row = lax.broadcasted_iota(jnp.int32, (c_len, c_len), 0)
    col = lax.broadcasted_iota(jnp.int32, (c_len, c_len), 1)
    causal = col <= row
    tri = jnp.where(causal, 1.0, 0.0).astype(BF16)
    ng = ng_ref[...]

    def chunk(c, carry):
        r0 = pl.multiple_of(c * c_len, c_len)
        g = g_ref[pl.ds(r0, c_len), :]
        g_hi = g.astype(BF16)
        g_lo = (g - g_hi.astype(F32)).astype(BF16)
        b = (jnp.dot(tri, g_hi, preferred_element_type=F32)
             + jnp.dot(tri, g_lo, preferred_element_type=F32))
        b_last = b[c_len - 1:c_len, :]
        q = gq_ref[pl.ds(r0, c_len), :].astype(F32) * (GLA_DK ** -0.5)
        k = gk_ref[pl.ds(r0, c_len), :].astype(F32)
        qd = (q * jnp.exp(b)).astype(BF16)
        kd = (k * jnp.exp(-b)).astype(BF16)
        kl = (k * jnp.exp(b_last - b)).astype(BF16)
        e_last = jnp.exp(b_last)
        for h in range(GLA_HEADS):
            ks = slice(h * GLA_DK_PAD, (h + 1) * GLA_DK_PAD)
            vs = slice(h * GLA_DV_PAD, (h + 1) * GLA_DV_PAD)
            vh = gv_ref[pl.ds(r0, c_len), vs]
            st = st_ref[h]
            a = lax.dot_general(qd[:, ks], kd[:, ks], (((1,), (1,)), ((), ())), preferred_element_type=F32)
            a = jnp.where(causal, a, 0.0).astype(BF16)
            o = jnp.dot(a, vh, preferred_element_type=F32)
            o = o + lax.dot_general(qd[:, ks], st.astype(BF16), (((1,), (1,)), ((), ())),
                                    preferred_element_type=F32)
            st_ref[h] = st * e_last[:, ks] + lax.dot_general(
                vh, kl[:, ks], (((0,), (0,)), ((), ())), preferred_element_type=F32)
            ms = jnp.sum(o * o, axis=-1, keepdims=True) * (1.0 / GLA_DV)
            on = o * lax.rsqrt(ms + EPS) * ng
            r = gr_ref[pl.ds(r0, c_len), vs].astype(F32)
            o_ref[pl.ds(r0, c_len), vs] = (on * (r / (1.0 + jnp.exp(-r)))).astype(o_ref.dtype)
        return carry

    lax.fori_loop(0, tile // c_len, chunk, 0)


def _gla_mixer(proj, w2p, gbp, ngp, tile):
    t = proj.shape[0]
    kw = GLA_HEADS * GLA_DK_PAD
    return pl.pallas_call(
        _gla_kernel,
        out_shape=jax.ShapeDtypeStruct((t, MIX_C_WIDTH), BF16),
        grid=(t // tile,),
        in_specs=[
            pl.BlockSpec((tile, MIX_C_WIDTH), lambda i: (i, OFF_GV // MIX_C_WIDTH)),
            pl.BlockSpec((tile, MIX_C_WIDTH), lambda i: (i, OFF_GR // MIX_C_WIDTH)),
            pl.BlockSpec((tile, kw), lambda i: (i, OFF_GQ // kw)),
            pl.BlockSpec((tile, kw), lambda i: (i, OFF_GK // kw)),
            pl.BlockSpec((tile, LANES), lambda i: (i, OFF_CODE // LANES)),
            pl.BlockSpec((LANES, kw), lambda i: (0, 0)),
            pl.BlockSpec((1, kw), lambda i: (0, 0)),
            pl.BlockSpec((1, GLA_DV_PAD), lambda i: (0, 0)),
        ],
        out_specs=pl.BlockSpec((tile, MIX_C_WIDTH), lambda i: (i, 0)),
        scratch_shapes=[
            pltpu.VMEM((GLA_HEADS, GLA_DV_PAD, GLA_DK_PAD), F32),
            pltpu.VMEM((tile, kw), F32),
        ],
        compiler_params=_cparams(("arbitrary",)),
        name="gla_mixer",
    )(proj, proj, proj, proj, proj, w2p, gbp, ngp)


def _final_norm_kernel(x_ref, g_ref, o_ref):
    x = x_ref[...]
    ms = jnp.mean(x * x, axis=-1, keepdims=True)
    o_ref[...] = x * lax.rsqrt(ms + EPS) * g_ref[...]


def _final_norm(x, g, tm):
    t, d = x.shape
    return pl.pallas_call(
        _final_norm_kernel,
        out_shape=jax.ShapeDtypeStruct((t, d), F32),
        grid=(t // tm,),
        in_specs=[pl.BlockSpec((tm, d), lambda i: (i, 0)), pl.BlockSpec((1, d), lambda i: (0, 0))],
        out_specs=pl.BlockSpec((tm, d), lambda i: (i, 0)),
        compiler_params=_cparams(("parallel",)),
        name="final_norm",
    )(x, g)


def _pad_heads(w, heads, d, d_pad):
    lead = w.shape[:-1]
    w = w.reshape(lead + (heads, d))
    w = jnp.pad(w, [(0, 0)] * len(lead) + [(0, 0), (0, d_pad - d)])
    return w.reshape(lead + (heads * d_pad,))


def _layout_w_in(w_in):
    splits = [GMLP_WIDTH, GMLP_WIDTH, DIFF_WIDTH, DIFF_WIDTH, DIFF_WIDTH,
              GLA_KEY_WIDTH, GLA_KEY_WIDTH, GLA_WIDTH, GLA_WIDTH, GLA_RANK]
    idx = [0]
    for s in splits:
        idx.append(idx[-1] + s)
    u, v, dq, dk, dv, gq, gk, gv, gr, code = (w_in[..., idx[n]:idx[n + 1]] for n in range(10))
    code = jnp.pad(code, [(0, 0)] * (code.ndim - 1) + [(0, PROJ_WIDTH - OFF_CODE - GLA_RANK)])
    parts = [_pad_heads(gv, GLA_HEADS, GLA_DV, GLA_DV_PAD), _pad_heads(gr, GLA_HEADS, GLA_DV, GLA_DV_PAD),
             _pad_heads(gq, GLA_HEADS, GLA_DK, GLA_DK_PAD), _pad_heads(gk, GLA_HEADS, GLA_DK, GLA_DK_PAD),
             u, v, dq, dk, dv, code]
    return jnp.concatenate(parts, axis=-1).astype(BF16)


def _layout_w_out(w_out):
    wa = w_out[:, :GMLP_WIDTH, :].astype(BF16)
    wb = w_out[:, GMLP_WIDTH:GMLP_WIDTH + DIFF_WIDTH, :].astype(BF16)
    wc = w_out[:, GMLP_WIDTH + DIFF_WIDTH:, :]
    depth, _, n = wc.shape
    wc = wc.reshape(depth, GLA_HEADS, GLA_DV, n)
    wc = jnp.pad(wc, [(0, 0), (0, 0), (0, GLA_DV_PAD - GLA_DV), (0, 0)])
    return wa, wb, wc.reshape(depth, MIX_C_WIDTH, n).astype(BF16)


def kernel(x, norm1_g, w_in, gmlp_ln_g, spatial_w, spatial_b, diff_lambdas, diff_norm_g, gla_gate_w2,
           gla_gate_b, gla_norm_g, w_out, norm2_g, w_ffn_in, w_ffn_out, final_g):
    b, t, d = x.shape
    assert b == 1 and d == D_MODEL
    xs = x.reshape(t, d)

    w1 = _layout_w_in(w_in)
    wa, wb, wc = _layout_w_out(w_out)
    wf_in = w_ffn_in.astype(BF16)
    wf_out = w_ffn_out.astype(BF16)
    w2p = jnp.pad(_pad_heads(gla_gate_w2, GLA_HEADS, GLA_DK, GLA_DK_PAD),
                  [(0, 0), (0, LANES - GLA_RANK), (0, 0)]).astype(BF16)
    gbp = _pad_heads(gla_gate_b, GLA_HEADS, GLA_DK, GLA_DK_PAD)[:, None, :]
    ngp = jnp.pad(gla_norm_g, [(0, 0), (0, GLA_DV_PAD - GLA_DV)])[:, None, :]
    sb_t = jnp.swapaxes(spatial_b, 1, 2)

    col = jnp.arange(PROJ_WIDTH)
    q_scale = (DIFF_HEAD_DIM ** -0.5) * math.log2(math.e)
    col_scale = jnp.where((col >= OFF_DQ) & (col < OFF_DK), q_scale, 1.0).astype(F32)[None, :]

    tm = min(1024, t)
    for l in range(DEPTH):
        lambda_init = 0.8 - 0.6 * math.exp(-0.3 * l)
        proj = _norm_matmul(xs, norm1_g[l][None, :], w1[l], col_scale, tm, PROJ_TN)
        a_out = _gmlp_mixer(proj, gmlp_ln_g[l][None, :], spatial_w[l], sb_t[l], min(512, t))
        b_out = _diff_attention(proj, diff_lambdas[l], diff_norm_g[l][None, :], lambda_init, min(512, t))
        c_out = _gla_mixer(proj, w2p[l], gbp[l], ngp[l], min(256, t))
        xs = _matmul_residual([a_out, b_out, c_out], [wa[l], wb[l], wc[l]], xs, min(512, t), d, "out_proj")
        hid = _norm_swiglu(xs, norm2_g[l][None, :], wf_in[l], tm, 512)
        xs = _matmul_residual([hid], [wf_out[l]], xs, tm, 512, "ffn_out")
    out = _final_norm(xs, final_g[None, :], min(512, t))
    return out.reshape(b, t, d)
```

```python
import functools
import math

import jax
import jax.numpy as jnp
from jax import lax
from jax.experimental import pallas as pl
from jax.experimental.pallas import tpu as pltpu

F32 = jnp.float32
BF16 = jnp.bfloat16

D_MODEL = 2048
DEPTH = 4
EPS = 1e-6
LANES = 128

GMLP_WIDTH = 512
GMLP_GROUPS = 4
GMLP_CHUNK = 128
DIFF_WIDTH = 768
DIFF_HEADS = 6
DIFF_HEAD_DIM = 64
GLA_HEADS = 4
GLA_DK = 96
GLA_DV = 192
GLA_DK_PAD = 128
GLA_DV_PAD = 256
GLA_KEY_WIDTH = GLA_HEADS * GLA_DK
GLA_WIDTH = GLA_HEADS * GLA_DV
GLA_RANK = 16
GLA_TAU = 16.0
GLA_CHUNK = 64
FFN_HIDDEN = 5632

OFF_GV = 0
OFF_GR = OFF_GV + GLA_HEADS * GLA_DV_PAD
OFF_GQ = OFF_GR + GLA_HEADS * GLA_DV_PAD
OFF_GK = OFF_GQ + GLA_HEADS * GLA_DK_PAD
OFF_U = OFF_GK + GLA_HEADS * GLA_DK_PAD
OFF_V = OFF_U + GMLP_WIDTH
OFF_DQ = OFF_V + GMLP_WIDTH
OFF_DK = OFF_DQ + DIFF_WIDTH
OFF_DV = OFF_DK + DIFF_WIDTH
OFF_CODE = OFF_DV + DIFF_WIDTH
PROJ_TN = 512
PROJ_WIDTH = -(-(OFF_CODE + LANES) // PROJ_TN) * PROJ_TN
MIX_C_WIDTH = GLA_HEADS * GLA_DV_PAD

VMEM_LIMIT = 56 * 1024 * 1024


def _cparams(sem):
    return pltpu.CompilerParams(dimension_semantics=sem, vmem_limit_bytes=VMEM_LIMIT)


def _rmsnorm_rows(x_ref, g_ref, h_ref, rows):
    tm = x_ref.shape[0]
    g = g_ref[...]
    for r in range(0, tm, rows):
        x = x_ref[r:r + rows, :]
        ms = jnp.mean(x * x, axis=-1, keepdims=True)
        h_ref[r:r + rows, :] = (x * lax.rsqrt(ms + EPS) * g).astype(BF16)


def _norm_mm_kernel(x_ref, g_ref, w_ref, cs_ref, o_ref, h_ref):
    @pl.when(pl.program_id(1) == 0)
    def _():
        _rmsnorm_rows(x_ref, g_ref, h_ref, 256)

    acc = jnp.dot(h_ref[...], w_ref[...], preferred_element_type=F32)
    o_ref[...] = (acc * cs_ref[...]).astype(o_ref.dtype)


def _norm_swiglu_kernel(x_ref, g_ref, wg_ref, wu_ref, o_ref, h_ref):
    @pl.when(pl.program_id(1) == 0)
    def _():
        _rmsnorm_rows(x_ref, g_ref, h_ref, 256)

    h = h_ref[...]
    gate = jnp.dot(h, wg_ref[...], preferred_element_type=F32)
    up = jnp.dot(h, wu_ref[...], preferred_element_type=F32)
    o_ref[...] = (gate / (1.0 + jnp.exp(-gate)) * up).astype(o_ref.dtype)


def _norm_matmul(x, g, w, col_scale, tm, tn):
    t, d = x.shape
    n = w.shape[1]
    return pl.pallas_call(
        _norm_mm_kernel,
        out_shape=jax.ShapeDtypeStruct((t, n), BF16),
        grid=(t // tm, n // tn),
        in_specs=[
            pl.BlockSpec((tm, d), lambda i, j: (i, 0)),
            pl.BlockSpec((1, d), lambda i, j: (0, 0)),
            pl.BlockSpec((d, tn), lambda i, j: (0, j)),
            pl.BlockSpec((1, tn), lambda i, j: (0, j)),
        ],
        out_specs=pl.BlockSpec((tm, tn), lambda i, j: (i, j)),
        scratch_shapes=[pltpu.VMEM((tm, d), BF16)],
        compiler_params=_cparams(("parallel", "arbitrary")),
        name="norm_in_proj",
    )(x, g, w, col_scale)


def _norm_swiglu(x, g, w, tm, tn):
    t, d = x.shape
    hidden = w.shape[1] // 2
    nj = hidden // tn
    return pl.pallas_call(
        _norm_swiglu_kernel,
        out_shape=jax.ShapeDtypeStruct((t, hidden), BF16),
        grid=(t // tm, nj),
        in_specs=[
            pl.BlockSpec((tm, d), lambda i, j: (i, 0)),
            pl.BlockSpec((1, d), lambda i, j: (0, 0)),
            pl.BlockSpec((d, tn), lambda i, j: (0, j)),
            pl.BlockSpec((d, tn), lambda i, j: (0, j + nj)),
        ],
        out_specs=pl.BlockSpec((tm, tn), lambda i, j: (i, j)),
        scratch_shapes=[pltpu.VMEM((tm, d), BF16)],
        compiler_params=_cparams(("parallel", "arbitrary")),
        name="norm_ffn_in",
    )(x, g, w, w)


def _mm_res_kernel(*refs, n_a):
    a_refs, w_refs = refs[:n_a], refs[n_a:2 * n_a]
    r_ref, o_ref = refs[2 * n_a], refs[2 * n_a + 1]
    acc = r_ref[...]
    for a_ref, w_ref in zip(a_refs, w_refs):
        acc = acc + jnp.dot(a_ref[...], w_ref[...], preferred_element_type=F32)
    o_ref[...] = acc


def _matmul_residual(a_list, w_list, res, tm, tn, name):
    t, n = res.shape
    n_a = len(a_list)
    in_specs = [pl.BlockSpec((tm, a.shape[1]), lambda i, j: (i, 0)) for a in a_list]
    in_specs += [pl.BlockSpec((w.shape[0], tn), lambda i, j: (0, j)) for w in w_list]
    in_specs += [pl.BlockSpec((tm, tn), lambda i, j: (i, j))]
    return pl.pallas_call(
        functools.partial(_mm_res_kernel, n_a=n_a),
        out_shape=jax.ShapeDtypeStruct((t, n), F32),
        grid=(t // tm, n // tn),
        in_specs=in_specs,
        out_specs=pl.BlockSpec((tm, tn), lambda i, j: (i, j)),
        compiler_params=_cparams(("parallel", "arbitrary")),
        name=name,
    )(*a_list, *w_list, res)


def _gelu_tanh(x):
    return 0.5 * x * (1.0 + jnp.tanh(math.sqrt(2.0 / math.pi) * (x + 0.044715 * (x * x * x))))


def _gmlp_kernel(u_ref, v_ref, lng_ref, w_ref, b_ref, o_ref):
    tile = u_ref.shape[0]
    gd = GMLP_WIDTH // GMLP_GROUPS
    v = _gelu_tanh(v_ref[...].astype(F32))
    mu = jnp.mean(v, axis=-1, keepdims=True)
    vc = v - mu
    var = jnp.mean(vc * vc, axis=-1, keepdims=True)
    vn = (vc * lax.rsqrt(var + EPS) * lng_ref[...]).astype(BF16)
    row = lax.broadcasted_iota(jnp.int32, (GMLP_CHUNK, GMLP_CHUNK), 0)
    col = lax.broadcasted_iota(jnp.int32, (GMLP_CHUNK, GMLP_CHUNK), 1)
    causal = col <= row
    bias = b_ref[...]
    for g in range(GMLP_GROUPS):
        wg = jnp.where(causal, w_ref[g], 0.0).astype(BF16)
        bg = bias[:, g:g + 1]
        for c in range(tile // GMLP_CHUNK):
            rs = slice(c * GMLP_CHUNK, (c + 1) * GMLP_CHUNK)
            cs = slice(g * gd, (g + 1) * gd)
            mixed = jnp.dot(wg, vn[rs, cs], preferred_element_type=F32) + bg
            u = _gelu_tanh(u_ref[rs, cs].astype(F32))
            o_ref[rs, cs] = (u * mixed).astype(o_ref.dtype)


def _gmlp_mixer(proj, ln_g, spatial_w, spatial_b_t, tile):
    t = proj.shape[0]
    return pl.pallas_call(
        _gmlp_kernel,
        out_shape=jax.ShapeDtypeStruct((t, GMLP_WIDTH), BF16),
        grid=(t // tile,),
        in_specs=[
            pl.BlockSpec((tile, GMLP_WIDTH), lambda i: (i, OFF_U // GMLP_WIDTH)),
            pl.BlockSpec((tile, GMLP_WIDTH), lambda i: (i, OFF_V // GMLP_WIDTH)),
            pl.BlockSpec((1, GMLP_WIDTH), lambda i: (0, 0)),
            pl.BlockSpec((GMLP_GROUPS, GMLP_CHUNK, GMLP_CHUNK), lambda i: (0, 0, 0)),
            pl.BlockSpec((GMLP_CHUNK, GMLP_GROUPS), lambda i: (0, 0)),
        ],
        out_specs=pl.BlockSpec((tile, GMLP_WIDTH), lambda i: (i, 0)),
        compiler_params=_cparams(("parallel",)),
        name="gmlp_mixer",
    )(proj, proj, ln_g, spatial_w, spatial_b_t)


def _diff_attn_kernel(lam_ref, q_ref, k_ref, v_ref, g_ref, o_ref, vt_ref, qt_ref, sa_ref, sb_ref, ca_ref, cb_ref,
                      m_ref, l_ref, acc_ref, *, blk, lambda_init):
    i = pl.program_id(1)
    n_blk = vt_ref.shape[0]

    @pl.when(i == 0)
    def _():
        def transpose_v(c, carry):
            r0 = pl.multiple_of(c * blk, blk)
            vt_ref[c] = v_ref[pl.ds(r0, blk), :].astype(F32).T.astype(BF16)
            return carry
        lax.fori_loop(0, n_blk, transpose_v, 0)

    qt = q_ref[...].astype(F32).T
    sub = lax.broadcasted_iota(jnp.int32, (LANES, blk), 0)
    qt_ref[0] = jnp.where(sub < DIFF_HEAD_DIM, qt, 0.0).astype(BF16)
    qt_ref[1] = jnp.where(sub >= DIFF_HEAD_DIM, qt, 0.0).astype(BF16)
    m_ref[...] = jnp.full(m_ref.shape, -jnp.inf, F32)
    l_ref[...] = jnp.zeros(l_ref.shape, F32)
    acc_ref[...] = jnp.zeros(acc_ref.shape, F32)

    def scores(j, s_ref, c_ref):
        kb = k_ref[pl.ds(pl.multiple_of(j * blk, blk), blk), :]
        for h in range(2):
            s = jnp.dot(kb, qt_ref[h], preferred_element_type=F32)
            s_ref[h] = s
            c_ref[h] = jnp.max(s, axis=0, keepdims=True)

    def accumulate(j, s_ref, c_ref, masked):
        vtb = vt_ref[j]
        for h in range(2):
            s = s_ref[h]
            if masked:
                row = lax.broadcasted_iota(jnp.int32, (blk, blk), 0)
                col = lax.broadcasted_iota(jnp.int32, (blk, blk), 1)
                s = jnp.where(row <= col, s, -jnp.inf)
                c = jnp.max(s, axis=0, keepdims=True)
            else:
                c = c_ref[h]
            m_prev = m_ref[h]
            m_new = jnp.maximum(m_prev, c)
            alpha = jnp.exp2(m_prev - m_new)
            p = jnp.exp2(s - m_new)
            l_ref[h] = alpha * l_ref[h] + jnp.sum(p, axis=0, keepdims=True)
            acc_ref[h] = alpha * acc_ref[h] + jnp.dot(vtb, p.astype(BF16), preferred_element_type=F32)
            m_ref[h] = m_new

    scores(0, sa_ref, ca_ref)

    def two_blocks(j):
        scores(j + 1, sb_ref, cb_ref)
        accumulate(j, sa_ref, ca_ref, False)
        scores(j + 2, sa_ref, ca_ref)
        accumulate(j + 1, sb_ref, cb_ref, False)

    def quad(p, carry):
        two_blocks(4 * p)
        two_blocks(4 * p + 2)
        return carry

    def pair(p, carry):
        two_blocks(4 * (i // 4) + 2 * p)
        return carry

    lax.fori_loop(0, i // 4, quad, 0)
    lax.fori_loop(0, (i % 4) // 2, pair, 0)

    @pl.when(i % 2 == 1)
    def _():
        scores(i, sb_ref, cb_ref)
        accumulate(i - 1, sa_ref, ca_ref, False)
        accumulate(i, sb_ref, cb_ref, True)

    @pl.when(i % 2 == 0)
    def _():
        accumulate(i, sa_ref, ca_ref, True)

    lf = lam_ref[...]
    lam = (jnp.exp(jnp.sum(lf[0:1] * lf[1:2], axis=1, keepdims=True))
           - jnp.exp(jnp.sum(lf[2:3] * lf[3:4], axis=1, keepdims=True)) + lambda_init)
    o = (acc_ref[0] / l_ref[0] - lam * (acc_ref[1] / l_ref[1])).T
    ms = jnp.mean(o * o, axis=-1, keepdims=True)
    o = o * lax.rsqrt(ms + EPS) * g_ref[...] * (1.0 - lambda_init)
    o_ref[...] = o.astype(o_ref.dtype)


def _diff_attention(proj, lambdas, norm_g, lambda_init, blk):
    t = proj.shape[0]
    qo, ko, vo = OFF_DQ // LANES, OFF_DK // LANES, OFF_DV // LANES
    return pl.pallas_call(
        functools.partial(_diff_attn_kernel, blk=blk, lambda_init=lambda_init),
        out_shape=jax.ShapeDtypeStruct((t, DIFF_WIDTH), BF16),
        grid=(DIFF_HEADS, t // blk),
        in_specs=[
            pl.BlockSpec((4, DIFF_HEAD_DIM), lambda h, i: (0, 0)),
            pl.BlockSpec((blk, LANES), lambda h, i: (i, qo + h)),
            pl.BlockSpec((t, LANES), lambda h, i: (0, ko + h)),
            pl.BlockSpec((t, LANES), lambda h, i: (0, vo + h)),
            pl.BlockSpec((1, LANES), lambda h, i: (0, 0)),
        ],
        out_specs=pl.BlockSpec((blk, LANES), lambda h, i: (i, h)),
        scratch_shapes=[
            pltpu.VMEM((t // blk, LANES, blk), BF16),
            pltpu.VMEM((2, LANES, blk), BF16),
            pltpu.VMEM((2, blk, blk), F32),
            pltpu.VMEM((2, blk, blk), F32),
            pltpu.VMEM((2, 1, blk), F32),
            pltpu.VMEM((2, 1, blk), F32),
            pltpu.VMEM((2, 1, blk), F32),
            pltpu.VMEM((2, 1, blk), F32),
            pltpu.VMEM((2, LANES, blk), F32),
        ],
        compiler_params=_cparams(("arbitrary", "arbitrary")),
        name="diff_attention",
    )(lambdas, proj, proj, proj, norm_g)


def _gla_kernel(gv_ref, gr_ref, gq_ref, gk_ref, code_ref, w2_ref, gb_ref, ng_ref, o_ref, st_ref, g_ref):
    tile = gq_ref.shape[0]
    c_len = GLA_CHUNK

    @pl.when(pl.program_id(0) == 0)
    def _():
        st_ref[...] = jnp.zeros(st_ref.shape, F32)

    z = jnp.dot(code_ref[...], w2_ref[...], preferred_element_type=F32) + gb_ref[...]
    g_ref[...] = (jnp.minimum(z, 0.0) - jnp.log1p(jnp.exp(-jnp.abs(z)))) * (1.0 / GLA_TAU)

    row = lax.broadcasted_iota(jnp.int32, (c_len, c_len), 0)
    col = lax.broadcasted_iota(jnp.int32, (c_len, c_len), 1)
    causal = col <= row
    tri = jnp.where(causal, 1.0, 0.0).astype(BF16)
    ng = ng_ref[...]

    def chunk(c, carry):
        r0 = pl.multiple_of(c * c_len, c_len)
        g = g_ref[pl.ds(r0, c_len), :]
        g_hi = g.astype(BF16)
        g_lo = (g - g_hi.astype(F32)).astype(BF16)
        b = (jnp.dot(tri, g_hi, preferred_element_type=F32)
             + jnp.dot(tri, g_lo, preferred_element_type=F32))
        b_last = b[c_len - 1:c_len, :]
        q = gq_ref[pl.ds(r0, c_len), :].astype(F32) * (GLA_DK ** -0.5)
        k = gk_ref[pl.ds(r0, c_len), :].astype(F32)
        qd = (q * jnp.exp(b)).astype(BF16)
        kd = (k * jnp.exp(-b)).astype(BF16)
        kl = (k * jnp.exp(b_last - b)).astype(BF16)
        e_last = jnp.exp(b_last)
        for h in range(GLA_HEADS):
            ks = slice(h * GLA_DK_PAD, (h + 1) * GLA_DK_PAD)
            vs = slice(h * GLA_DV_PAD, (h + 1) * GLA_DV_PAD)
            vh = gv_ref[pl.ds(r0, c_len), vs]
            st = st_ref[h]
            a = lax.dot_general(qd[:, ks], kd[:, ks], (((1,), (1,)), ((), ())), preferred_element_type=F32)
            a = jnp.where(causal, a, 0.0).astype(BF16)
            o = jnp.dot(a, vh, preferred_element_type=F32)
            o = o + lax.dot_general(qd[:, ks], st.astype(BF16), (((1,), (1,)), ((), ())),
                                    preferred_element_type=F32)
            st_ref[h] = st * e_last[:, ks] + lax.dot_general(
                vh, kl[:, ks], (((0,), (0,)), ((), ())), preferred_element_type=F32)
            ms = jnp.sum(o * o, axis=-1, keepdims=True) * (1.0 / GLA_DV)
            on = o * lax.rsqrt(ms + EPS) * ng
            r = gr_ref[pl.ds(r0, c_len), vs].astype(F32)
            o_ref[pl.ds(r0, c_len), vs] = (on * (r / (1.0 + jnp.exp(-r)))).astype(o_ref.dtype)
        return carry

    lax.fori_loop(0, tile // c_len, chunk, 0, unroll=True)


def _gla_mixer(proj, w2p, gbp, ngp, tile):
    t = proj.shape[0]
    kw = GLA_HEADS * GLA_DK_PAD
    return pl.pallas_call(
        _gla_kernel,
        out_shape=jax.ShapeDtypeStruct((t, MIX_C_WIDTH), BF16),
        grid=(t // tile,),
        in_specs=[
            pl.BlockSpec((tile, MIX_C_WIDTH), lambda i: (i, OFF_GV // MIX_C_WIDTH)),
            pl.BlockSpec((tile, MIX_C_WIDTH), lambda i: (i, OFF_GR // MIX_C_WIDTH)),
            pl.BlockSpec((tile, kw), lambda i: (i, OFF_GQ // kw)),
            pl.BlockSpec((tile, kw), lambda i: (i, OFF_GK // kw)),
            pl.BlockSpec((tile, LANES), lambda i: (i, OFF_CODE // LANES)),
            pl.BlockSpec((LANES, kw), lambda i: (0, 0)),
            pl.BlockSpec((1, kw), lambda i: (0, 0)),
            pl.BlockSpec((1, GLA_DV_PAD), lambda i: (0, 0)),
        ],
        out_specs=pl.BlockSpec((tile, MIX_C_WIDTH), lambda i: (i, 0)),
        scratch_shapes=[
            pltpu.VMEM((GLA_HEADS, GLA_DV_PAD, GLA_DK_PAD), F32),
            pltpu.VMEM((tile, kw), F32),
        ],
        compiler_params=_cparams(("arbitrary",)),
        name="gla_mixer",
    )(proj, proj, proj, proj, proj, w2p, gbp, ngp)


def _final_norm_kernel(x_ref, g_ref, o_ref):
    x = x_ref[...]
    ms = jnp.mean(x * x, axis=-1, keepdims=True)
    o_ref[...] = x * lax.rsqrt(ms + EPS) * g_ref[...]


def _final_norm(x, g, tm):
    t, d = x.shape
    return pl.pallas_call(
        _final_norm_kernel,
        out_shape=jax.ShapeDtypeStruct((t, d), F32),
        grid=(t // tm,),
        in_specs=[pl.BlockSpec((tm, d), lambda i: (i, 0)), pl.BlockSpec((1, d), lambda i: (0, 0))],
        out_specs=pl.BlockSpec((tm, d), lambda i: (i, 0)),
        compiler_params=_cparams(("parallel",)),
        name="final_norm",
    )(x, g)


def _pad_heads(w, heads, d, d_pad):
    lead = w.shape[:-1]
    w = w.reshape(lead + (heads, d))
    w = jnp.pad(w, [(0, 0)] * len(lead) + [(0, 0), (0, d_pad - d)])
    return w.reshape(lead + (heads * d_pad,))


def _layout_w_in(w_in):
    splits = [GMLP_WIDTH, GMLP_WIDTH, DIFF_WIDTH, DIFF_WIDTH, DIFF_WIDTH,
              GLA_KEY_WIDTH, GLA_KEY_WIDTH, GLA_WIDTH, GLA_WIDTH, GLA_RANK]
    idx = [0]
    for s in splits:
        idx.append(idx[-1] + s)
    u, v, dq, dk, dv, gq, gk, gv, gr, code = (w_in[..., idx[n]:idx[n + 1]] for n in range(10))
    code = jnp.pad(code, [(0, 0)] * (code.ndim - 1) + [(0, PROJ_WIDTH - OFF_CODE - GLA_RANK)])
    parts = [_pad_heads(gv, GLA_HEADS, GLA_DV, GLA_DV_PAD), _pad_heads(gr, GLA_HEADS, GLA_DV, GLA_DV_PAD),
             _pad_heads(gq, GLA_HEADS, GLA_DK, GLA_DK_PAD), _pad_heads(gk, GLA_HEADS, GLA_DK, GLA_DK_PAD),
             u, v, dq, dk, dv, code]
    return jnp.concatenate(parts, axis=-1).astype(BF16)


def _layout_w_out(w_out):
    wa = w_out[:, :GMLP_WIDTH, :].astype(BF16)
    wb = w_out[:, GMLP_WIDTH:GMLP_WIDTH + DIFF_WIDTH, :].astype(BF16)
    wc = w_out[:, GMLP_WIDTH + DIFF_WIDTH:, :]
    depth, _, n = wc.shape
    wc = wc.reshape(depth, GLA_HEADS, GLA_DV, n)
    wc = jnp.pad(wc, [(0, 0), (0, 0), (0, GLA_DV_PAD - GLA_DV), (0, 0)])
    return wa, wb, wc.reshape(depth, MIX_C_WIDTH, n).astype(BF16)


def kernel(x, norm1_g, w_in, gmlp_ln_g, spatial_w, spatial_b, diff_lambdas, diff_norm_g, gla_gate_w2,
           gla_gate_b, gla_norm_g, w_out, norm2_g, w_ffn_in, w_ffn_out, final_g):
    b, t, d = x.shape
    assert b == 1 and d == D_MODEL
    xs = x.reshape(t, d)

    w1 = _layout_w_in(w_in)
    wa, wb, wc = _layout_w_out(w_out)
    wf_in = w_ffn_in.astype(BF16)
    wf_out = w_ffn_out.astype(BF16)
    w2p = jnp.pad(_pad_heads(gla_gate_w2, GLA_HEADS, GLA_DK, GLA_DK_PAD),
                  [(0, 0), (0, LANES - GLA_RANK), (0, 0)]).astype(BF16)
    gbp = _pad_heads(gla_gate_b, GLA_HEADS, GLA_DK, GLA_DK_PAD)[:, None, :]
    ngp = jnp.pad(gla_norm_g, [(0, 0), (0, GLA_DV_PAD - GLA_DV)])[:, None, :]
    sb_t = jnp.swapaxes(spatial_b, 1, 2)

    col = jnp.arange(PROJ_WIDTH)
    q_scale = (DIFF_HEAD_DIM ** -0.5) * math.log2(math.e)
    col_scale = jnp.where((col >= OFF_DQ) & (col < OFF_DK), q_scale, 1.0).astype(F32)[None, :]

    tm = min(1024, t)
    for l in range(DEPTH):
        lambda_init = 0.8 - 0.6 * math.exp(-0.3 * l)
        proj = _norm_matmul(xs, norm1_g[l][None, :], w1[l], col_scale, tm, PROJ_TN)
        a_out = _gmlp_mixer(proj, gmlp_ln_g[l][None, :], spatial_w[l], sb_t[l], min(512, t))
        b_out = _diff_attention(proj, diff_lambdas[l], diff_norm_g[l][None, :], lambda_init, min(512, t))
        c_out = _gla_mixer(proj, w2p[l], gbp[l], ngp[l], min(256, t))
        xs = _matmul_residual([a_out, b_out, c_out], [wa[l], wb[l], wc[l]], xs, min(512, t), d, "out_proj")
        hid = _norm_swiglu(xs, norm2_g[l][None, :], wf_in[l], tm, 512)
        xs = _matmul_residual([hid], [wf_out[l]], xs, tm, 512, "ffn_out")
    out = _final_norm(xs, final_g[None, :], min(512, t))
    return out.reshape(b, t, d)
```

```python
import functools
import math

import jax
import jax.numpy as jnp
from jax import lax
from jax.experimental import pallas as pl
from jax.experimental.pallas import tpu as pltpu

F32 = jnp.float32
BF16 = jnp.bfloat16

D_MODEL = 2048
DEPTH = 4
EPS = 1e-6
LANES = 128

GMLP_WIDTH = 512
GMLP_GROUPS = 4
GMLP_CHUNK = 128
DIFF_WIDTH = 768
DIFF_HEADS = 6
DIFF_HEAD_DIM = 64
GLA_HEADS = 4
GLA_DK = 96
GLA_DV = 192
GLA_DK_PAD = 128
GLA_DV_PAD = 256
GLA_KEY_WIDTH = GLA_HEADS * GLA_DK
GLA_WIDTH = GLA_HEADS * GLA_DV
GLA_RANK = 16
GLA_TAU = 16.0
GLA_CHUNK = 64
FFN_HIDDEN = 5632

OFF_GV = 0
OFF_GR = OFF_GV + GLA_HEADS * GLA_DV_PAD
OFF_GQ = OFF_GR + GLA_HEADS * GLA_DV_PAD
OFF_GK = OFF_GQ + GLA_HEADS * GLA_DK_PAD
OFF_U = OFF_GK + GLA_HEADS * GLA_DK_PAD
OFF_V = OFF_U + GMLP_WIDTH
OFF_DQ = OFF_V + GMLP_WIDTH
OFF_DK = OFF_DQ + DIFF_WIDTH
OFF_DV = OFF_DK + DIFF_WIDTH
OFF_CODE = OFF_DV + DIFF_WIDTH
PROJ_TN = 512
PROJ_WIDTH = -(-(OFF_CODE + LANES) // PROJ_TN) * PROJ_TN
MIX_C_WIDTH = GLA_HEADS * GLA_DV_PAD

VMEM_LIMIT = 56 * 1024 * 1024


def _cparams(sem):
    return pltpu.CompilerParams(dimension_semantics=sem, vmem_limit_bytes=VMEM_LIMIT)


def _rmsnorm_rows(x_ref, g_ref, h_ref, rows):
    tm = x_ref.shape[0]
    g = g_ref[...]
    for r in range(0, tm, rows):
        x = x_ref[r:r + rows, :]
        ms = jnp.mean(x * x, axis=-1, keepdims=True)
        h_ref[r:r + rows, :] = (x * lax.rsqrt(ms + EPS) * g).astype(BF16)


def _norm_mm_kernel(x_ref, g_ref, w_ref, cs_ref, o_ref, h_ref):
    @pl.when(pl.program_id(1) == 0)
    def _():
        _rmsnorm_rows(x_ref, g_ref, h_ref, 256)

    acc = jnp.dot(h_ref[...], w_ref[...], preferred_element_type=F32)
    o_ref[...] = (acc * cs_ref[...]).astype(o_ref.dtype)


def _norm_swiglu_kernel(x_ref, g_ref, wg_ref, wu_ref, o_ref, h_ref):
    @pl.when(pl.program_id(1) == 0)
    def _():
        _rmsnorm_rows(x_ref, g_ref, h_ref, 256)

    h = h_ref[...]
    gate = jnp.dot(h, wg_ref[...], preferred_element_type=F32)
    up = jnp.dot(h, wu_ref[...], preferred_element_type=F32)
    o_ref[...] = (gate / (1.0 + jnp.exp(-gate)) * up).astype(o_ref.dtype)


def _norm_matmul(x, g, w, col_scale, tm, tn):
    t, d = x.shape
    n = w.shape[1]
    return pl.pallas_call(
        _norm_mm_kernel,
        out_shape=jax.ShapeDtypeStruct((t, n), BF16),
        grid=(t // tm, n // tn),
        in_specs=[
            pl.BlockSpec((tm, d), lambda i, j: (i, 0)),
            pl.BlockSpec((1, d), lambda i, j: (0, 0)),
            pl.BlockSpec((d, tn), lambda i, j: (0, j)),
            pl.BlockSpec((1, tn), lambda i, j: (0, j)),
        ],
        out_specs=pl.BlockSpec((tm, tn), lambda i, j: (i, j)),
        scratch_shapes=[pltpu.VMEM((tm, d), BF16)],
        compiler_params=_cparams(("parallel", "arbitrary")),
        name="norm_in_proj",
    )(x, g, w, col_scale)


def _norm_swiglu(x, g, w, tm, tn):
    t, d = x.shape
    hidden = w.shape[1] // 2
    nj = hidden // tn
    return pl.pallas_call(
        _norm_swiglu_kernel,
        out_shape=jax.ShapeDtypeStruct((t, hidden), BF16),
        grid=(t // tm, nj),
        in_specs=[
            pl.BlockSpec((tm, d), lambda i, j: (i, 0)),
            pl.BlockSpec((1, d), lambda i, j: (0, 0)),
            pl.BlockSpec((d, tn), lambda i, j: (0, j)),
            pl.BlockSpec((d, tn), lambda i, j: (0, j + nj)),
        ],
        out_specs=pl.BlockSpec((tm, tn), lambda i, j: (i, j)),
        scratch_shapes=[pltpu.VMEM((tm, d), BF16)],
        compiler_params=_cparams(("parallel", "arbitrary")),
        name="norm_ffn_in",
    )(x, g, w, w)


def _mm_res_kernel(*refs, n_a):
    a_refs, w_refs = refs[:n_a], refs[n_a:2 * n_a]
    r_ref, o_ref = refs[2 * n_a], refs[2 * n_a + 1]
    acc = r_ref[...]
    for a_ref, w_ref in zip(a_refs, w_refs):
        acc = acc + jnp.dot(a_ref[...], w_ref[...], preferred_element_type=F32)
    o_ref[...] = acc


def _matmul_residual(a_list, w_list, res, tm, tn, name):
    t, n = res.shape
    n_a = len(a_list)
    in_specs = [pl.BlockSpec((tm, a.shape[1]), lambda i, j: (i, 0)) for a in a_list]
    in_specs += [pl.BlockSpec((w.shape[0], tn), lambda i, j: (0, j)) for w in w_list]
    in_specs += [pl.BlockSpec((tm, tn), lambda i, j: (i, j))]
    return pl.pallas_call(
        functools.partial(_mm_res_kernel, n_a=n_a),
        out_shape=jax.ShapeDtypeStruct((t, n), F32),
        grid=(t // tm, n // tn),
        in_specs=in_specs,
        out_specs=pl.BlockSpec((tm, tn), lambda i, j: (i, j)),
        compiler_params=_cparams(("parallel", "arbitrary")),
        name=name,
    )(*a_list, *w_list, res)


def _gelu_tanh(x):
    return 0.5 * x * (1.0 + jnp.tanh(math.sqrt(2.0 / math.pi) * (x + 0.044715 * (x * x * x))))


def _gmlp_kernel(u_ref, v_ref, lng_ref, w_ref, b_ref, o_ref):
    tile = u_ref.shape[0]
    gd = GMLP_WIDTH // GMLP_GROUPS
    v = _gelu_tanh(v_ref[...].astype(F32))
    mu = jnp.mean(v, axis=-1, keepdims=True)
    vc = v - mu
    var = jnp.mean(vc * vc, axis=-1, keepdims=True)
    vn = (vc * lax.rsqrt(var + EPS) * lng_ref[...]).astype(BF16)
    row = lax.broadcasted_iota(jnp.int32, (GMLP_CHUNK, GMLP_CHUNK), 0)
    col = lax.broadcasted_iota(jnp.int32, (GMLP_CHUNK, GMLP_CHUNK), 1)
    causal = col <= row
    bias = b_ref[...]
    for g in range(GMLP_GROUPS):
        wg = jnp.where(causal, w_ref[g], 0.0).astype(BF16)
        bg = bias[:, g:g + 1]
        for c in range(tile // GMLP_CHUNK):
            rs = slice(c * GMLP_CHUNK, (c + 1) * GMLP_CHUNK)
            cs = slice(g * gd, (g + 1) * gd)
            mixed = jnp.dot(wg, vn[rs, cs], preferred_element_type=F32) + bg
            u = _gelu_tanh(u_ref[rs, cs].astype(F32))
            o_ref[rs, cs] = (u * mixed).astype(o_ref.dtype)


def _gmlp_mixer(proj, ln_g, spatial_w, spatial_b_t, tile):
    t = proj.shape[0]
    return pl.pallas_call(
        _gmlp_kernel,
        out_shape=jax.ShapeDtypeStruct((t, GMLP_WIDTH), BF16),
        grid=(t // tile,),
        in_specs=[
            pl.BlockSpec((tile, GMLP_WIDTH), lambda i: (i, OFF_U // GMLP_WIDTH)),
            pl.BlockSpec((tile, GMLP_WIDTH), lambda i: (i, OFF_V // GMLP_WIDTH)),
            pl.BlockSpec((1, GMLP_WIDTH), lambda i: (0, 0)),
            pl.BlockSpec((GMLP_GROUPS, GMLP_CHUNK, GMLP_CHUNK), lambda i: (0, 0, 0)),
            pl.BlockSpec((GMLP_CHUNK, GMLP_GROUPS), lambda i: (0, 0)),
        ],
        out_specs=pl.BlockSpec((tile, GMLP_WIDTH), lambda i: (i, 0)),
        compiler_params=_cparams(("parallel",)),
        name="gmlp_mixer",
    )(proj, proj, ln_g, spatial_w, spatial_b_t)


def _diff_attn_kernel(lam_ref, q_ref, k_ref, v_ref, g_ref, o_ref, vt_ref, qt_ref, sa_ref, sb_ref, ca_ref, cb_ref,
                      m_ref, l_ref, acc_ref, *, blk, lambda_init):
    i = pl.program_id(1)
    n_blk = vt_ref.shape[0]

    @pl.when(i == 0)
    def _():
        def transpose_v(c, carry):
            r0 = pl.multiple_of(c * blk, blk)
            vt_ref[c] = v_ref[pl.ds(r0, blk), :].astype(F32).T.astype(BF16)
            return carry
        lax.fori_loop(0, n_blk, transpose_v, 0)

    qt = q_ref[...].astype(F32).T
    sub = lax.broadcasted_iota(jnp.int32, (LANES, blk), 0)
    qt_ref[0] = jnp.where(sub < DIFF_HEAD_DIM, qt, 0.0).astype(BF16)
    qt_ref[1] = jnp.where(sub >= DIFF_HEAD_DIM, qt, 0.0).astype(BF16)
    m_ref[...] = jnp.full(m_ref.shape, -jnp.inf, F32)
    l_ref[...] = jnp.zeros(l_ref.shape, F32)
    acc_ref[...] = jnp.zeros(acc_ref.shape, F32)

    def scores(j, s_ref, c_ref):
        kb = k_ref[pl.ds(pl.multiple_of(j * blk, blk), blk), :]
        for h in range(2):
            s = jnp.dot(kb, qt_ref[h], preferred_element_type=F32)
            s_ref[h] = s
            c_ref[h] = jnp.max(s, axis=0, keepdims=True)

    def accumulate(j, s_ref, c_ref, masked):
        vtb = vt_ref[j]
        for h in range(2):
            s = s_ref[h]
            if masked:
                row = lax.broadcasted_iota(jnp.int32, (blk, blk), 0)
                col = lax.broadcasted_iota(jnp.int32, (blk, blk), 1)
                s = jnp.where(row <= col, s, -jnp.inf)
                c = jnp.max(s, axis=0, keepdims=True)
            else:
                c = c_ref[h]
            m_prev = m_ref[h]
            m_new = jnp.maximum(m_prev, c)
            alpha = jnp.exp2(m_prev - m_new)
            p = jnp.exp2(s - m_new)
            l_ref[h] = alpha * l_ref[h] + jnp.sum(p, axis=0, keepdims=True)
            acc_ref[h] = alpha * acc_ref[h] + jnp.dot(vtb, p.astype(BF16), preferred_element_type=F32)
            m_ref[h] = m_new

    scores(0, sa_ref, ca_ref)

    def two_blocks(j):
        scores(j + 1, sb_ref, cb_ref)
        accumulate(j, sa_ref, ca_ref, False)
        scores(j + 2, sa_ref, ca_ref)
        accumulate(j + 1, sb_ref, cb_ref, False)

    def oct_(p, carry):
        for u in range(4):
            two_blocks(8 * p + 2 * u)
        return carry

    def quad(p, carry):
        two_blocks(8 * (i // 8) + 4 * p)
        two_blocks(8 * (i // 8) + 4 * p + 2)
        return carry

    def pair(p, carry):
        two_blocks(4 * (i // 4) + 2 * p)
        return carry

    lax.fori_loop(0, i // 8, oct_, 0)
    lax.fori_loop(0, (i % 8) // 4, quad, 0)
    lax.fori_loop(0, (i % 4) // 2, pair, 0)

    @pl.when(i % 2 == 1)
    def _():
        scores(i, sb_ref, cb_ref)
        accumulate(i - 1, sa_ref, ca_ref, False)
        accumulate(i, sb_ref, cb_ref, True)

    @pl.when(i % 2 == 0)
    def _():
        accumulate(i, sa_ref, ca_ref, True)

    lf = lam_ref[...]
    lam = (jnp.exp(jnp.sum(lf[0:1] * lf[1:2], axis=1, keepdims=True))
           - jnp.exp(jnp.sum(lf[2:3] * lf[3:4], axis=1, keepdims=True)) + lambda_init)
    o = (acc_ref[0] / l_ref[0] - lam * (acc_ref[1] / l_ref[1])).T
    ms = jnp.mean(o * o, axis=-1, keepdims=True)
    o = o * lax.rsqrt(ms + EPS) * g_ref[...] * (1.0 - lambda_init)
    o_ref[...] = o.astype(o_ref.dtype)


def _diff_attention(proj, lambdas, norm_g, lambda_init, blk):
    t = proj.shape[0]
    qo, ko, vo = OFF_DQ // LANES, OFF_DK // LANES, OFF_DV // LANES
    return pl.pallas_call(
        functools.partial(_diff_attn_kernel, blk=blk, lambda_init=lambda_init),
        out_shape=jax.ShapeDtypeStruct((t, DIFF_WIDTH), BF16),
        grid=(DIFF_HEADS, t // blk),
        in_specs=[
            pl.BlockSpec((4, DIFF_HEAD_DIM), lambda h, i: (0, 0)),
            pl.BlockSpec((blk, LANES), lambda h, i: (i, qo + h)),
            pl.BlockSpec((t, LANES), lambda h, i: (0, ko + h)),
            pl.BlockSpec((t, LANES), lambda h, i: (0, vo + h)),
            pl.BlockSpec((1, LANES), lambda h, i: (0, 0)),
        ],
        out_specs=pl.BlockSpec((blk, LANES), lambda h, i: (i, h)),
        scratch_shapes=[
            pltpu.VMEM((t // blk, LANES, blk), BF16),
            pltpu.VMEM((2, LANES, blk), BF16),
            pltpu.VMEM((2, blk, blk), F32),
            pltpu.VMEM((2, blk, blk), F32),
            pltpu.VMEM((2, 1, blk), F32),
            pltpu.VMEM((2, 1, blk), F32),
            pltpu.VMEM((2, 1, blk), F32),
            pltpu.VMEM((2, 1, blk), F32),
            pltpu.VMEM((2, LANES, blk), F32),
        ],
        compiler_params=_cparams(("arbitrary", "arbitrary")),
        name="diff_attention",
    )(lambdas, proj, proj, proj, norm_g)


def _gla_kernel(gv_ref, gr_ref, gq_ref, gk_ref, code_ref, w2_ref, gb_ref, ng_ref, o_ref, st_ref, g_ref):
    tile = gq_ref.shape[0]
    c_len = GLA_CHUNK

    @pl.when(pl.program_id(0) == 0)
    def _():
        st_ref[...] = jnp.zeros(st_ref.shape, F32)

    z = jnp.dot(code_ref[...], w2_ref[...], preferred_element_type=F32) + gb_ref[...]
    g_ref[...] = (jnp.minimum(z, 0.0) - jnp.log1p(jnp.exp(-jnp.abs(z)))) * (1.0 / GLA_TAU)

    row = lax.broadcasted_iota(jnp.int32, (c_len, c_len), 0)
    col = lax.broadcasted_iota(jnp.int32, (c_len, c_len), 1)
    causal = col <= row
    tri = jnp.where(causal, 1.0, 0.0).astype(BF16)
    ng = ng_ref[...]

    def chunk(c, carry):
        r0 = pl.multiple_of(c * c_len, c_len)
        g = g_ref[pl.ds(r0, c_len), :]
        g_hi = g.astype(BF16)
        g_lo = (g - g_hi.astype(F32)).astype(BF16)
        b = (jnp.dot(tri, g_hi, preferred_element_type=F32)
             + jnp.dot(tri, g_lo, preferred_element_type=F32))
        b_last = b[c_len - 1:c_len, :]
        q = gq_ref[pl.ds(r0, c_len), :].astype(F32) * (GLA_DK ** -0.5)
        k = gk_ref[pl.ds(r0, c_len), :].astype(F32)
        qd = (q * jnp.exp(b)).astype(BF16)
        kd = (k * jnp.exp(-b)).astype(BF16)
        kl = (k * jnp.exp(b_last - b)).astype(BF16)
        e_last = jnp.exp(b_last)
        for h in range(GLA_HEADS):
            ks = slice(h * GLA_DK_PAD, (h + 1) * GLA_DK_PAD)
            vs = slice(h * GLA_DV_PAD, (h + 1) * GLA_DV_PAD)
            vh = gv_ref[pl.ds(r0, c_len), vs]
            st = st_ref[h]
            a = lax.dot_general(qd[:, ks], kd[:, ks], (((1,), (1,)), ((), ())), preferred_element_type=F32)
            a = jnp.where(causal, a, 0.0).astype(BF16)
            o = jnp.dot(a, vh, preferred_element_type=F32)
            o = o + lax.dot_general(qd[:, ks], st.astype(BF16), (((1,), (1,)), ((), ())),
                                    preferred_element_type=F32)
            st_ref[h] = st * e_last[:, ks] + lax.dot_general(
                vh, kl[:, ks], (((0,), (0,)), ((), ())), preferred_element_type=F32)
            ms = jnp.sum(o * o, axis=-1, keepdims=True) * (1.0 / GLA_DV)
            on = o * lax.rsqrt(ms + EPS) * ng
            r = gr_ref[pl.ds(r0, c_len), vs].astype(F32)
            o_ref[pl.ds(r0, c_len), vs] = (on * (r / (1.0 + jnp.exp(-r)))).astype(o_ref.dtype)
        return carry

    lax.fori_loop(0, tile // c_len, chunk, 0, unroll=True)


def _gla_mixer(proj, w2p, gbp, ngp, tile):
    t = proj.shape[0]
    kw = GLA_HEADS * GLA_DK_PAD
    return pl.pallas_call(
        _gla_kernel,
        out_shape=jax.ShapeDtypeStruct((t, MIX_C_WIDTH), BF16),
        grid=(t // tile,),
        in_specs=[
            pl.BlockSpec((tile, MIX_C_WIDTH), lambda i: (i, OFF_GV // MIX_C_WIDTH)),
            pl.BlockSpec((tile, MIX_C_WIDTH), lambda i: (i, OFF_GR // MIX_C_WIDTH)),
            pl.BlockSpec((tile, kw), lambda i: (i, OFF_GQ // kw)),
            pl.BlockSpec((tile, kw), lambda i: (i, OFF_GK // kw)),
            pl.BlockSpec((tile, LANES), lambda i: (i, OFF_CODE // LANES)),
            pl.BlockSpec((LANES, kw), lambda i: (0, 0)),
            pl.BlockSpec((1, kw), lambda i: (0, 0)),
            pl.BlockSpec((1, GLA_DV_PAD), lambda i: (0, 0)),
        ],
        out_specs=pl.BlockSpec((tile, MIX_C_WIDTH), lambda i: (i, 0)),
        scratch_shapes=[
            pltpu.VMEM((GLA_HEADS, GLA_DV_PAD, GLA_DK_PAD), F32),
            pltpu.VMEM((tile, kw), F32),
        ],
        compiler_params=_cparams(("arbitrary",)),
        name="gla_mixer",
    )(proj, proj, proj, proj, proj, w2p, gbp, ngp)


def _final_norm_kernel(x_ref, g_ref, o_ref):
    x = x_ref[...]
    ms = jnp.mean(x * x, axis=-1, keepdims=True)
    o_ref[...] = x * lax.rsqrt(ms + EPS) * g_ref[...]


def _final_norm(x, g, tm):
    t, d = x.shape
    return pl.pallas_call(
        _final_norm_kernel,
        out_shape=jax.ShapeDtypeStruct((t, d), F32),
        grid=(t // tm,),
        in_specs=[pl.BlockSpec((tm, d), lambda i: (i, 0)), pl.BlockSpec((1, d), lambda i: (0, 0))],
        out_specs=pl.BlockSpec((tm, d), lambda i: (i, 0)),
        compiler_params=_cparams(("parallel",)),
        name="final_norm",
    )(x, g)


def _pad_heads(w, heads, d, d_pad):
    lead = w.shape[:-1]
    w = w.reshape(lead + (heads, d))
    w = jnp.pad(w, [(0, 0)] * len(lead) + [(0, 0), (0, d_pad - d)])
    return w.reshape(lead + (heads * d_pad,))


def _head_pieces(w, axis, off, heads, d, d_pad):
    pieces = []
    for h in range(heads):
        sl = lax.slice_in_dim(w, off + h * d, off + (h + 1) * d, axis=axis)
        pieces.append(sl)
        zshape = list(w.shape)
        zshape[axis] = d_pad - d
        pieces.append(jnp.zeros(zshape, w.dtype))
    return pieces


def _layout_w_in(w_in):
    splits = [GMLP_WIDTH, GMLP_WIDTH, DIFF_WIDTH, DIFF_WIDTH, DIFF_WIDTH,
              GLA_KEY_WIDTH, GLA_KEY_WIDTH, GLA_WIDTH, GLA_WIDTH, GLA_RANK]
    idx = [0]
    for s in splits:
        idx.append(idx[-1] + s)
    o_u, o_v, o_dq, o_dk, o_dv, o_gq, o_gk, o_gv, o_gr, o_code, end = idx
    ax = w_in.ndim - 1
    pieces = (_head_pieces(w_in, ax, o_gv, GLA_HEADS, GLA_DV, GLA_DV_PAD)
              + _head_pieces(w_in, ax, o_gr, GLA_HEADS, GLA_DV, GLA_DV_PAD)
              + _head_pieces(w_in, ax, o_gq, GLA_HEADS, GLA_DK, GLA_DK_PAD)
              + _head_pieces(w_in, ax, o_gk, GLA_HEADS, GLA_DK, GLA_DK_PAD)
              + [lax.slice_in_dim(w_in, o_u, o_gq, axis=ax),
                 lax.slice_in_dim(w_in, o_code, end, axis=ax),
                 jnp.zeros(w_in.shape[:-1] + (PROJ_WIDTH - OFF_CODE - GLA_RANK,), w_in.dtype)])
    return jnp.concatenate(pieces, axis=ax).astype(BF16)


def _layout_w_out(w_out):
    wa = w_out[:, :GMLP_WIDTH, :].astype(BF16)
    wb = w_out[:, GMLP_WIDTH:GMLP_WIDTH + DIFF_WIDTH, :].astype(BF16)
    pieces = _head_pieces(w_out, 1, GMLP_WIDTH + DIFF_WIDTH, GLA_HEADS, GLA_DV, GLA_DV_PAD)
    return wa, wb, jnp.concatenate(pieces, axis=1).astype(BF16)


def kernel(x, norm1_g, w_in, gmlp_ln_g, spatial_w, spatial_b, diff_lambdas, diff_norm_g, gla_gate_w2,
           gla_gate_b, gla_norm_g, w_out, norm2_g, w_ffn_in, w_ffn_out, final_g):
    b, t, d = x.shape
    assert b == 1 and d == D_MODEL
    xs = x.reshape(t, d)

    w1 = _layout_w_in(w_in)
    wa, wb, wc = _layout_w_out(w_out)
    wf_in = w_ffn_in.astype(BF16)
    wf_out = w_ffn_out.astype(BF16)
    w2p = jnp.pad(_pad_heads(gla_gate_w2, GLA_HEADS, GLA_DK, GLA_DK_PAD),
                  [(0, 0), (0, LANES - GLA_RANK), (0, 0)]).astype(BF16)
    gbp = _pad_heads(gla_gate_b, GLA_HEADS, GLA_DK, GLA_DK_PAD)[:, None, :]
    ngp = jnp.pad(gla_norm_g, [(0, 0), (0, GLA_DV_PAD - GLA_DV)])[:, None, :]
    sb_t = jnp.swapaxes(spatial_b, 1, 2)

    col = jnp.arange(PROJ_WIDTH)
    q_scale = (DIFF_HEAD_DIM ** -0.5) * math.log2(math.e)
    col_scale = jnp.where((col >= OFF_DQ) & (col < OFF_DK), q_scale, 1.0).astype(F32)[None, :]

    tm = min(1024, t)
    for l in range(DEPTH):
        lambda_init = 0.8 - 0.6 * math.exp(-0.3 * l)
        proj = _norm_matmul(xs, norm1_g[l][None, :], w1[l], col_scale, min(512, t), PROJ_WIDTH // 2)
        a_out = _gmlp_mixer(proj, gmlp_ln_g[l][None, :], spatial_w[l], sb_t[l], min(512, t))
        b_out = _diff_attention(proj, diff_lambdas[l], diff_norm_g[l][None, :], lambda_init, min(512, t))
        c_out = _gla_mixer(proj, w2p[l], gbp[l], ngp[l], min(256, t))
        xs = _matmul_residual([a_out, b_out, c_out], [wa[l], wb[l], wc[l]], xs, min(512, t), d, "out_proj")
        hid = _norm_swiglu(xs, norm2_g[l][None, :], wf_in[l], tm, 512)
        xs = _matmul_residual([hid], [wf_out[l]], xs, tm, 512, "ffn_out")
    out = _final_norm(xs, final_g[None, :], min(512, t))
    return out.reshape(b, t, d)
```

```python
import functools
import math

import jax
import jax.numpy as jnp
from jax import lax
from jax.experimental import pallas as pl
from jax.experimental.pallas import tpu as pltpu

F32 = jnp.float32
BF16 = jnp.bfloat16

D_MODEL = 2048
DEPTH = 4
EPS = 1e-6
LANES = 128

GMLP_WIDTH = 512
GMLP_GROUPS = 4
GMLP_CHUNK = 128
DIFF_WIDTH = 768
DIFF_HEADS = 6
DIFF_HEAD_DIM = 64
GLA_HEADS = 4
GLA_DK = 96
GLA_DV = 192
GLA_DK_PAD = 128
GLA_DV_PAD = 256
GLA_KEY_WIDTH = GLA_HEADS * GLA_DK
GLA_WIDTH = GLA_HEADS * GLA_DV
GLA_RANK = 16
GLA_TAU = 16.0
GLA_CHUNK = 64
FFN_HIDDEN = 5632

OFF_GV = 0
OFF_GR = OFF_GV + GLA_HEADS * GLA_DV_PAD
OFF_GQ = OFF_GR + GLA_HEADS * GLA_DV_PAD
OFF_GK = OFF_GQ + GLA_HEADS * GLA_DK_PAD
OFF_U = OFF_GK + GLA_HEADS * GLA_DK_PAD
OFF_V = OFF_U + GMLP_WIDTH
OFF_DQ = OFF_V + GMLP_WIDTH
OFF_DK = OFF_DQ + DIFF_WIDTH
OFF_DV = OFF_DK + DIFF_WIDTH
OFF_CODE = OFF_DV + DIFF_WIDTH
PROJ_TN = 512
PROJ_WIDTH = -(-(OFF_CODE + LANES) // PROJ_TN) * PROJ_TN
MIX_C_WIDTH = GLA_HEADS * GLA_DV_PAD

VMEM_LIMIT = 56 * 1024 * 1024


def _cparams(sem):
    return pltpu.CompilerParams(dimension_semantics=sem, vmem_limit_bytes=VMEM_LIMIT)


def _rmsnorm_rows(x_ref, g_ref, h_ref, rows):
    tm = x_ref.shape[0]
    g = g_ref[...]
    for r in range(0, tm, rows):
        x = x_ref[r:r + rows, :]
        ms = jnp.mean(x * x, axis=-1, keepdims=True)
        h_ref[r:r + rows, :] = (x * lax.rsqrt(ms + EPS) * g).astype(BF16)


def _norm_mm_kernel(x_ref, g_ref, w_ref, cs_ref, o_ref, h_ref):
    @pl.when(pl.program_id(1) == 0)
    def _():
        _rmsnorm_rows(x_ref, g_ref, h_ref, 256)

    acc = jnp.dot(h_ref[...], w_ref[...], preferred_element_type=F32)
    o_ref[...] = (acc * cs_ref[...]).astype(o_ref.dtype)


def _swiglu_kernel(h_ref, wg_ref, wu_ref, o_ref):
    h = h_ref[...]
    gate = jnp.dot(h, wg_ref[...], preferred_element_type=F32)
    up = jnp.dot(h, wu_ref[...], preferred_element_type=F32)
    o_ref[...] = (gate / (1.0 + jnp.exp(-gate)) * up).astype(o_ref.dtype)


def _norm_matmul(x, g, w, col_scale, tm, tn):
    t, d = x.shape
    n = w.shape[1]
    return pl.pallas_call(
        _norm_mm_kernel,
        out_shape=jax.ShapeDtypeStruct((t, n), BF16),
        grid=(t // tm, n // tn),
        in_specs=[
            pl.BlockSpec((tm, d), lambda i, j: (i, 0)),
            pl.BlockSpec((1, d), lambda i, j: (0, 0)),
            pl.BlockSpec((d, tn), lambda i, j: (0, j)),
            pl.BlockSpec((1, tn), lambda i, j: (0, j)),
        ],
        out_specs=pl.BlockSpec((tm, tn), lambda i, j: (i, j)),
        scratch_shapes=[pltpu.VMEM((tm, d), BF16)],
        compiler_params=_cparams(("parallel", "arbitrary")),
        name="norm_in_proj",
    )(x, g, w, col_scale)


def _swiglu(h, w, tm, tn):
    t, d = h.shape
    hidden = w.shape[1] // 2
    nj = hidden // tn
    return pl.pallas_call(
        _swiglu_kernel,
        out_shape=jax.ShapeDtypeStruct((t, hidden), BF16),
        grid=(t // tm, nj),
        in_specs=[
            pl.BlockSpec((tm, d), lambda i, j: (i, 0)),
            pl.BlockSpec((d, tn), lambda i, j: (0, j)),
            pl.BlockSpec((d, tn), lambda i, j: (0, j + nj)),
        ],
        out_specs=pl.BlockSpec((tm, tn), lambda i, j: (i, j)),
        compiler_params=_cparams(("parallel", "arbitrary")),
        name="ffn_in",
    )(h, w, w)


def _mm_res_kernel(*refs, n_a):
    a_refs, w_refs = refs[:n_a], refs[n_a:2 * n_a]
    r_ref, o_ref = refs[2 * n_a], refs[2 * n_a + 1]
    acc = r_ref[...]
    for a_ref, w_ref in zip(a_refs, w_refs):
        acc = acc + jnp.dot(a_ref[...], w_ref[...], preferred_element_type=F32)
    o_ref[...] = acc


def _mm_res_norm_kernel(*refs, n_a):
    a_refs, w_refs = refs[:n_a], refs[n_a:2 * n_a]
    r_ref, g_ref, o_ref, h_ref = refs[2 * n_a:2 * n_a + 4]
    acc = r_ref[...]
    for a_ref, w_ref in zip(a_refs, w_refs):
        acc = acc + jnp.dot(a_ref[...], w_ref[...], preferred_element_type=F32)
    o_ref[...] = acc
    ms = jnp.mean(acc * acc, axis=-1, keepdims=True)
    h_ref[...] = (acc * lax.rsqrt(ms + EPS) * g_ref[...]).astype(h_ref.dtype)


def _matmul_residual_norm(a_list, w_list, res, g, tm, name):
    t, n = res.shape
    n_a = len(a_list)
    in_specs = [pl.BlockSpec((tm, a.shape[1]), lambda i: (i, 0)) for a in a_list]
    in_specs += [pl.BlockSpec((w.shape[0], n), lambda i: (0, 0)) for w in w_list]
    in_specs += [pl.BlockSpec((tm, n), lambda i: (i, 0)), pl.BlockSpec((1, n), lambda i: (0, 0))]
    return pl.pallas_call(
        functools.partial(_mm_res_norm_kernel, n_a=n_a),
        out_shape=(jax.ShapeDtypeStruct((t, n), F32), jax.ShapeDtypeStruct((t, n), BF16)),
        grid=(t // tm,),
        in_specs=in_specs,
        out_specs=(pl.BlockSpec((tm, n), lambda i: (i, 0)), pl.BlockSpec((tm, n), lambda i: (i, 0))),
        compiler_params=_cparams(("parallel",)),
        name=name,
    )(*a_list, *w_list, res, g)


def _matmul_residual(a_list, w_list, res, tm, tn, name):
    t, n = res.shape
    n_a = len(a_list)
    in_specs = [pl.BlockSpec((tm, a.shape[1]), lambda i, j: (i, 0)) for a in a_list]
    in_specs += [pl.BlockSpec((w.shape[0], tn), lambda i, j: (0, j)) for w in w_list]
    in_specs += [pl.BlockSpec((tm, tn), lambda i, j: (i, j))]
    return pl.pallas_call(
        functools.partial(_mm_res_kernel, n_a=n_a),
        out_shape=jax.ShapeDtypeStruct((t, n), F32),
        grid=(t // tm, n // tn),
        in_specs=in_specs,
        out_specs=pl.BlockSpec((tm, tn), lambda i, j: (i, j)),
        compiler_params=_cparams(("parallel", "arbitrary")),
        name=name,
    )(*a_list, *w_list, res)


def _gelu_tanh(x):
    return 0.5 * x * (1.0 + jnp.tanh(math.sqrt(2.0 / math.pi) * (x + 0.044715 * (x * x * x))))


def _gmlp_kernel(u_ref, v_ref, lng_ref, w_ref, b_ref, o_ref):
    tile = u_ref.shape[0]
    gd = GMLP_WIDTH // GMLP_GROUPS
    v = _gelu_tanh(v_ref[...].astype(F32))
    mu = jnp.mean(v, axis=-1, keepdims=True)
    vc = v - mu
    var = jnp.mean(vc * vc, axis=-1, keepdims=True)
    vn = (vc * lax.rsqrt(var + EPS) * lng_ref[...]).astype(BF16)
    row = lax.broadcasted_iota(jnp.int32, (GMLP_CHUNK, GMLP_CHUNK), 0)
    col = lax.broadcasted_iota(jnp.int32, (GMLP_CHUNK, GMLP_CHUNK), 1)
    causal = col <= row
    bias = b_ref[...]
    for g in range(GMLP_GROUPS):
        wg = jnp.where(causal, w_ref[g], 0.0).astype(BF16)
        bg = bias[:, g:g + 1]
        for c in range(tile // GMLP_CHUNK):
            rs = slice(c * GMLP_CHUNK, (c + 1) * GMLP_CHUNK)
            cs = slice(g * gd, (g + 1) * gd)
            mixed = jnp.dot(wg, vn[rs, cs], preferred_element_type=F32) + bg
            u = _gelu_tanh(u_ref[rs, cs].astype(F32))
            o_ref[rs, cs] = (u * mixed).astype(o_ref.dtype)


def _gmlp_mixer(proj, ln_g, spatial_w, spatial_b_t, tile):
    t = proj.shape[0]
    return pl.pallas_call(
        _gmlp_kernel,
        out_shape=jax.ShapeDtypeStruct((t, GMLP_WIDTH), BF16),
        grid=(t // tile,),
        in_specs=[
            pl.BlockSpec((tile, GMLP_WIDTH), lambda i: (i, OFF_U // GMLP_WIDTH)),
            pl.BlockSpec((tile, GMLP_WIDTH), lambda i: (i, OFF_V // GMLP_WIDTH)),
            pl.BlockSpec((1, GMLP_WIDTH), lambda i: (0, 0)),
            pl.BlockSpec((GMLP_GROUPS, GMLP_CHUNK, GMLP_CHUNK), lambda i: (0, 0, 0)),
            pl.BlockSpec((GMLP_CHUNK, GMLP_GROUPS), lambda i: (0, 0)),
        ],
        out_specs=pl.BlockSpec((tile, GMLP_WIDTH), lambda i: (i, 0)),
        compiler_params=_cparams(("parallel",)),
        name="gmlp_mixer",
    )(proj, proj, ln_g, spatial_w, spatial_b_t)


def _diff_attn_kernel(lam_ref, q_ref, k_ref, v_ref, g_ref, o_ref, vt_ref, qt_ref, sa_ref, sb_ref, ca_ref, cb_ref,
                      m_ref, l_ref, acc_ref, *, blk, lambda_init):
    i = pl.program_id(1)
    n_blk = vt_ref.shape[0]

    @pl.when(i == 0)
    def _():
        def transpose_v(c, carry):
            r0 = pl.multiple_of(c * blk, blk)
            vt_ref[c] = v_ref[pl.ds(r0, blk), :].astype(F32).T.astype(BF16)
            return carry
        lax.fori_loop(0, n_blk, transpose_v, 0)

    qt = q_ref[...].astype(F32).T
    sub = lax.broadcasted_iota(jnp.int32, (LANES, blk), 0)
    qt_ref[0] = jnp.where(sub < DIFF_HEAD_DIM, qt, 0.0).astype(BF16)
    qt_ref[1] = jnp.where(sub >= DIFF_HEAD_DIM, qt, 0.0).astype(BF16)
    m_ref[...] = jnp.full(m_ref.shape, -jnp.inf, F32)
    l_ref[...] = jnp.zeros(l_ref.shape, F32)
    acc_ref[...] = jnp.zeros(acc_ref.shape, F32)

    def scores(j, s_ref, c_ref):
        kb = k_ref[pl.ds(pl.multiple_of(j * blk, blk), blk), :]
        for h in range(2):
            s = jnp.dot(kb, qt_ref[h], preferred_element_type=F32)
            s_ref[h] = s
            c_ref[h] = jnp.max(s, axis=0, keepdims=True)

    def accumulate(j, s_ref, c_ref, masked):
        vtb = vt_ref[j]
        for h in range(2):
            s = s_ref[h]
            if masked:
                row = lax.broadcasted_iota(jnp.int32, (blk, blk), 0)
                col = lax.broadcasted_iota(jnp.int32, (blk, blk), 1)
                s = jnp.where(row <= col, s, -jnp.inf)
                c = jnp.max(s, axis=0, keepdims=True)
            else:
                c = c_ref[h]
            m_prev = m_ref[h]
            m_new = jnp.maximum(m_prev, c)
            alpha = jnp.exp2(m_prev - m_new)
            p = jnp.exp2(s - m_new)
            l_ref[h] = alpha * l_ref[h] + jnp.sum(p, axis=0, keepdims=True)
            acc_ref[h] = alpha * acc_ref[h] + jnp.dot(vtb, p.astype(BF16), preferred_element_type=F32)
            m_ref[h] = m_new

    scores(0, sa_ref, ca_ref)

    def two_blocks(j):
        scores(j + 1, sb_ref, cb_ref)
        accumulate(j, sa_ref, ca_ref, False)
        scores(j + 2, sa_ref, ca_ref)
        accumulate(j + 1, sb_ref, cb_ref, False)

    def oct_(p, carry):
        for u in range(4):
            two_blocks(8 * p + 2 * u)
        return carry

    def quad(p, carry):
        two_blocks(8 * (i // 8) + 4 * p)
        two_blocks(8 * (i // 8) + 4 * p + 2)
        return carry

    def pair(p, carry):
        two_blocks(4 * (i // 4) + 2 * p)
        return carry

    lax.fori_loop(0, i // 8, oct_, 0)
    lax.fori_loop(0, (i % 8) // 4, quad, 0)
    lax.fori_loop(0, (i % 4) // 2, pair, 0)

    @pl.when(i % 2 == 1)
    def _():
        scores(i, sb_ref, cb_ref)
        accumulate(i - 1, sa_ref, ca_ref, False)
        accumulate(i, sb_ref, cb_ref, True)

    @pl.when(i % 2 == 0)
    def _():
        accumulate(i, sa_ref, ca_ref, True)

    lf = lam_ref[...]
    lam = (jnp.exp(jnp.sum(lf[0:1] * lf[1:2], axis=1, keepdims=True))
           - jnp.exp(jnp.sum(lf[2:3] * lf[3:4], axis=1, keepdims=True)) + lambda_init)
    o = (acc_ref[0] / l_ref[0] - lam * (acc_ref[1] / l_ref[1])).T
    ms = jnp.mean(o * o, axis=-1, keepdims=True)
    o = o * lax.rsqrt(ms + EPS) * g_ref[...] * (1.0 - lambda_init)
    o_ref[...] = o.astype(o_ref.dtype)


def _diff_attention(proj, lambdas, norm_g, lambda_init, blk):
    t = proj.shape[0]
    qo, ko, vo = OFF_DQ // LANES, OFF_DK // LANES, OFF_DV // LANES
    return pl.pallas_call(
        functools.partial(_diff_attn_kernel, blk=blk, lambda_init=lambda_init),
        out_shape=jax.ShapeDtypeStruct((t, DIFF_WIDTH), BF16),
        grid=(DIFF_HEADS, t // blk),
        in_specs=[
            pl.BlockSpec((4, DIFF_HEAD_DIM), lambda h, i: (0, 0)),
            pl.BlockSpec((blk, LANES), lambda h, i: (i, qo + h)),
            pl.BlockSpec((t, LANES), lambda h, i: (0, ko + h)),
            pl.BlockSpec((t, LANES), lambda h, i: (0, vo + h)),
            pl.BlockSpec((1, LANES), lambda h, i: (0, 0)),
        ],
        out_specs=pl.BlockSpec((blk, LANES), lambda h, i: (i, h)),
        scratch_shapes=[
            pltpu.VMEM((t // blk, LANES, blk), BF16),
            pltpu.VMEM((2, LANES, blk), BF16),
            pltpu.VMEM((2, blk, blk), F32),
            pltpu.VMEM((2, blk, blk), F32),
            pltpu.VMEM((2, 1, blk), F32),
            pltpu.VMEM((2, 1, blk), F32),
            pltpu.VMEM((2, 1, blk), F32),
            pltpu.VMEM((2, 1, blk), F32),
            pltpu.VMEM((2, LANES, blk), F32),
        ],
        compiler_params=_cparams(("arbitrary", "arbitrary")),
        name="diff_attention",
    )(lambdas, proj, proj, proj, norm_g)


def _gla_kernel(gv_ref, gr_ref, gq_ref, gk_ref, code_ref, w2_ref, gb_ref, ng_ref, o_ref, st_ref, g_ref):
    tile = gq_ref.shape[0]
    c_len = GLA_CHUNK

    @pl.when(pl.program_id(0) == 0)
    def _():
        st_ref[...] = jnp.zeros(st_ref.shape, F32)

    z = jnp.dot(code_ref[...], w2_ref[...], preferred_element_type=F32) + gb_ref[...]
    g_ref[...] = (jnp.minimum(z, 0.0) - jnp.log1p(jnp.exp(-jnp.abs(z)))) * (1.0 / GLA_TAU)

    row = lax.broadcasted_iota(jnp.int32, (c_len, c_len), 0)
    col = lax.broadcasted_iota(jnp.int32, (c_len, c_len), 1)
    causal = col <= row
    tri = jnp.where(causal, 1.0, 0.0).astype(BF16)
    ng = ng_ref[...]

    def chunk(c, carry):
        r0 = pl.multiple_of(c * c_len, c_len)
        g = g_ref[pl.ds(r0, c_len), :]
        g_hi = g.astype(BF16)
        g_lo = (g - g_hi.astype(F32)).astype(BF16)
        b = (jnp.dot(tri, g_hi, preferred_element_type=F32)
             + jnp.dot(tri, g_lo, preferred_element_type=F32))
        b_last = b[c_len - 1:c_len, :]
        q = gq_ref[pl.ds(r0, c_len), :].astype(F32) * (GLA_DK ** -0.5)
        k = gk_ref[pl.ds(r0, c_len), :].astype(F32)
        qd = (q * jnp.exp(b)).astype(BF16)
        kd = (k * jnp.exp(-b)).astype(BF16)
        kl = (k * jnp.exp(b_last - b)).astype(BF16)
        e_last = jnp.exp(b_last)
        for h in range(GLA_HEADS):
            ks = slice(h * GLA_DK_PAD, (h + 1) * GLA_DK_PAD)
            vs = slice(h * GLA_DV_PAD, (h + 1) * GLA_DV_PAD)
            vh = gv_ref[pl.ds(r0, c_len), vs]
            st = st_ref[h]
            a = lax.dot_general(qd[:, ks], kd[:, ks], (((1,), (1,)), ((), ())), preferred_element_type=F32)
            a = jnp.where(causal, a, 0.0).astype(BF16)
            o = jnp.dot(a, vh, preferred_element_type=F32)
            o = o + lax.dot_general(qd[:, ks], st.astype(BF16), (((1,), (1,)), ((), ())),
                                    preferred_element_type=F32)
            st_ref[h] = st * e_last[:, ks] + lax.dot_general(
                vh, kl[:, ks], (((0,), (0,)), ((), ())), preferred_element_type=F32)
            ms = jnp.sum(o * o, axis=-1, keepdims=True) * (1.0 / GLA_DV)
            on = o * lax.rsqrt(ms + EPS) * ng
            r = gr_ref[pl.ds(r0, c_len), vs].astype(F32)
            o_ref[pl.ds(r0, c_len), vs] = (on * (r / (1.0 + jnp.exp(-r)))).astype(o_ref.dtype)
        return carry

    lax.fori_loop(0, tile // c_len, chunk, 0, unroll=True)


def _gla_mixer(proj, w2p, gbp, ngp, tile):
    t = proj.shape[0]
    kw = GLA_HEADS * GLA_DK_PAD
    return pl.pallas_call(
        _gla_kernel,
        out_shape=jax.ShapeDtypeStruct((t, MIX_C_WIDTH), BF16),
        grid=(t // tile,),
        in_specs=[
            pl.BlockSpec((tile, MIX_C_WIDTH), lambda i: (i, OFF_GV // MIX_C_WIDTH)),
            pl.BlockSpec((tile, MIX_C_WIDTH), lambda i: (i, OFF_GR // MIX_C_WIDTH)),
            pl.BlockSpec((tile, kw), lambda i: (i, OFF_GQ // kw)),
            pl.BlockSpec((tile, kw), lambda i: (i, OFF_GK // kw)),
            pl.BlockSpec((tile, LANES), lambda i: (i, OFF_CODE // LANES)),
            pl.BlockSpec((LANES, kw), lambda i: (0, 0)),
            pl.BlockSpec((1, kw), lambda i: (0, 0)),
            pl.BlockSpec((1, GLA_DV_PAD), lambda i: (0, 0)),
        ],
        out_specs=pl.BlockSpec((tile, MIX_C_WIDTH), lambda i: (i, 0)),
        scratch_shapes=[
            pltpu.VMEM((GLA_HEADS, GLA_DV_PAD, GLA_DK_PAD), F32),
            pltpu.VMEM((tile, kw), F32),
        ],
        compiler_params=_cparams(("arbitrary",)),
        name="gla_mixer",
    )(proj, proj, proj, proj, proj, w2p, gbp, ngp)


def _final_norm_kernel(x_ref, g_ref, o_ref):
    x = x_ref[...]
    ms = jnp.mean(x * x, axis=-1, keepdims=True)
    o_ref[...] = x * lax.rsqrt(ms + EPS) * g_ref[...]


def _final_norm(x, g, tm):
    t, d = x.shape
    return pl.pallas_call(
        _final_norm_kernel,
        out_shape=jax.ShapeDtypeStruct((t, d), F32),
        grid=(t // tm,),
        in_specs=[pl.BlockSpec((tm, d), lambda i: (i, 0)), pl.BlockSpec((1, d), lambda i: (0, 0))],
        out_specs=pl.BlockSpec((tm, d), lambda i: (i, 0)),
        compiler_params=_cparams(("parallel",)),
        name="final_norm",
    )(x, g)


def _pad_heads(w, heads, d, d_pad):
    lead = w.shape[:-1]
    w = w.reshape(lead + (heads, d))
    w = jnp.pad(w, [(0, 0)] * len(lead) + [(0, 0), (0, d_pad - d)])
    return w.reshape(lead + (heads * d_pad,))


def _head_pieces(w, axis, off, heads, d, d_pad):
    pieces = []
    for h in range(heads):
        sl = lax.slice_in_dim(w, off + h * d, off + (h + 1) * d, axis=axis)
        pieces.append(sl)
        zshape = list(w.shape)
        zshape[axis] = d_pad - d
        pieces.append(jnp.zeros(zshape, w.dtype))
    return pieces


def _layout_w_in(w_in):
    splits = [GMLP_WIDTH, GMLP_WIDTH, DIFF_WIDTH, DIFF_WIDTH, DIFF_WIDTH,
              GLA_KEY_WIDTH, GLA_KEY_WIDTH, GLA_WIDTH, GLA_WIDTH, GLA_RANK]
    idx = [0]
    for s in splits:
        idx.append(idx[-1] + s)
    o_u, o_v, o_dq, o_dk, o_dv, o_gq, o_gk, o_gv, o_gr, o_code, end = idx
    ax = w_in.ndim - 1
    pieces = (_head_pieces(w_in, ax, o_gv, GLA_HEADS, GLA_DV, GLA_DV_PAD)
              + _head_pieces(w_in, ax, o_gr, GLA_HEADS, GLA_DV, GLA_DV_PAD)
              + _head_pieces(w_in, ax, o_gq, GLA_HEADS, GLA_DK, GLA_DK_PAD)
              + _head_pieces(w_in, ax, o_gk, GLA_HEADS, GLA_DK, GLA_DK_PAD)
              + [lax.slice_in_dim(w_in, o_u, o_gq, axis=ax),
                 lax.slice_in_dim(w_in, o_code, end, axis=ax),
                 jnp.zeros(w_in.shape[:-1] + (PROJ_WIDTH - OFF_CODE - GLA_RANK,), w_in.dtype)])
    return jnp.concatenate(pieces, axis=ax).astype(BF16)


def _layout_w_out(w_out):
    wa = w_out[:, :GMLP_WIDTH, :].astype(BF16)
    wb = w_out[:, GMLP_WIDTH:GMLP_WIDTH + DIFF_WIDTH, :].astype(BF16)
    pieces = _head_pieces(w_out, 1, GMLP_WIDTH + DIFF_WIDTH, GLA_HEADS, GLA_DV, GLA_DV_PAD)
    return wa, wb, jnp.concatenate(pieces, axis=1).astype(BF16)


def kernel(x, norm1_g, w_in, gmlp_ln_g, spatial_w, spatial_b, diff_lambdas, diff_norm_g, gla_gate_w2,
           gla_gate_b, gla_norm_g, w_out, norm2_g, w_ffn_in, w_ffn_out, final_g):
    b, t, d = x.shape
    assert b == 1 and d == D_MODEL
    xs = x.reshape(t, d)

    w2p = jnp.pad(_pad_heads(gla_gate_w2, GLA_HEADS, GLA_DK, GLA_DK_PAD),
                  [(0, 0), (0, LANES - GLA_RANK), (0, 0)]).astype(BF16)
    gbp = _pad_heads(gla_gate_b, GLA_HEADS, GLA_DK, GLA_DK_PAD)[:, None, :]
    ngp = jnp.pad(gla_norm_g, [(0, 0), (0, GLA_DV_PAD - GLA_DV)])[:, None, :]
    sb_t = jnp.swapaxes(spatial_b, 1, 2)

    col = jnp.arange(PROJ_WIDTH)
    q_scale = (DIFF_HEAD_DIM ** -0.5) * math.log2(math.e)
    col_scale = jnp.where((col >= OFF_DQ) & (col < OFF_DK), q_scale, 1.0).astype(F32)[None, :]

    for l in range(DEPTH):
        lambda_init = 0.8 - 0.6 * math.exp(-0.3 * l)
        w1 = _layout_w_in(w_in[l])
        wa, wb, wc = _layout_w_out(w_out[l:l + 1])
        proj = _norm_matmul(xs, norm1_g[l][None, :], w1, col_scale, min(512, t), PROJ_WIDTH // 2)
        a_out = _gmlp_mixer(proj, gmlp_ln_g[l][None, :], spatial_w[l], sb_t[l], min(512, t))
        b_out = _diff_attention(proj, diff_lambdas[l], diff_norm_g[l][None, :], lambda_init, min(512, t))
        c_out = _gla_mixer(proj, w2p[l], gbp[l], ngp[l], min(256, t))
        xs, h2 = _matmul_residual_norm([a_out, b_out, c_out], [wa[0], wb[0], wc[0]], xs, norm2_g[l][None, :],
                                       min(512, t), "out_proj")
        hid = _swiglu(h2, w_ffn_in[l].astype(BF16), min(2048, t), 512)
        xs = _matmul_residual([hid], [w_ffn_out[l].astype(BF16)], xs, min(1024, t), 512, "ffn_out")
    out = _final_norm(xs, final_g[None, :], min(512, t))
    return out.reshape(b, t, d)
```

```python
import functools
import math

import jax
import jax.numpy as jnp
from jax import lax
from jax.experimental import pallas as pl
from jax.experimental.pallas import tpu as pltpu

F32 = jnp.float32
BF16 = jnp.bfloat16

D_MODEL = 2048
DEPTH = 4
EPS = 1e-6
LANES = 128

GMLP_WIDTH = 512
GMLP_GROUPS = 4
GMLP_CHUNK = 128
DIFF_WIDTH = 768
DIFF_HEADS = 6
DIFF_HEAD_DIM = 64
GLA_HEADS = 4
GLA_DK = 96
GLA_DV = 192
GLA_DK_PAD = 128
GLA_DV_PAD = 256
GLA_KEY_WIDTH = GLA_HEADS * GLA_DK
GLA_WIDTH = GLA_HEADS * GLA_DV
GLA_RANK = 16
GLA_TAU = 16.0
GLA_CHUNK = 64
FFN_HIDDEN = 5632

OFF_GV = 0
OFF_GR = OFF_GV + GLA_HEADS * GLA_DV_PAD
OFF_GQ = OFF_GR + GLA_HEADS * GLA_DV_PAD
OFF_GK = OFF_GQ + GLA_HEADS * GLA_DK_PAD
OFF_U = OFF_GK + GLA_HEADS * GLA_DK_PAD
OFF_V = OFF_U + GMLP_WIDTH
OFF_DQ = OFF_V + GMLP_WIDTH
OFF_DK = OFF_DQ + DIFF_WIDTH
OFF_DV = OFF_DK + DIFF_WIDTH
OFF_CODE = OFF_DV + DIFF_WIDTH
PROJ_TN = 512
PROJ_WIDTH = -(-(OFF_CODE + LANES) // PROJ_TN) * PROJ_TN
MIX_C_WIDTH = GLA_HEADS * GLA_DV_PAD

VMEM_LIMIT = 56 * 1024 * 1024


def _cparams(sem):
    return pltpu.CompilerParams(dimension_semantics=sem, vmem_limit_bytes=VMEM_LIMIT)


def _rmsnorm_rows(x_ref, g_ref, h_ref, rows):
    tm = x_ref.shape[0]
    g = g_ref[...]
    for r in range(0, tm, rows):
        x = x_ref[r:r + rows, :]
        ms = jnp.mean(x * x, axis=-1, keepdims=True)
        h_ref[r:r + rows, :] = (x * lax.rsqrt(ms + EPS) * g).astype(BF16)


def _norm_mm_kernel(x_ref, g_ref, w_ref, cs_ref, o_ref, h_ref):
    @pl.when(pl.program_id(1) == 0)
    def _():
        _rmsnorm_rows(x_ref, g_ref, h_ref, 256)

    acc = jnp.dot(h_ref[...], w_ref[...], preferred_element_type=F32)
    o_ref[...] = (acc * cs_ref[...]).astype(o_ref.dtype)


def _swiglu_kernel(h_ref, wg_ref, wu_ref, o_ref):
    h = h_ref[...]
    gate = jnp.dot(h, wg_ref[...], preferred_element_type=F32)
    up = jnp.dot(h, wu_ref[...], preferred_element_type=F32)
    o_ref[...] = (gate / (1.0 + jnp.exp(-gate)) * up).astype(o_ref.dtype)


def _norm_matmul(x, g, w, col_scale, tm, tn):
    t, d = x.shape
    n = w.shape[1]
    return pl.pallas_call(
        _norm_mm_kernel,
        out_shape=jax.ShapeDtypeStruct((t, n), BF16),
        grid=(t // tm, n // tn),
        in_specs=[
            pl.BlockSpec((tm, d), lambda i, j: (i, 0)),
            pl.BlockSpec((1, d), lambda i, j: (0, 0)),
            pl.BlockSpec((d, tn), lambda i, j: (0, j)),
            pl.BlockSpec((1, tn), lambda i, j: (0, j)),
        ],
        out_specs=pl.BlockSpec((tm, tn), lambda i, j: (i, j)),
        scratch_shapes=[pltpu.VMEM((tm, d), BF16)],
        compiler_params=_cparams(("parallel", "arbitrary")),
        name="norm_in_proj",
    )(x, g, w, col_scale)


def _swiglu(h, w, layer, tm, tn):
    t, d = h.shape
    hidden = w.shape[2] // 2
    nj = hidden // tn
    return pl.pallas_call(
        _swiglu_kernel,
        out_shape=jax.ShapeDtypeStruct((t, hidden), BF16),
        grid=(t // tm, nj),
        in_specs=[
            pl.BlockSpec((tm, d), lambda i, j: (i, 0)),
            pl.BlockSpec((None, d, tn), lambda i, j: (layer, 0, j)),
            pl.BlockSpec((None, d, tn), lambda i, j: (layer, 0, j + nj)),
        ],
        out_specs=pl.BlockSpec((tm, tn), lambda i, j: (i, j)),
        compiler_params=_cparams(("parallel", "arbitrary")),
        name="ffn_in",
    )(h, w, w)


def _mm_res_kernel(*refs, n_a):
    a_refs, w_refs = refs[:n_a], refs[n_a:2 * n_a]
    r_ref, o_ref = refs[2 * n_a], refs[2 * n_a + 1]
    acc = r_ref[...]
    for a_ref, w_ref in zip(a_refs, w_refs):
        acc = acc + jnp.dot(a_ref[...], w_ref[...], preferred_element_type=F32)
    o_ref[...] = acc


def _mm_res_norm_kernel(*refs, n_a):
    a_refs, w_refs = refs[:n_a], refs[n_a:2 * n_a]
    r_ref, g_ref, o_ref, h_ref = refs[2 * n_a:2 * n_a + 4]
    acc = r_ref[...]
    for a_ref, w_ref in zip(a_refs, w_refs):
        acc = acc + jnp.dot(a_ref[...], w_ref[...], preferred_element_type=F32)
    o_ref[...] = acc
    ms = jnp.mean(acc * acc, axis=-1, keepdims=True)
    h_ref[...] = (acc * lax.rsqrt(ms + EPS) * g_ref[...]).astype(h_ref.dtype)


def _matmul_residual_norm(a_list, w_list, layer, res, g, tm, name):
    t, n = res.shape
    n_a = len(a_list)
    in_specs = [pl.BlockSpec((tm, a.shape[1]), lambda i: (i, 0)) for a in a_list]
    in_specs += [pl.BlockSpec((None, w.shape[1], n), lambda i: (layer, 0, 0)) for w in w_list]
    in_specs += [pl.BlockSpec((tm, n), lambda i: (i, 0)), pl.BlockSpec((1, n), lambda i: (0, 0))]
    return pl.pallas_call(
        functools.partial(_mm_res_norm_kernel, n_a=n_a),
        out_shape=(jax.ShapeDtypeStruct((t, n), F32), jax.ShapeDtypeStruct((t, n), BF16)),
        grid=(t // tm,),
        in_specs=in_specs,
        out_specs=(pl.BlockSpec((tm, n), lambda i: (i, 0)), pl.BlockSpec((tm, n), lambda i: (i, 0))),
        compiler_params=_cparams(("parallel",)),
        name=name,
    )(*a_list, *w_list, res, g)


def _matmul_residual(a_list, w_list, layer, res, tm, tn, name):
    t, n = res.shape
    n_a = len(a_list)
    in_specs = [pl.BlockSpec((tm, a.shape[1]), lambda i, j: (i, 0)) for a in a_list]
    in_specs += [pl.BlockSpec((None, w.shape[1], tn), lambda i, j: (layer, 0, j)) for w in w_list]
    in_specs += [pl.BlockSpec((tm, tn), lambda i, j: (i, j))]
    return pl.pallas_call(
        functools.partial(_mm_res_kernel, n_a=n_a),
        out_shape=jax.ShapeDtypeStruct((t, n), F32),
        grid=(t // tm, n // tn),
        in_specs=in_specs,
        out_specs=pl.BlockSpec((tm, tn), lambda i, j: (i, j)),
        compiler_params=_cparams(("parallel", "arbitrary")),
        name=name,
    )(*a_list, *w_list, res)


def _gelu_tanh(x):
    return 0.5 * x * (1.0 + jnp.tanh(math.sqrt(2.0 / math.pi) * (x + 0.044715 * (x * x * x))))


def _gmlp_kernel(u_ref, v_ref, lng_ref, w_ref, b_ref, o_ref):
    tile = u_ref.shape[0]
    gd = GMLP_WIDTH // GMLP_GROUPS
    v = _gelu_tanh(v_ref[...].astype(F32))
    mu = jnp.mean(v, axis=-1, keepdims=True)
    vc = v - mu
    var = jnp.mean(vc * vc, axis=-1, keepdims=True)
    vn = (vc * lax.rsqrt(var + EPS) * lng_ref[...]).astype(BF16)
    row = lax.broadcasted_iota(jnp.int32, (GMLP_CHUNK, GMLP_CHUNK), 0)
    col = lax.broadcasted_iota(jnp.int32, (GMLP_CHUNK, GMLP_CHUNK), 1)
    causal = col <= row
    bias = b_ref[...]
    for g in range(GMLP_GROUPS):
        wg = jnp.where(causal, w_ref[g], 0.0).astype(BF16)
        bg = bias[:, g:g + 1]
        for c in range(tile // GMLP_CHUNK):
            rs = slice(c * GMLP_CHUNK, (c + 1) * GMLP_CHUNK)
            cs = slice(g * gd, (g + 1) * gd)
            mixed = jnp.dot(wg, vn[rs, cs], preferred_element_type=F32) + bg
            u = _gelu_tanh(u_ref[rs, cs].astype(F32))
            o_ref[rs, cs] = (u * mixed).astype(o_ref.dtype)


def _gmlp_mixer(proj, ln_g, spatial_w, spatial_b_t, tile):
    t = proj.shape[0]
    return pl.pallas_call(
        _gmlp_kernel,
        out_shape=jax.ShapeDtypeStruct((t, GMLP_WIDTH), BF16),
        grid=(t // tile,),
        in_specs=[
            pl.BlockSpec((tile, GMLP_WIDTH), lambda i: (i, OFF_U // GMLP_WIDTH)),
            pl.BlockSpec((tile, GMLP_WIDTH), lambda i: (i, OFF_V // GMLP_WIDTH)),
            pl.BlockSpec((1, GMLP_WIDTH), lambda i: (0, 0)),
            pl.BlockSpec((GMLP_GROUPS, GMLP_CHUNK, GMLP_CHUNK), lambda i: (0, 0, 0)),
            pl.BlockSpec((GMLP_CHUNK, GMLP_GROUPS), lambda i: (0, 0)),
        ],
        out_specs=pl.BlockSpec((tile, GMLP_WIDTH), lambda i: (i, 0)),
        compiler_params=_cparams(("parallel",)),
        name="gmlp_mixer",
    )(proj, proj, ln_g, spatial_w, spatial_b_t)


def _diff_attn_kernel(lam_ref, q_ref, k_ref, v_ref, g_ref, o_ref, vt_ref, qt_ref, sa_ref, sb_ref, pa_ref, pb_ref,
                      ca_ref, cb_ref, sta_ref, stb_ref, psa_ref, psb_ref, pc_ref, cc_ref, stc_ref, psc_ref,
                      m_ref, l_ref, acc_ref, *, blk, lambda_init):
    i = pl.program_id(1)
    n_blk = vt_ref.shape[0]

    @pl.when(i == 0)
    def _():
        def transpose_v(c, carry):
            r0 = pl.multiple_of(c * blk, blk)
            vt_ref[c] = v_ref[pl.ds(r0, blk), :].astype(F32).T.astype(BF16)
            return carry
        lax.fori_loop(0, n_blk, transpose_v, 0)

    qt = q_ref[...].astype(F32).T
    sub = lax.broadcasted_iota(jnp.int32, (LANES, blk), 0)
    qt_ref[0] = jnp.where(sub < DIFF_HEAD_DIM, qt, 0.0).astype(BF16)
    qt_ref[1] = jnp.where(sub >= DIFF_HEAD_DIM, qt, 0.0).astype(BF16)
    def reset_state():
        m_ref[...] = jnp.full(m_ref.shape, -jnp.inf, F32)
        l_ref[...] = jnp.zeros(l_ref.shape, F32)
        acc_ref[...] = jnp.zeros(acc_ref.shape, F32)

    def k_block(j):
        return k_ref[pl.ds(pl.multiple_of(j * blk, blk), blk), :]

    def causal_mask(s):
        row = lax.broadcasted_iota(jnp.int32, (blk, blk), 0)
        col = lax.broadcasted_iota(jnp.int32, (blk, blk), 1)
        return jnp.where(row <= col, s, -jnp.inf)

    def probs(j, cur, prev, first=False, masked=False):
        p_ref, st_ref, c_ref, ps_ref = cur
        kb = k_block(j)
        for h in range(2):
            s = jnp.dot(kb, qt_ref[h], preferred_element_type=F32)
            if masked:
                s = causal_mask(s)
            c = jnp.max(s, axis=0, keepdims=True)
            stab = c if first else jnp.maximum(prev[1][h], prev[2][h])
            p = jnp.exp2(s - stab)
            ps_ref[h] = jnp.sum(p, axis=0, keepdims=True)
            p_ref[h] = p.astype(BF16)
            c_ref[h] = c
            st_ref[h] = stab

    def fold(j, cur):
        p_ref, st_ref, _, ps_ref = cur
        vtb = vt_ref[j]
        for h in range(2):
            stab = st_ref[h]
            alpha = jnp.exp2(m_ref[h] - stab)
            l_ref[h] = alpha * l_ref[h] + ps_ref[h]
            acc_ref[h] = alpha * acc_ref[h] + jnp.dot(vtb, p_ref[h], preferred_element_type=F32)
            m_ref[h] = stab

    buf_a = (pa_ref, sta_ref, ca_ref, psa_ref)
    buf_b = (pb_ref, stb_ref, cb_ref, psb_ref)
    buf_c = (pc_ref, stc_ref, cc_ref, psc_ref)

    def fast_two_blocks(j):
        probs(j + 1, buf_b, buf_a)
        fold(j, buf_a)
        probs(j + 2, buf_a, buf_b)
        fold(j + 1, buf_b)

    reset_state()
    n_pairs = jnp.maximum(i - 1, 0) // 2

    @pl.when(i > 0)
    def _():
        probs(0, buf_a, None, first=True)

    def fast_oct(p, carry):
        for u in range(4):
            fast_two_blocks(8 * p + 2 * u)
        return carry

    def fast_pair(p, carry):
        fast_two_blocks(8 * (n_pairs // 4) + 2 * p)
        return carry

    lax.fori_loop(0, n_pairs // 4, fast_oct, 0)
    lax.fori_loop(0, n_pairs % 4, fast_pair, 0)
    j0 = 2 * n_pairs

    @pl.when(i == 0)
    def _():
        probs(0, buf_a, None, first=True, masked=True)
        fold(0, buf_a)

    @pl.when(i % 2 == 1)
    def _():
        probs(i, buf_b, buf_a, masked=True)
        fold(j0, buf_a)
        fold(i, buf_b)

    @pl.when((i % 2 == 0) & (i > 0))
    def _():
        probs(j0 + 1, buf_b, buf_a)
        fold(j0, buf_a)
        probs(i, buf_c, buf_b, masked=True)
        fold(j0 + 1, buf_b)
        fold(i, buf_c)

    def scores(j, s_ref, c_ref):
        kb = k_block(j)
        for h in range(2):
            s = jnp.dot(kb, qt_ref[h], preferred_element_type=F32)
            s_ref[h] = s
            c_ref[h] = jnp.max(s, axis=0, keepdims=True)

    def accumulate(j, s_ref, c_ref, masked):
        vtb = vt_ref[j]
        for h in range(2):
            s = s_ref[h]
            if masked:
                s = causal_mask(s)
                c = jnp.max(s, axis=0, keepdims=True)
            else:
                c = c_ref[h]
            m_prev = m_ref[h]
            m_new = jnp.maximum(m_prev, c)
            alpha = jnp.exp2(m_prev - m_new)
            p = jnp.exp2(s - m_new)
            l_ref[h] = alpha * l_ref[h] + jnp.sum(p, axis=0, keepdims=True)
            acc_ref[h] = alpha * acc_ref[h] + jnp.dot(vtb, p.astype(BF16), preferred_element_type=F32)
            m_ref[h] = m_new

    total = jnp.sum(l_ref[...]) + jnp.sum(acc_ref[...])
    overflowed = jnp.logical_not(jnp.abs(total) < jnp.inf)

    @pl.when(overflowed)
    def _():
        reset_state()
        scores(0, sa_ref, ca_ref)

        def slow_pair(p, carry):
            j = 2 * p
            scores(j + 1, sb_ref, cb_ref)
            accumulate(j, sa_ref, ca_ref, False)
            scores(j + 2, sa_ref, ca_ref)
            accumulate(j + 1, sb_ref, cb_ref, False)
            return carry

        lax.fori_loop(0, i // 2, slow_pair, 0)

        @pl.when(i % 2 == 1)
        def _():
            scores(i, sb_ref, cb_ref)
            accumulate(i - 1, sa_ref, ca_ref, False)
            accumulate(i, sb_ref, cb_ref, True)

        @pl.when(i % 2 == 0)
        def _():
            accumulate(i, sa_ref, ca_ref, True)

    lf = lam_ref[...]
    lam = (jnp.exp(jnp.sum(lf[0:1] * lf[1:2], axis=1, keepdims=True))
           - jnp.exp(jnp.sum(lf[2:3] * lf[3:4], axis=1, keepdims=True)) + lambda_init)
    o = (acc_ref[0] / l_ref[0] - lam * (acc_ref[1] / l_ref[1])).T
    ms = jnp.mean(o * o, axis=-1, keepdims=True)
    o = o * lax.rsqrt(ms + EPS) * g_ref[...] * (1.0 - lambda_init)
    o_ref[...] = o.astype(o_ref.dtype)


def _diff_attention(proj, lambdas, norm_g, lambda_init, blk):
    t = proj.shape[0]
    qo, ko, vo = OFF_DQ // LANES, OFF_DK // LANES, OFF_DV // LANES
    return pl.pallas_call(
        functools.partial(_diff_attn_kernel, blk=blk, lambda_init=lambda_init),
        out_shape=jax.ShapeDtypeStruct((t, DIFF_WIDTH), BF16),
        grid=(DIFF_HEADS, t // blk),
        in_specs=[
            pl.BlockSpec((4, DIFF_HEAD_DIM), lambda h, i: (0, 0)),
            pl.BlockSpec((blk, LANES), lambda h, i: (i, qo + h)),
            pl.BlockSpec((t, LANES), lambda h, i: (0, ko + h)),
            pl.BlockSpec((t, LANES), lambda h, i: (0, vo + h)),
            pl.BlockSpec((1, LANES), lambda h, i: (0, 0)),
        ],
        out_specs=pl.BlockSpec((blk, LANES), lambda h, i: (i, h)),
        scratch_shapes=[
            pltpu.VMEM((t // blk, LANES, blk), BF16),
            pltpu.VMEM((2, LANES, blk), BF16),
            pltpu.VMEM((2, blk, blk), F32),
            pltpu.VMEM((2, blk, blk), F32),
            pltpu.VMEM((2, blk, blk), BF16),
            pltpu.VMEM((2, blk, blk), BF16),
            pltpu.VMEM((2, 1, blk), F32),
            pltpu.VMEM((2, 1, blk), F32),
            pltpu.VMEM((2, 1, blk), F32),
            pltpu.VMEM((2, 1, blk), F32),
            pltpu.VMEM((2, 1, blk), F32),
            pltpu.VMEM((2, 1, blk), F32),
            pltpu.VMEM((2, blk, blk), BF16),
            pltpu.VMEM((2, 1, blk), F32),
            pltpu.VMEM((2, 1, blk), F32),
            pltpu.VMEM((2, 1, blk), F32),
            pltpu.VMEM((2, 1, blk), F32),
            pltpu.VMEM((2, 1, blk), F32),
            pltpu.VMEM((2, LANES, blk), F32),
        ],
        compiler_params=_cparams(("arbitrary", "arbitrary")),
        name="diff_attention",
    )(lambdas, proj, proj, proj, norm_g)


def _gla_kernel(gv_ref, gr_ref, gq_ref, gk_ref, code_ref, w2_ref, gb_ref, ng_ref, o_ref, st_ref, g_ref):
    tile = gq_ref.shape[0]
    c_len = GLA_CHUNK

    @pl.when(pl.program_id(0) == 0)
    def _():
        st_ref[...] = jnp.zeros(st_ref.shape, F32)

    z = jnp.dot(code_ref[...], w2_ref[...], preferred_element_type=F32) + gb_ref[...]
    g_ref[...] = (jnp.minimum(z, 0.0) - jnp.log1p(jnp.exp(-jnp.abs(z)))) * (1.0 / GLA_TAU)

    row = lax.broadcasted_iota(jnp.int32, (c_len, c_len), 0)
    col = lax.broadcasted_iota(jnp.int32, (c_len, c_len), 1)
    causal = col <= row
    tri = jnp.where(causal, 1.0, 0.0).astype(BF16)
    ng = ng_ref[...]

    def chunk(c, carry):
        r0 = pl.multiple_of(c * c_len, c_len)
        g = g_ref[pl.ds(r0, c_len), :]
        g_hi = g.astype(BF16)
        g_lo = (g - g_hi.astype(F32)).astype(BF16)
        b = (jnp.dot(tri, g_hi, preferred_element_type=F32)
             + jnp.dot(tri, g_lo, preferred_element_type=F32))
        b_last = b[c_len - 1:c_len, :]
        q = gq_ref[pl.ds(r0, c_len), :].astype(F32) * (GLA_DK ** -0.5)
        k = gk_ref[pl.ds(r0, c_len), :].astype(F32)
        qd = (q * jnp.exp(b)).astype(BF16)
        kd = (k * jnp.exp(-b)).astype(BF16)
        kl = (k * jnp.exp(b_last - b)).astype(BF16)
        e_last = jnp.exp(b_last)
        for h in range(GLA_HEADS):
            ks = slice(h * GLA_DK_PAD, (h + 1) * GLA_DK_PAD)
            vs = slice(h * GLA_DV_PAD, (h + 1) * GLA_DV_PAD)
            vh = gv_ref[pl.ds(r0, c_len), vs]
            st = st_ref[h]
            a = lax.dot_general(qd[:, ks], kd[:, ks], (((1,), (1,)), ((), ())), preferred_element_type=F32)
            a = jnp.where(causal, a, 0.0).astype(BF16)
            o = jnp.dot(a, vh, preferred_element_type=F32)
            o = o + lax.dot_general(qd[:, ks], st.astype(BF16), (((1,), (1,)), ((), ())),
                                    preferred_element_type=F32)
            st_ref[h] = st * e_last[:, ks] + lax.dot_general(
                vh, kl[:, ks], (((0,), (0,)), ((), ())), preferred_element_type=F32)
            ms = jnp.sum(o * o, axis=-1, keepdims=True) * (1.0 / GLA_DV)
            on = o * lax.rsqrt(ms + EPS) * ng
            r = gr_ref[pl.ds(r0, c_len), vs].astype(F32)
            o_ref[pl.ds(r0, c_len), vs] = (on * (r / (1.0 + jnp.exp(-r)))).astype(o_ref.dtype)
        return carry

    lax.fori_loop(0, tile // c_len, chunk, 0, unroll=True)


def _gla_mixer(proj, w2p, gbp, ngp, tile):
    t = proj.shape[0]
    kw = GLA_HEADS * GLA_DK_PAD
    return pl.pallas_call(
        _gla_kernel,
        out_shape=jax.ShapeDtypeStruct((t, MIX_C_WIDTH), BF16),
        grid=(t // tile,),
        in_specs=[
            pl.BlockSpec((tile, MIX_C_WIDTH), lambda i: (i, OFF_GV // MIX_C_WIDTH)),
            pl.BlockSpec((tile, MIX_C_WIDTH), lambda i: (i, OFF_GR // MIX_C_WIDTH)),
            pl.BlockSpec((tile, kw), lambda i: (i, OFF_GQ // kw)),
            pl.BlockSpec((tile, kw), lambda i: (i, OFF_GK // kw)),
            pl.BlockSpec((tile, LANES), lambda i: (i, OFF_CODE // LANES)),
            pl.BlockSpec((LANES, kw), lambda i: (0, 0)),
            pl.BlockSpec((1, kw), lambda i: (0, 0)),
            pl.BlockSpec((1, GLA_DV_PAD), lambda i: (0, 0)),
        ],
        out_specs=pl.BlockSpec((tile, MIX_C_WIDTH), lambda i: (i, 0)),
        scratch_shapes=[
            pltpu.VMEM((GLA_HEADS, GLA_DV_PAD, GLA_DK_PAD), F32),
            pltpu.VMEM((tile, kw), F32),
        ],
        compiler_params=_cparams(("arbitrary",)),
        name="gla_mixer",
    )(proj, proj, proj, proj, proj, w2p, gbp, ngp)


def _final_norm_kernel(x_ref, g_ref, o_ref):
    x = x_ref[...]
    ms = jnp.mean(x * x, axis=-1, keepdims=True)
    o_ref[...] = x * lax.rsqrt(ms + EPS) * g_ref[...]


def _final_norm(x, g, tm):
    t, d = x.shape
    return pl.pallas_call(
        _final_norm_kernel,
        out_shape=jax.ShapeDtypeStruct((t, d), F32),
        grid=(t // tm,),
        in_specs=[pl.BlockSpec((tm, d), lambda i: (i, 0)), pl.BlockSpec((1, d), lambda i: (0, 0))],
        out_specs=pl.BlockSpec((tm, d), lambda i: (i, 0)),
        compiler_params=_cparams(("parallel",)),
        name="final_norm",
    )(x, g)


def _pad_heads(w, heads, d, d_pad):
    lead = w.shape[:-1]
    w = w.reshape(lead + (heads, d))
    w = jnp.pad(w, [(0, 0)] * len(lead) + [(0, 0), (0, d_pad - d)])
    return w.reshape(lead + (heads * d_pad,))


def _head_pieces(w, axis, off, heads, d, d_pad):
    pieces = []
    for h in range(heads):
        sl = lax.slice_in_dim(w, off + h * d, off + (h + 1) * d, axis=axis)
        pieces.append(sl)
        zshape = list(w.shape)
        zshape[axis] = d_pad - d
        pieces.append(jnp.zeros(zshape, w.dtype))
    return pieces


def _layout_w_in(w_in):
    splits = [GMLP_WIDTH, GMLP_WIDTH, DIFF_WIDTH, DIFF_WIDTH, DIFF_WIDTH,
              GLA_KEY_WIDTH, GLA_KEY_WIDTH, GLA_WIDTH, GLA_WIDTH, GLA_RANK]
    idx = [0]
    for s in splits:
        idx.append(idx[-1] + s)
    o_u, o_v, o_dq, o_dk, o_dv, o_gq, o_gk, o_gv, o_gr, o_code, end = idx
    ax = w_in.ndim - 1
    pieces = (_head_pieces(w_in, ax, o_gv, GLA_HEADS, GLA_DV, GLA_DV_PAD)
              + _head_pieces(w_in, ax, o_gr, GLA_HEADS, GLA_DV, GLA_DV_PAD)
              + _head_pieces(w_in, ax, o_gq, GLA_HEADS, GLA_DK, GLA_DK_PAD)
              + _head_pieces(w_in, ax, o_gk, GLA_HEADS, GLA_DK, GLA_DK_PAD)
              + [lax.slice_in_dim(w_in, o_u, o_gq, axis=ax),
                 lax.slice_in_dim(w_in, o_code, end, axis=ax),
                 jnp.zeros(w_in.shape[:-1] + (PROJ_WIDTH - OFF_CODE - GLA_RANK,), w_in.dtype)])
    return jnp.concatenate(pieces, axis=ax).astype(BF16)


def _layout_w_out(w_out):
    wa = w_out[:, :GMLP_WIDTH, :].astype(BF16)
    wb = w_out[:, GMLP_WIDTH:GMLP_WIDTH + DIFF_WIDTH, :].astype(BF16)
    pieces = _head_pieces(w_out, 1, GMLP_WIDTH + DIFF_WIDTH, GLA_HEADS, GLA_DV, GLA_DV_PAD)
    return wa, wb, jnp.concatenate(pieces, axis=1).astype(BF16)


def kernel(x, norm1_g, w_in, gmlp_ln_g, spatial_w, spatial_b, diff_lambdas, diff_norm_g, gla_gate_w2,
           gla_gate_b, gla_norm_g, w_out, norm2_g, w_ffn_in, w_ffn_out, final_g):
    b, t, d = x.shape
    assert b == 1 and d == D_MODEL
    xs = x.reshape(t, d)

    w2p = jnp.pad(_pad_heads(gla_gate_w2, GLA_HEADS, GLA_DK, GLA_DK_PAD),
                  [(0, 0), (0, LANES - GLA_RANK), (0, 0)]).astype(BF16)
    gbp = _pad_heads(gla_gate_b, GLA_HEADS, GLA_DK, GLA_DK_PAD)[:, None, :]
    ngp = jnp.pad(gla_norm_g, [(0, 0), (0, GLA_DV_PAD - GLA_DV)])[:, None, :]
    sb_t = jnp.swapaxes(spatial_b, 1, 2)

    col = jnp.arange(PROJ_WIDTH)
    q_scale = (DIFF_HEAD_DIM ** -0.5) * math.log2(math.e)
    col_scale = jnp.where((col >= OFF_DQ) & (col < OFF_DK), q_scale, 1.0).astype(F32)[None, :]

    wa, wb, wc = _layout_w_out(w_out)
    wf_in = w_ffn_in.astype(BF16)
    wf_out = w_ffn_out.astype(BF16)

    for l in range(DEPTH):
        lambda_init = 0.8 - 0.6 * math.exp(-0.3 * l)
        w1 = _layout_w_in(w_in[l])
        proj = _norm_matmul(xs, norm1_g[l][None, :], w1, col_scale, min(512, t), PROJ_WIDTH // 2)
        a_out = _gmlp_mixer(proj, gmlp_ln_g[l][None, :], spatial_w[l], sb_t[l], min(512, t))
        b_out = _diff_attention(proj, diff_lambdas[l], diff_norm_g[l][None, :], lambda_init, min(512, t))
        c_out = _gla_mixer(proj, w2p[l], gbp[l], ngp[l], min(256, t))
        xs, h2 = _matmul_residual_norm([a_out, b_out, c_out], [wa, wb, wc], l, xs, norm2_g[l][None, :],
                                       min(512, t), "out_proj")
        hid = _swiglu(h2, wf_in, l, min(2048, t), 512)
        xs = _matmul_residual([hid], [wf_out], l, xs, min(1024, t), 512, "ffn_out")
    out = _final_norm(xs, final_g[None, :], min(512, t))
    return out.reshape(b, t, d)
```

```python
import functools
import math

import jax
import jax.numpy as jnp
from jax import lax
from jax.experimental import pallas as pl
from jax.experimental.pallas import tpu as pltpu

F32 = jnp.float32
BF16 = jnp.bfloat16

D_MODEL = 2048
DEPTH = 4
EPS = 1e-6
LANES = 128

GMLP_WIDTH = 512
GMLP_GROUPS = 4
GMLP_CHUNK = 128
DIFF_WIDTH = 768
DIFF_HEADS = 6
DIFF_HEAD_DIM = 64
GLA_HEADS = 4
GLA_DK = 96
GLA_DV = 192
GLA_DK_PAD = 128
GLA_DV_PAD = 256
GLA_KEY_WIDTH = GLA_HEADS * GLA_DK
GLA_WIDTH = GLA_HEADS * GLA_DV
GLA_RANK = 16
GLA_TAU = 16.0
GLA_CHUNK = 64
GLA_SAFE_LOG_DECAY = 80.0
FFN_HIDDEN = 5632

OFF_GV = 0
OFF_GR = OFF_GV + GLA_HEADS * GLA_DV_PAD
OFF_GQ = OFF_GR + GLA_HEADS * GLA_DV_PAD
OFF_GK = OFF_GQ + GLA_HEADS * GLA_DK_PAD
OFF_U = OFF_GK + GLA_HEADS * GLA_DK_PAD
OFF_V = OFF_U + GMLP_WIDTH
OFF_DQ = OFF_V + GMLP_WIDTH
OFF_DK = OFF_DQ + DIFF_WIDTH
OFF_DV = OFF_DK + DIFF_WIDTH
OFF_CODE = OFF_DV + DIFF_WIDTH
PROJ_TN = 512
PROJ_WIDTH = -(-(OFF_CODE + LANES) // PROJ_TN) * PROJ_TN
MIX_C_WIDTH = GLA_HEADS * GLA_DV_PAD

VMEM_LIMIT = 56 * 1024 * 1024


def _cparams(sem):
    return pltpu.CompilerParams(dimension_semantics=sem, vmem_limit_bytes=VMEM_LIMIT)


def _rmsnorm_rows(x_ref, g_ref, h_ref, rows):
    tm = x_ref.shape[0]
    g = g_ref[...]
    for r in range(0, tm, rows):
        x = x_ref[r:r + rows, :]
        ms = jnp.mean(x * x, axis=-1, keepdims=True)
        h_ref[r:r + rows, :] = (x * lax.rsqrt(ms + EPS) * g).astype(BF16)


def _norm_mm_kernel(x_ref, g_ref, w_ref, cs_ref, o_ref, h_ref):
    @pl.when(pl.program_id(1) == 0)
    def _():
        _rmsnorm_rows(x_ref, g_ref, h_ref, 256)

    acc = jnp.dot(h_ref[...], w_ref[...], preferred_element_type=F32)
    o_ref[...] = (acc * cs_ref[...]).astype(o_ref.dtype)


def _swiglu_kernel(h_ref, wg_ref, wu_ref, o_ref):
    h = h_ref[...]
    gate = jnp.dot(h, wg_ref[...], preferred_element_type=F32)
    up = jnp.dot(h, wu_ref[...], preferred_element_type=F32)
    o_ref[...] = (gate / (1.0 + jnp.exp(-gate)) * up).astype(o_ref.dtype)


def _norm_matmul(x, g, w, col_scale, tm, tn):
    t, d = x.shape
    n = w.shape[1]
    return pl.pallas_call(
        _norm_mm_kernel,
        out_shape=jax.ShapeDtypeStruct((t, n), BF16),
        grid=(t // tm, n // tn),
        in_specs=[
            pl.BlockSpec((tm, d), lambda i, j: (i, 0)),
            pl.BlockSpec((1, d), lambda i, j: (0, 0)),
            pl.BlockSpec((d, tn), lambda i, j: (0, j)),
            pl.BlockSpec((1, tn), lambda i, j: (0, j)),
        ],
        out_specs=pl.BlockSpec((tm, tn), lambda i, j: (i, j)),
        scratch_shapes=[pltpu.VMEM((tm, d), BF16)],
        compiler_params=_cparams(("parallel", "arbitrary")),
        name="norm_in_proj",
    )(x, g, w, col_scale)


def _swiglu(h, w, layer, tm, tn):
    t, d = h.shape
    hidden = w.shape[2] // 2
    nj = hidden // tn
    return pl.pallas_call(
        _swiglu_kernel,
        out_shape=jax.ShapeDtypeStruct((t, hidden), BF16),
        grid=(t // tm, nj),
        in_specs=[
            pl.BlockSpec((tm, d), lambda i, j: (i, 0)),
            pl.BlockSpec((None, d, tn), lambda i, j: (layer, 0, j)),
            pl.BlockSpec((None, d, tn), lambda i, j: (layer, 0, j + nj)),
        ],
        out_specs=pl.BlockSpec((tm, tn), lambda i, j: (i, j)),
        compiler_params=_cparams(("parallel", "arbitrary")),
        name="ffn_in",
    )(h, w, w)


def _mm_res_kernel(*refs, n_a):
    a_refs, w_refs = refs[:n_a], refs[n_a:2 * n_a]
    r_ref, o_ref = refs[2 * n_a], refs[2 * n_a + 1]
    acc = r_ref[...]
    for a_ref, w_ref in zip(a_refs, w_refs):
        acc = acc + jnp.dot(a_ref[...], w_ref[...], preferred_element_type=F32)
    o_ref[...] = acc


def _mm_res_norm_kernel(*refs, n_a):
    a_refs, w_refs = refs[:n_a], refs[n_a:2 * n_a]
    r_ref, g_ref, o_ref, h_ref = refs[2 * n_a:2 * n_a + 4]
    acc = r_ref[...]
    for a_ref, w_ref in zip(a_refs, w_refs):
        acc = acc + jnp.dot(a_ref[...], w_ref[...], preferred_element_type=F32)
    o_ref[...] = acc
    ms = jnp.mean(acc * acc, axis=-1, keepdims=True)
    h_ref[...] = (acc * lax.rsqrt(ms + EPS) * g_ref[...]).astype(h_ref.dtype)


def _matmul_residual_norm(a_list, w_list, layer, res, g, tm, name):
    t, n = res.shape
    n_a = len(a_list)
    in_specs = [pl.BlockSpec((tm, a.shape[1]), lambda i: (i, 0)) for a in a_list]
    in_specs += [pl.BlockSpec((None, w.shape[1], n), lambda i: (layer, 0, 0)) for w in w_list]
    in_specs += [pl.BlockSpec((tm, n), lambda i: (i, 0)), pl.BlockSpec((1, n), lambda i: (0, 0))]
    return pl.pallas_call(
        functools.partial(_mm_res_norm_kernel, n_a=n_a),
        out_shape=(jax.ShapeDtypeStruct((t, n), F32), jax.ShapeDtypeStruct((t, n), BF16)),
        grid=(t // tm,),
        in_specs=in_specs,
        out_specs=(pl.BlockSpec((tm, n), lambda i: (i, 0)), pl.BlockSpec((tm, n), lambda i: (i, 0))),
        compiler_params=_cparams(("parallel",)),
        name=name,
    )(*a_list, *w_list, res, g)


def _matmul_residual(a_list, w_list, layer, res, tm, tn, name):
    t, n = res.shape
    n_a = len(a_list)
    in_specs = [pl.BlockSpec((tm, a.shape[1]), lambda i, j: (i, 0)) for a in a_list]
    in_specs += [pl.BlockSpec((None, w.shape[1], tn), lambda i, j: (layer, 0, j)) for w in w_list]
    in_specs += [pl.BlockSpec((tm, tn), lambda i, j: (i, j))]
    return pl.pallas_call(
        functools.partial(_mm_res_kernel, n_a=n_a),
        out_shape=jax.ShapeDtypeStruct((t, n), F32),
        grid=(t // tm, n // tn),
        in_specs=in_specs,
        out_specs=pl.BlockSpec((tm, tn), lambda i, j: (i, j)),
        compiler_params=_cparams(("parallel", "arbitrary")),
        name=name,
    )(*a_list, *w_list, res)


def _gelu_tanh(x):
    return 0.5 * x * (1.0 + jnp.tanh(math.sqrt(2.0 / math.pi) * (x + 0.044715 * (x * x * x))))


def _gmlp_kernel(u_ref, v_ref, lng_ref, w_ref, b_ref, o_ref):
    tile = u_ref.shape[0]
    gd = GMLP_WIDTH // GMLP_GROUPS
    v = _gelu_tanh(v_ref[...].astype(F32))
    mu = jnp.mean(v, axis=-1, keepdims=True)
    vc = v - mu
    var = jnp.mean(vc * vc, axis=-1, keepdims=True)
    vn = (vc * lax.rsqrt(var + EPS) * lng_ref[...]).astype(BF16)
    row = lax.broadcasted_iota(jnp.int32, (GMLP_CHUNK, GMLP_CHUNK), 0)
    col = lax.broadcasted_iota(jnp.int32, (GMLP_CHUNK, GMLP_CHUNK), 1)
    causal = col <= row
    bias = b_ref[...]
    for g in range(GMLP_GROUPS):
        wg = jnp.where(causal, w_ref[g], 0.0).astype(BF16)
        bg = bias[:, g:g + 1]
        for c in range(tile // GMLP_CHUNK):
            rs = slice(c * GMLP_CHUNK, (c + 1) * GMLP_CHUNK)
            cs = slice(g * gd, (g + 1) * gd)
            mixed = jnp.dot(wg, vn[rs, cs], preferred_element_type=F32) + bg
            u = _gelu_tanh(u_ref[rs, cs].astype(F32))
            o_ref[rs, cs] = (u * mixed).astype(o_ref.dtype)


def _gmlp_mixer(proj, ln_g, spatial_w, spatial_b_t, tile):
    t = proj.shape[0]
    return pl.pallas_call(
        _gmlp_kernel,
        out_shape=jax.ShapeDtypeStruct((t, GMLP_WIDTH), BF16),
        grid=(t // tile,),
        in_specs=[
            pl.BlockSpec((tile, GMLP_WIDTH), lambda i: (i, OFF_U // GMLP_WIDTH)),
            pl.BlockSpec((tile, GMLP_WIDTH), lambda i: (i, OFF_V // GMLP_WIDTH)),
            pl.BlockSpec((1, GMLP_WIDTH), lambda i: (0, 0)),
            pl.BlockSpec((GMLP_GROUPS, GMLP_CHUNK, GMLP_CHUNK), lambda i: (0, 0, 0)),
            pl.BlockSpec((GMLP_CHUNK, GMLP_GROUPS), lambda i: (0, 0)),
        ],
        out_specs=pl.BlockSpec((tile, GMLP_WIDTH), lambda i: (i, 0)),
        compiler_params=_cparams(("parallel",)),
        name="gmlp_mixer",
    )(proj, proj, ln_g, spatial_w, spatial_b_t)


def _diff_attn_kernel(lam_ref, q_ref, k_ref, v_ref, g_ref, o_ref, vt_ref, qt_ref, sa_ref, sb_ref, pa_ref, pb_ref,
                      ca_ref, cb_ref, sta_ref, stb_ref, psa_ref, psb_ref, pc_ref, cc_ref, stc_ref, psc_ref,
                      m_ref, l_ref, acc_ref, *, blk, lambda_init):
    i = pl.program_id(1)
    n_blk = vt_ref.shape[0]

    @pl.when(i == 0)
    def _():
        def transpose_v(c, carry):
            r0 = pl.multiple_of(c * blk, blk)
            vt_ref[c] = v_ref[pl.ds(r0, blk), :].astype(F32).T.astype(BF16)
            return carry
        lax.fori_loop(0, n_blk, transpose_v, 0)

    qt = q_ref[...].astype(F32).T
    sub = lax.broadcasted_iota(jnp.int32, (LANES, blk), 0)
    qt_ref[0] = jnp.where(sub < DIFF_HEAD_DIM, qt, 0.0).astype(BF16)
    qt_ref[1] = jnp.where(sub >= DIFF_HEAD_DIM, qt, 0.0).astype(BF16)
    def reset_state():
        m_ref[...] = jnp.full(m_ref.shape, -jnp.inf, F32)
        l_ref[...] = jnp.zeros(l_ref.shape, F32)
        acc_ref[...] = jnp.zeros(acc_ref.shape, F32)

    def k_block(j):
        return k_ref[pl.ds(pl.multiple_of(j * blk, blk), blk), :]

    def causal_mask(s):
        row = lax.broadcasted_iota(jnp.int32, (blk, blk), 0)
        col = lax.broadcasted_iota(jnp.int32, (blk, blk), 1)
        return jnp.where(row <= col, s, -jnp.inf)

    def probs(j, cur, prev, first=False, masked=False):
        p_ref, st_ref, c_ref, ps_ref = cur
        kb = k_block(j)
        for h in range(2):
            s = jnp.dot(kb, qt_ref[h], preferred_element_type=F32)
            if masked:
                s = causal_mask(s)
            c = jnp.max(s, axis=0, keepdims=True)
            stab = c if first else jnp.maximum(prev[1][h], prev[2][h])
            p = jnp.exp2(s - stab)
            ps_ref[h] = jnp.sum(p, axis=0, keepdims=True)
            p_ref[h] = p.astype(BF16)
            c_ref[h] = c
            st_ref[h] = stab

    def fold(j, cur):
        p_ref, st_ref, _, ps_ref = cur
        vtb = vt_ref[j]
        for h in range(2):
            stab = st_ref[h]
            alpha = jnp.exp2(m_ref[h] - stab)
            l_ref[h] = alpha * l_ref[h] + ps_ref[h]
            acc_ref[h] = alpha * acc_ref[h] + jnp.dot(vtb, p_ref[h], preferred_element_type=F32)
            m_ref[h] = stab

    buf_a = (pa_ref, sta_ref, ca_ref, psa_ref)
    buf_b = (pb_ref, stb_ref, cb_ref, psb_ref)
    buf_c = (pc_ref, stc_ref, cc_ref, psc_ref)

    def fast_two_blocks(j):
        probs(j + 1, buf_b, buf_a)
        fold(j, buf_a)
        probs(j + 2, buf_a, buf_b)
        fold(j + 1, buf_b)

    reset_state()
    n_pairs = jnp.maximum(i - 1, 0) // 2

    @pl.when(i > 0)
    def _():
        probs(0, buf_a, None, first=True)

    def fast_oct(p, carry):
        for u in range(4):
            fast_two_blocks(8 * p + 2 * u)
        return carry

    def fast_pair(p, carry):
        fast_two_blocks(8 * (n_pairs // 4) + 2 * p)
        return carry

    lax.fori_loop(0, n_pairs // 4, fast_oct, 0)
    lax.fori_loop(0, n_pairs % 4, fast_pair, 0)
    j0 = 2 * n_pairs

    @pl.when(i == 0)
    def _():
        probs(0, buf_a, None, first=True, masked=True)
        fold(0, buf_a)

    @pl.when(i % 2 == 1)
    def _():
        probs(i, buf_b, buf_a, masked=True)
        fold(j0, buf_a)
        fold(i, buf_b)

    @pl.when((i % 2 == 0) & (i > 0))
    def _():
        probs(j0 + 1, buf_b, buf_a)
        fold(j0, buf_a)
        probs(i, buf_c, buf_b, masked=True)
        fold(j0 + 1, buf_b)
        fold(i, buf_c)

    def scores(j, s_ref, c_ref):
        kb = k_block(j)
        for h in range(2):
            s = jnp.dot(kb, qt_ref[h], preferred_element_type=F32)
            s_ref[h] = s
            c_ref[h] = jnp.max(s, axis=0, keepdims=True)

    def accumulate(j, s_ref, c_ref, masked):
        vtb = vt_ref[j]
        for h in range(2):
            s = s_ref[h]
            if masked:
                s = causal_mask(s)
                c = jnp.max(s, axis=0, keepdims=True)
            else:
                c = c_ref[h]
            m_prev = m_ref[h]
            m_new = jnp.maximum(m_prev, c)
            alpha = jnp.exp2(m_prev - m_new)
            p = jnp.exp2(s - m_new)
            l_ref[h] = alpha * l_ref[h] + jnp.sum(p, axis=0, keepdims=True)
            acc_ref[h] = alpha * acc_ref[h] + jnp.dot(vtb, p.astype(BF16), preferred_element_type=F32)
            m_ref[h] = m_new

    total = jnp.sum(l_ref[...]) + jnp.sum(acc_ref[...])
    overflowed = jnp.logical_not(jnp.abs(total) < jnp.inf)

    @pl.when(overflowed)
    def _():
        reset_state()
        scores(0, sa_ref, ca_ref)

        def slow_pair(p, carry):
            j = 2 * p
            scores(j + 1, sb_ref, cb_ref)
            accumulate(j, sa_ref, ca_ref, False)
            scores(j + 2, sa_ref, ca_ref)
            accumulate(j + 1, sb_ref, cb_ref, False)
            return carry

        lax.fori_loop(0, i // 2, slow_pair, 0)

        @pl.when(i % 2 == 1)
        def _():
            scores(i, sb_ref, cb_ref)
            accumulate(i - 1, sa_ref, ca_ref, False)
            accumulate(i, sb_ref, cb_ref, True)

        @pl.when(i % 2 == 0)
        def _():
            accumulate(i, sa_ref, ca_ref, True)

    lf = lam_ref[...]
    lam = (jnp.exp(jnp.sum(lf[0:1] * lf[1:2], axis=1, keepdims=True))
           - jnp.exp(jnp.sum(lf[2:3] * lf[3:4], axis=1, keepdims=True)) + lambda_init)
    o = (acc_ref[0] / l_ref[0] - lam * (acc_ref[1] / l_ref[1])).T
    ms = jnp.mean(o * o, axis=-1, keepdims=True)
    o = o * lax.rsqrt(ms + EPS) * g_ref[...] * (1.0 - lambda_init)
    o_ref[...] = o.astype(o_ref.dtype)


def _diff_attention(proj, lambdas, norm_g, lambda_init, blk):
    t = proj.shape[0]
    qo, ko, vo = OFF_DQ // LANES, OFF_DK // LANES, OFF_DV // LANES
    return pl.pallas_call(
        functools.partial(_diff_attn_kernel, blk=blk, lambda_init=lambda_init),
        out_shape=jax.ShapeDtypeStruct((t, DIFF_WIDTH), BF16),
        grid=(DIFF_HEADS, t // blk),
        in_specs=[
            pl.BlockSpec((4, DIFF_HEAD_DIM), lambda h, i: (0, 0)),
            pl.BlockSpec((blk, LANES), lambda h, i: (i, qo + h)),
            pl.BlockSpec((t, LANES), lambda h, i: (0, ko + h)),
            pl.BlockSpec((t, LANES), lambda h, i: (0, vo + h)),
            pl.BlockSpec((1, LANES), lambda h, i: (0, 0)),
        ],
        out_specs=pl.BlockSpec((blk, LANES), lambda h, i: (i, h)),
        scratch_shapes=[
            pltpu.VMEM((t // blk, LANES, blk), BF16),
            pltpu.VMEM((2, LANES, blk), BF16),
            pltpu.VMEM((2, blk, blk), F32),
            pltpu.VMEM((2, blk, blk), F32),
            pltpu.VMEM((2, blk, blk), BF16),
            pltpu.VMEM((2, blk, blk), BF16),
            pltpu.VMEM((2, 1, blk), F32),
            pltpu.VMEM((2, 1, blk), F32),
            pltpu.VMEM((2, 1, blk), F32),
            pltpu.VMEM((2, 1, blk), F32),
            pltpu.VMEM((2, 1, blk), F32),
            pltpu.VMEM((2, 1, blk), F32),
            pltpu.VMEM((2, blk, blk), BF16),
            pltpu.VMEM((2, 1, blk), F32),
            pltpu.VMEM((2, 1, blk), F32),
            pltpu.VMEM((2, 1, blk), F32),
            pltpu.VMEM((2, 1, blk), F32),
            pltpu.VMEM((2, 1, blk), F32),
            pltpu.VMEM((2, LANES, blk), F32),
        ],
        compiler_params=_cparams(("arbitrary", "arbitrary")),
        name="diff_attention",
    )(lambdas, proj, proj, proj, norm_g)


def _gla_kernel(gv_ref, gr_ref, gq_ref, gk_ref, code_ref, w2_ref, gb_ref, ng_ref, o_ref, st_ref, g_ref, b_ref,
                kf_ref):
    tile = gq_ref.shape[0]
    c_len = GLA_CHUNK
    n_chunks = tile // c_len

    @pl.when(pl.program_id(0) == 0)
    def _():
        st_ref[...] = jnp.zeros(st_ref.shape, F32)

    z = jnp.dot(code_ref[...], w2_ref[...], preferred_element_type=F32) + gb_ref[...]
    g_ref[...] = (jnp.minimum(z, 0.0) - jnp.log1p(jnp.exp(-jnp.abs(z)))) * (1.0 / GLA_TAU)

    row = lax.broadcasted_iota(jnp.int32, (c_len, c_len), 0)
    col = lax.broadcasted_iota(jnp.int32, (c_len, c_len), 1)
    causal = col <= row
    tri = jnp.where(causal, 1.0, 0.0).astype(BF16)
    ng = ng_ref[...]

    def chunk(c, direct):
        r0 = c * c_len if isinstance(c, int) else pl.multiple_of(c * c_len, c_len)
        g = g_ref[pl.ds(r0, c_len), :]
        g_hi = g.astype(BF16)
        g_lo = (g - g_hi.astype(F32)).astype(BF16)
        b = (jnp.dot(tri, g_hi, preferred_element_type=F32)
             + jnp.dot(tri, g_lo, preferred_element_type=F32))
        b_last = b[c_len - 1:c_len, :]
        q = gq_ref[pl.ds(r0, c_len), :].astype(F32) * (GLA_DK ** -0.5)
        k = gk_ref[pl.ds(r0, c_len), :].astype(F32)
        qd = (q * jnp.exp(b)).astype(BF16)
        kl = (k * jnp.exp(b_last - b)).astype(BF16)
        e_last = jnp.exp(b_last)
        if direct:
            b_ref[...] = b
            kf_ref[...] = k
        else:
            kd = (k * jnp.exp(-b)).astype(BF16)
        for h in range(GLA_HEADS):
            ks = slice(h * GLA_DK_PAD, (h + 1) * GLA_DK_PAD)
            vs = slice(h * GLA_DV_PAD, (h + 1) * GLA_DV_PAD)
            vh = gv_ref[pl.ds(r0, c_len), vs]
            st = st_ref[h]
            if direct:
                qh, bh = q[:, ks], b[:, ks]

                def columns(grp, a):
                    g0 = pl.multiple_of(grp * 8, 8)
                    b8 = b_ref[pl.ds(g0, 8), ks]
                    k8 = kf_ref[pl.ds(g0, 8), ks]
                    for u in range(8):
                        w = qh * k8[u:u + 1] * jnp.exp(jnp.minimum(bh - b8[u:u + 1], 0.0))
                        a = jnp.where(col == g0 + u, jnp.sum(w, axis=1, keepdims=True), a)
                    return a

                a = lax.fori_loop(0, c_len // 8, columns, jnp.zeros((c_len, c_len), F32))
            else:
                a = lax.dot_general(qd[:, ks], kd[:, ks], (((1,), (1,)), ((), ())), preferred_element_type=F32)
            a = jnp.where(causal, a, 0.0).astype(BF16)
            o = jnp.dot(a, vh, preferred_element_type=F32)
            o = o + lax.dot_general(qd[:, ks], st.astype(BF16), (((1,), (1,)), ((), ())),
                                    preferred_element_type=F32)
            st_ref[h] = st * e_last[:, ks] + lax.dot_general(
                vh, kl[:, ks], (((0,), (0,)), ((), ())), preferred_element_type=F32)
            ms = jnp.sum(o * o, axis=-1, keepdims=True) * (1.0 / GLA_DV)
            on = o * lax.rsqrt(ms + EPS) * ng
            r = gr_ref[pl.ds(r0, c_len), vs].astype(F32)
            o_ref[pl.ds(r0, c_len), vs] = (on * (r / (1.0 + jnp.exp(-r)))).astype(o_ref.dtype)

    lowest = jnp.zeros((1, g_ref.shape[1]), F32)
    for c in range(n_chunks):
        lowest = jnp.minimum(lowest, jnp.sum(g_ref[c * c_len:(c + 1) * c_len, :], axis=0, keepdims=True))
    steep = jnp.min(lowest) < -GLA_SAFE_LOG_DECAY

    @pl.when(jnp.logical_not(steep))
    def _():
        for c in range(n_chunks):
            chunk(c, direct=False)

    @pl.when(steep)
    def _():
        def body(c, carry):
            chunk(c, direct=True)
            return carry
        lax.fori_loop(0, n_chunks, body, 0)


def _gla_mixer(proj, w2p, gbp, ngp, tile):
    t = proj.shape[0]
    kw = GLA_HEADS * GLA_DK_PAD
    return pl.pallas_call(
        _gla_kernel,
        out_shape=jax.ShapeDtypeStruct((t, MIX_C_WIDTH), BF16),
        grid=(t // tile,),
        in_specs=[
            pl.BlockSpec((tile, MIX_C_WIDTH), lambda i: (i, OFF_GV // MIX_C_WIDTH)),
            pl.BlockSpec((tile, MIX_C_WIDTH), lambda i: (i, OFF_GR // MIX_C_WIDTH)),
            pl.BlockSpec((tile, kw), lambda i: (i, OFF_GQ // kw)),
            pl.BlockSpec((tile, kw), lambda i: (i, OFF_GK // kw)),
            pl.BlockSpec((tile, LANES), lambda i: (i, OFF_CODE // LANES)),
            pl.BlockSpec((LANES, kw), lambda i: (0, 0)),
            pl.BlockSpec((1, kw), lambda i: (0, 0)),
            pl.BlockSpec((1, GLA_DV_PAD), lambda i: (0, 0)),
        ],
        out_specs=pl.BlockSpec((tile, MIX_C_WIDTH), lambda i: (i, 0)),
        scratch_shapes=[
            pltpu.VMEM((GLA_HEADS, GLA_DV_PAD, GLA_DK_PAD), F32),
            pltpu.VMEM((tile, kw), F32),
            pltpu.VMEM((GLA_CHUNK, kw), F32),
            pltpu.VMEM((GLA_CHUNK, kw), F32),
        ],
        compiler_params=_cparams(("arbitrary",)),
        name="gla_mixer",
    )(proj, proj, proj, proj, proj, w2p, gbp, ngp)


def _final_norm_kernel(x_ref, g_ref, o_ref):
    x = x_ref[...]
    ms = jnp.mean(x * x, axis=-1, keepdims=True)
    o_ref[...] = x * lax.rsqrt(ms + EPS) * g_ref[...]


def _final_norm(x, g, tm):
    t, d = x.shape
    return pl.pallas_call(
        _final_norm_kernel,
        out_shape=jax.ShapeDtypeStruct((t, d), F32),
        grid=(t // tm,),
        in_specs=[pl.BlockSpec((tm, d), lambda i: (i, 0)), pl.BlockSpec((1, d), lambda i: (0, 0))],
        out_specs=pl.BlockSpec((tm, d), lambda i: (i, 0)),
        compiler_params=_cparams(("parallel",)),
        name="final_norm",
    )(x, g)


def _pad_heads(w, heads, d, d_pad):
    lead = w.shape[:-1]
    w = w.reshape(lead + (heads, d))
    w = jnp.pad(w, [(0, 0)] * len(lead) + [(0, 0), (0, d_pad - d)])
    return w.reshape(lead + (heads * d_pad,))


def _head_pieces(w, axis, off, heads, d, d_pad):
    pieces = []
    for h in range(heads):
        sl = lax.slice_in_dim(w, off + h * d, off + (h + 1) * d, axis=axis)
        pieces.append(sl)
        zshape = list(w.shape)
        zshape[axis] = d_pad - d
        pieces.append(jnp.zeros(zshape, w.dtype))
    return pieces


def _layout_w_in(w_in):
    splits = [GMLP_WIDTH, GMLP_WIDTH, DIFF_WIDTH, DIFF_WIDTH, DIFF_WIDTH,
              GLA_KEY_WIDTH, GLA_KEY_WIDTH, GLA_WIDTH, GLA_WIDTH, GLA_RANK]
    idx = [0]
    for s in splits:
        idx.append(idx[-1] + s)
    o_u, o_v, o_dq, o_dk, o_dv, o_gq, o_gk, o_gv, o_gr, o_code, end = idx
    ax = w_in.ndim - 1
    pieces = (_head_pieces(w_in, ax, o_gv, GLA_HEADS, GLA_DV, GLA_DV_PAD)
              + _head_pieces(w_in, ax, o_gr, GLA_HEADS, GLA_DV, GLA_DV_PAD)
              + _head_pieces(w_in, ax, o_gq, GLA_HEADS, GLA_DK, GLA_DK_PAD)
              + _head_pieces(w_in, ax, o_gk, GLA_HEADS, GLA_DK, GLA_DK_PAD)
              + [lax.slice_in_dim(w_in, o_u, o_gq, axis=ax),
                 lax.slice_in_dim(w_in, o_code, end, axis=ax),
                 jnp.zeros(w_in.shape[:-1] + (PROJ_WIDTH - OFF_CODE - GLA_RANK,), w_in.dtype)])
    return jnp.concatenate(pieces, axis=ax).astype(BF16)


def _layout_w_out(w_out):
    wa = w_out[:, :GMLP_WIDTH, :].astype(BF16)
    wb = w_out[:, GMLP_WIDTH:GMLP_WIDTH + DIFF_WIDTH, :].astype(BF16)
    pieces = _head_pieces(w_out, 1, GMLP_WIDTH + DIFF_WIDTH, GLA_HEADS, GLA_DV, GLA_DV_PAD)
    return wa, wb, jnp.concatenate(pieces, axis=1).astype(BF16)


def kernel(x, norm1_g, w_in, gmlp_ln_g, spatial_w, spatial_b, diff_lambdas, diff_norm_g, gla_gate_w2,
           gla_gate_b, gla_norm_g, w_out, norm2_g, w_ffn_in, w_ffn_out, final_g):
    b, t, d = x.shape
    assert b == 1 and d == D_MODEL
    xs = x.reshape(t, d)

    w2p = jnp.pad(_pad_heads(gla_gate_w2, GLA_HEADS, GLA_DK, GLA_DK_PAD),
                  [(0, 0), (0, LANES - GLA_RANK), (0, 0)]).astype(BF16)
    gbp = _pad_heads(gla_gate_b, GLA_HEADS, GLA_DK, GLA_DK_PAD)[:, None, :]
    ngp = jnp.pad(gla_norm_g, [(0, 0), (0, GLA_DV_PAD - GLA_DV)])[:, None, :]
    sb_t = jnp.swapaxes(spatial_b, 1, 2)

    col = jnp.arange(PROJ_WIDTH)
    q_scale = (DIFF_HEAD_DIM ** -0.5) * math.log2(math.e)
    col_scale = jnp.where((col >= OFF_DQ) & (col < OFF_DK), q_scale, 1.0).astype(F32)[None, :]

    wa, wb, wc = _layout_w_out(w_out)
    wf_in = w_ffn_in.astype(BF16)
    wf_out = w_ffn_out.astype(BF16)

    for l in range(DEPTH):
        lambda_init = 0.8 - 0.6 * math.exp(-0.3 * l)
        w1 = _layout_w_in(w_in[l])
        proj = _norm_matmul(xs, norm1_g[l][None, :], w1, col_scale, min(512, t), PROJ_WIDTH // 2)
        a_out = _gmlp_mixer(proj, gmlp_ln_g[l][None, :], spatial_w[l], sb_t[l], min(512, t))
        b_out = _diff_attention(proj, diff_lambdas[l], diff_norm_g[l][None, :], lambda_init, min(512, t))
        c_out = _gla_mixer(proj, w2p[l], gbp[l], ngp[l], min(256, t))
        xs, h2 = _matmul_residual_norm([a_out, b_out, c_out], [wa, wb, wc], l, xs, norm2_g[l][None, :],
                                       min(512, t), "out_proj")
        hid = _swiglu(h2, wf_in, l, min(2048, t), 512)
        xs = _matmul_residual([hid], [wf_out], l, xs, min(1024, t), 512, "ffn_out")
    out = _final_norm(xs, final_g[None, :], min(512, t))
    return out.reshape(b, t, d)
```

```python
import functools
import math

import jax
import jax.numpy as jnp
from jax import lax
from jax.experimental import pallas as pl
from jax.experimental.pallas import tpu as pltpu

F32 = jnp.float32
BF16 = jnp.bfloat16

D_MODEL = 2048
DEPTH = 4
EPS = 1e-6
LANES = 128

GMLP_WIDTH = 512
GMLP_GROUPS = 4
GMLP_CHUNK = 128
DIFF_WIDTH = 768
DIFF_HEADS = 6
DIFF_HEAD_DIM = 64
GLA_HEADS = 4
GLA_DK = 96
GLA_DV = 192
GLA_DK_PAD = 128
GLA_DV_PAD = 256
GLA_KEY_WIDTH = GLA_HEADS * GLA_DK
GLA_WIDTH = GLA_HEADS * GLA_DV
GLA_RANK = 16
GLA_TAU = 16.0
GLA_CHUNK = 64
GLA_SAFE_LOG_DECAY = 80.0
FFN_HIDDEN = 5632

OFF_GV = 0
OFF_GR = OFF_GV + GLA_HEADS * GLA_DV_PAD
OFF_GQ = OFF_GR + GLA_HEADS * GLA_DV_PAD
OFF_GK = OFF_GQ + GLA_HEADS * GLA_DK_PAD
OFF_U = OFF_GK + GLA_HEADS * GLA_DK_PAD
OFF_V = OFF_U + GMLP_WIDTH
OFF_DQ = OFF_V + GMLP_WIDTH
OFF_DK = OFF_DQ + DIFF_WIDTH
OFF_DV = OFF_DK + DIFF_WIDTH
OFF_CODE = OFF_DV + DIFF_WIDTH
PROJ_TN = 512
PROJ_WIDTH = -(-(OFF_CODE + LANES) // PROJ_TN) * PROJ_TN
MIX_C_WIDTH = GLA_HEADS * GLA_DV_PAD

VMEM_LIMIT = 56 * 1024 * 1024


def _cparams(sem):
    return pltpu.CompilerParams(dimension_semantics=sem, vmem_limit_bytes=VMEM_LIMIT)


def _rmsnorm_rows(x_ref, g_ref, h_ref, rows):
    tm = x_ref.shape[0]
    g = g_ref[...]
    for r in range(0, tm, rows):
        x = x_ref[r:r + rows, :]
        ms = jnp.mean(x * x, axis=-1, keepdims=True)
        h_ref[r:r + rows, :] = (x * lax.rsqrt(ms + EPS) * g).astype(BF16)


def _norm_mm_kernel(x_ref, g_ref, w_ref, cs_ref, o_ref, h_ref):
    @pl.when(pl.program_id(1) == 0)
    def _():
        _rmsnorm_rows(x_ref, g_ref, h_ref, 256)

    acc = jnp.dot(h_ref[...], w_ref[...], preferred_element_type=F32)
    o_ref[...] = (acc * cs_ref[...]).astype(o_ref.dtype)


def _swiglu_kernel(h_ref, wg_ref, wu_ref, o_ref):
    h = h_ref[...]
    gate = jnp.dot(h, wg_ref[...], preferred_element_type=F32)
    up = jnp.dot(h, wu_ref[...], preferred_element_type=F32)
    o_ref[...] = (gate / (1.0 + jnp.exp(-gate)) * up).astype(o_ref.dtype)


def _norm_matmul(x, g, w, col_scale, tm, tn):
    t, d = x.shape
    n = w.shape[1]
    return pl.pallas_call(
        _norm_mm_kernel,
        out_shape=jax.ShapeDtypeStruct((t, n), BF16),
        grid=(t // tm, n // tn),
        in_specs=[
            pl.BlockSpec((tm, d), lambda i, j: (i, 0)),
            pl.BlockSpec((1, d), lambda i, j: (0, 0)),
            pl.BlockSpec((d, tn), lambda i, j: (0, j)),
            pl.BlockSpec((1, tn), lambda i, j: (0, j)),
        ],
        out_specs=pl.BlockSpec((tm, tn), lambda i, j: (i, j)),
        scratch_shapes=[pltpu.VMEM((tm, d), BF16)],
        compiler_params=_cparams(("parallel", "arbitrary")),
        name="norm_in_proj",
    )(x, g, w, col_scale)


def _swiglu(h, w, layer, tm, tn):
    t, d = h.shape
    hidden = w.shape[2] // 2
    nj = hidden // tn
    return pl.pallas_call(
        _swiglu_kernel,
        out_shape=jax.ShapeDtypeStruct((t, hidden), BF16),
        grid=(t // tm, nj),
        in_specs=[
            pl.BlockSpec((tm, d), lambda i, j: (i, 0)),
            pl.BlockSpec((None, d, tn), lambda i, j: (layer, 0, j)),
            pl.BlockSpec((None, d, tn), lambda i, j: (layer, 0, j + nj)),
        ],
        out_specs=pl.BlockSpec((tm, tn), lambda i, j: (i, j)),
        compiler_params=_cparams(("parallel", "arbitrary")),
        name="ffn_in",
    )(h, w, w)


def _mm_res_kernel(*refs, n_a):
    a_refs, w_refs = refs[:n_a], refs[n_a:2 * n_a]
    r_ref, o_ref = refs[2 * n_a], refs[2 * n_a + 1]
    acc = r_ref[...]
    for a_ref, w_ref in zip(a_refs, w_refs):
        acc = acc + jnp.dot(a_ref[...], w_ref[...], preferred_element_type=F32)
    o_ref[...] = acc


def _mm_res_norm_kernel(*refs, n_a):
    a_refs, w_refs = refs[:n_a], refs[n_a:2 * n_a]
    r_ref, g_ref, o_ref, h_ref = refs[2 * n_a:2 * n_a + 4]
    acc = r_ref[...]
    for a_ref, w_ref in zip(a_refs, w_refs):
        acc = acc + jnp.dot(a_ref[...], w_ref[...], preferred_element_type=F32)
    o_ref[...] = acc
    ms = jnp.mean(acc * acc, axis=-1, keepdims=True)
    h_ref[...] = (acc * lax.rsqrt(ms + EPS) * g_ref[...]).astype(h_ref.dtype)


def _matmul_residual_norm(a_list, w_list, layer, res, g, tm, name):
    t, n = res.shape
    n_a = len(a_list)
    in_specs = [pl.BlockSpec((tm, a.shape[1]), lambda i: (i, 0)) for a in a_list]
    in_specs += [pl.BlockSpec((None, w.shape[1], n), lambda i: (layer, 0, 0)) for w in w_list]
    in_specs += [pl.BlockSpec((tm, n), lambda i: (i, 0)), pl.BlockSpec((1, n), lambda i: (0, 0))]
    return pl.pallas_call(
        functools.partial(_mm_res_norm_kernel, n_a=n_a),
        out_shape=(jax.ShapeDtypeStruct((t, n), F32), jax.ShapeDtypeStruct((t, n), BF16)),
        grid=(t // tm,),
        in_specs=in_specs,
        out_specs=(pl.BlockSpec((tm, n), lambda i: (i, 0)), pl.BlockSpec((tm, n), lambda i: (i, 0))),
        compiler_params=_cparams(("parallel",)),
        name=name,
    )(*a_list, *w_list, res, g)


def _matmul_residual(a_list, w_list, layer, res, tm, tn, name):
    t, n = res.shape
    n_a = len(a_list)
    in_specs = [pl.BlockSpec((tm, a.shape[1]), lambda i, j: (i, 0)) for a in a_list]
    in_specs += [pl.BlockSpec((None, w.shape[1], tn), lambda i, j: (layer, 0, j)) for w in w_list]
    in_specs += [pl.BlockSpec((tm, tn), lambda i, j: (i, j))]
    return pl.pallas_call(
        functools.partial(_mm_res_kernel, n_a=n_a),
        out_shape=jax.ShapeDtypeStruct((t, n), F32),
        grid=(t // tm, n // tn),
        in_specs=in_specs,
        out_specs=pl.BlockSpec((tm, tn), lambda i, j: (i, j)),
        compiler_params=_cparams(("parallel", "arbitrary")),
        name=name,
    )(*a_list, *w_list, res)


def _gelu_tanh(x):
    return 0.5 * x * (1.0 + jnp.tanh(math.sqrt(2.0 / math.pi) * (x + 0.044715 * (x * x * x))))


def _gmlp_kernel(u_ref, v_ref, lng_ref, w_ref, b_ref, o_ref):
    tile = u_ref.shape[0]
    gd = GMLP_WIDTH // GMLP_GROUPS
    v = _gelu_tanh(v_ref[...].astype(F32))
    mu = jnp.mean(v, axis=-1, keepdims=True)
    vc = v - mu
    var = jnp.mean(vc * vc, axis=-1, keepdims=True)
    vn = (vc * lax.rsqrt(var + EPS) * lng_ref[...]).astype(BF16)
    row = lax.broadcasted_iota(jnp.int32, (GMLP_CHUNK, GMLP_CHUNK), 0)
    col = lax.broadcasted_iota(jnp.int32, (GMLP_CHUNK, GMLP_CHUNK), 1)
    causal = col <= row
    bias = b_ref[...]
    for g in range(GMLP_GROUPS):
        wg = jnp.where(causal, w_ref[g], 0.0).astype(BF16)
        bg = bias[:, g:g + 1]
        for c in range(tile // GMLP_CHUNK):
            rs = slice(c * GMLP_CHUNK, (c + 1) * GMLP_CHUNK)
            cs = slice(g * gd, (g + 1) * gd)
            mixed = jnp.dot(wg, vn[rs, cs], preferred_element_type=F32) + bg
            u = _gelu_tanh(u_ref[rs, cs].astype(F32))
            o_ref[rs, cs] = (u * mixed).astype(o_ref.dtype)


def _gmlp_mixer(proj, ln_g, spatial_w, spatial_b_t, tile):
    t = proj.shape[0]
    return pl.pallas_call(
        _gmlp_kernel,
        out_shape=jax.ShapeDtypeStruct((t, GMLP_WIDTH), BF16),
        grid=(t // tile,),
        in_specs=[
            pl.BlockSpec((tile, GMLP_WIDTH), lambda i: (i, OFF_U // GMLP_WIDTH)),
            pl.BlockSpec((tile, GMLP_WIDTH), lambda i: (i, OFF_V // GMLP_WIDTH)),
            pl.BlockSpec((1, GMLP_WIDTH), lambda i: (0, 0)),
            pl.BlockSpec((GMLP_GROUPS, GMLP_CHUNK, GMLP_CHUNK), lambda i: (0, 0, 0)),
            pl.BlockSpec((GMLP_CHUNK, GMLP_GROUPS), lambda i: (0, 0)),
        ],
        out_specs=pl.BlockSpec((tile, GMLP_WIDTH), lambda i: (i, 0)),
        compiler_params=_cparams(("parallel",)),
        name="gmlp_mixer",
    )(proj, proj, ln_g, spatial_w, spatial_b_t)


def _diff_attn_kernel(lam_ref, q_ref, k_ref, v_ref, g_ref, o_ref, vt_ref, qt_ref, sa_ref, sb_ref, pa_ref, pb_ref,
                      ca_ref, cb_ref, sta_ref, stb_ref, psa_ref, psb_ref, pc_ref, cc_ref, stc_ref, psc_ref,
                      m_ref, l_ref, acc_ref, *, blk, lambda_init):
    i = pl.program_id(1)
    n_blk = vt_ref.shape[0]

    @pl.when(i == 0)
    def _():
        def transpose_v(c, carry):
            r0 = pl.multiple_of(c * blk, blk)
            vt_ref[c] = v_ref[pl.ds(r0, blk), :].astype(F32).T.astype(BF16)
            return carry
        lax.fori_loop(0, n_blk, transpose_v, 0)

    qt = q_ref[...].astype(F32).T
    sub = lax.broadcasted_iota(jnp.int32, (LANES, blk), 0)
    qt_ref[0] = jnp.where(sub < DIFF_HEAD_DIM, qt, 0.0).astype(BF16)
    qt_ref[1] = jnp.where(sub >= DIFF_HEAD_DIM, qt, 0.0).astype(BF16)

    def reset_state():
        m_ref[...] = jnp.full(m_ref.shape, -jnp.inf, F32)
        l_ref[...] = jnp.zeros(l_ref.shape, F32)
        acc_ref[...] = jnp.zeros(acc_ref.shape, F32)

    def k_block(j):
        return k_ref[pl.ds(pl.multiple_of(j * blk, blk), blk), :]

    def causal_mask(s):
        row = lax.broadcasted_iota(jnp.int32, (blk, blk), 0)
        col = lax.broadcasted_iota(jnp.int32, (blk, blk), 1)
        return jnp.where(row <= col, s, -jnp.inf)

    def probs(j, cur, prev, first=False, masked=False):
        p_ref, st_ref, lse_ref, ps_ref = cur
        kb = k_block(j)
        for h in range(2):
            s = jnp.dot(kb, qt_ref[h], preferred_element_type=F32)
            if masked:
                s = causal_mask(s)
            if first:
                stab = jnp.max(s, axis=0, keepdims=True)
            else:
                stab = jnp.maximum(prev[1][h], prev[2][h])
            p = jnp.exp2(s - stab)
            ps = jnp.sum(p, axis=0, keepdims=True)
            ps_ref[h] = ps
            p_ref[h] = p.astype(BF16)
            lse_ref[h] = stab + jnp.log2(ps)
            st_ref[h] = stab

    def fold(j, cur):
        p_ref, st_ref, _, ps_ref = cur
        vtb = vt_ref[j]
        for h in range(2):
            stab = st_ref[h]
            alpha = jnp.exp2(m_ref[h] - stab)
            l_ref[h] = alpha * l_ref[h] + ps_ref[h]
            acc_ref[h] = alpha * acc_ref[h] + jnp.dot(vtb, p_ref[h], preferred_element_type=F32)
            m_ref[h] = stab

    buf_a = (pa_ref, sta_ref, ca_ref, psa_ref)
    buf_b = (pb_ref, stb_ref, cb_ref, psb_ref)
    buf_c = (pc_ref, stc_ref, cc_ref, psc_ref)

    def fast_two_blocks(j):
        probs(j + 1, buf_b, buf_a)
        fold(j, buf_a)
        probs(j + 2, buf_a, buf_b)
        fold(j + 1, buf_b)

    reset_state()
    n_pairs = jnp.maximum(i - 1, 0) // 2

    @pl.when(i > 0)
    def _():
        probs(0, buf_a, None, first=True)

    def fast_oct(p, carry):
        for u in range(4):
            fast_two_blocks(8 * p + 2 * u)
        return carry

    def fast_pair(p, carry):
        fast_two_blocks(8 * (n_pairs // 4) + 2 * p)
        return carry

    lax.fori_loop(0, n_pairs // 4, fast_oct, 0)
    lax.fori_loop(0, n_pairs % 4, fast_pair, 0)
    j0 = 2 * n_pairs

    @pl.when(i == 0)
    def _():
        probs(0, buf_a, None, first=True, masked=True)
        fold(0, buf_a)

    @pl.when(i % 2 == 1)
    def _():
        probs(i, buf_b, buf_a, masked=True)
        fold(j0, buf_a)
        fold(i, buf_b)

    @pl.when((i % 2 == 0) & (i > 0))
    def _():
        probs(j0 + 1, buf_b, buf_a)
        fold(j0, buf_a)
        probs(i, buf_c, buf_b, masked=True)
        fold(j0 + 1, buf_b)
        fold(i, buf_c)

    def scores(j, s_ref, c_ref):
        kb = k_block(j)
        for h in range(2):
            s = jnp.dot(kb, qt_ref[h], preferred_element_type=F32)
            s_ref[h] = s
            c_ref[h] = jnp.max(s, axis=0, keepdims=True)

    def accumulate(j, s_ref, c_ref, masked):
        vtb = vt_ref[j]
        for h in range(2):
            s = s_ref[h]
            if masked:
                s = causal_mask(s)
                c = jnp.max(s, axis=0, keepdims=True)
            else:
                c = c_ref[h]
            m_prev = m_ref[h]
            m_new = jnp.maximum(m_prev, c)
            alpha = jnp.exp2(m_prev - m_new)
            p = jnp.exp2(s - m_new)
            l_ref[h] = alpha * l_ref[h] + jnp.sum(p, axis=0, keepdims=True)
            acc_ref[h] = alpha * acc_ref[h] + jnp.dot(vtb, p.astype(BF16), preferred_element_type=F32)
            m_ref[h] = m_new

    total = jnp.sum(l_ref[...]) + jnp.sum(acc_ref[...])
    overflowed = jnp.logical_not(jnp.abs(total) < jnp.inf)

    @pl.when(overflowed)
    def _():
        reset_state()
        scores(0, sa_ref, ca_ref)

        def slow_pair(p, carry):
            j = 2 * p
            scores(j + 1, sb_ref, cb_ref)
            accumulate(j, sa_ref, ca_ref, False)
            scores(j + 2, sa_ref, ca_ref)
            accumulate(j + 1, sb_ref, cb_ref, False)
            return carry

        lax.fori_loop(0, i // 2, slow_pair, 0)

        @pl.when(i % 2 == 1)
        def _():
            scores(i, sb_ref, cb_ref)
            accumulate(i - 1, sa_ref, ca_ref, False)
            accumulate(i, sb_ref, cb_ref, True)

        @pl.when(i % 2 == 0)
        def _():
            accumulate(i, sa_ref, ca_ref, True)

    lf = lam_ref[...]
    lam = (jnp.exp(jnp.sum(lf[0:1] * lf[1:2], axis=1, keepdims=True))
           - jnp.exp(jnp.sum(lf[2:3] * lf[3:4], axis=1, keepdims=True)) + lambda_init)
    o = (acc_ref[0] / l_ref[0] - lam * (acc_ref[1] / l_ref[1])).T
    ms = jnp.mean(o * o, axis=-1, keepdims=True)
    o = o * lax.rsqrt(ms + EPS) * g_ref[...] * (1.0 - lambda_init)
    o_ref[...] = o.astype(o_ref.dtype)


def _diff_attention(proj, lambdas, norm_g, lambda_init, blk):
    t = proj.shape[0]
    qo, ko, vo = OFF_DQ // LANES, OFF_DK // LANES, OFF_DV // LANES
    stat = pltpu.VMEM((2, 1, blk), F32)
    prob_buffer = [pltpu.VMEM((2, blk, blk), BF16), stat, stat, stat]
    return pl.pallas_call(
        functools.partial(_diff_attn_kernel, blk=blk, lambda_init=lambda_init),
        out_shape=jax.ShapeDtypeStruct((t, DIFF_WIDTH), BF16),
        grid=(DIFF_HEADS, t // blk),
        in_specs=[
            pl.BlockSpec((4, DIFF_HEAD_DIM), lambda h, i: (0, 0)),
            pl.BlockSpec((blk, LANES), lambda h, i: (i, qo + h)),
            pl.BlockSpec((t, LANES), lambda h, i: (0, ko + h)),
            pl.BlockSpec((t, LANES), lambda h, i: (0, vo + h)),
            pl.BlockSpec((1, LANES), lambda h, i: (0, 0)),
        ],
        out_specs=pl.BlockSpec((blk, LANES), lambda h, i: (i, h)),
        scratch_shapes=[
            pltpu.VMEM((t // blk, LANES, blk), BF16),
            pltpu.VMEM((2, LANES, blk), BF16),
            pltpu.VMEM((2, blk, blk), F32),
            pltpu.VMEM((2, blk, blk), F32),
            pltpu.VMEM((2, blk, blk), BF16),
            pltpu.VMEM((2, blk, blk), BF16),
            stat, stat,
            stat, stat,
            stat, stat,
            *prob_buffer,
            stat,
            stat,
            pltpu.VMEM((2, LANES, blk), F32),
        ],
        compiler_params=_cparams(("arbitrary", "arbitrary")),
        name="diff_attention",
    )(lambdas, proj, proj, proj, norm_g)


def _gla_kernel(gv_ref, gr_ref, gq_ref, gk_ref, code_ref, w2_ref, gb_ref, ng_ref, o_ref, st_ref, g_ref, b_ref,
                kf_ref, st0_ref):
    tile = gq_ref.shape[0]
    c_len = GLA_CHUNK
    n_chunks = tile // c_len

    @pl.when(pl.program_id(0) == 0)
    def _():
        st_ref[...] = jnp.zeros(st_ref.shape, F32)

    z = jnp.dot(code_ref[...], w2_ref[...], preferred_element_type=F32) + gb_ref[...]
    g_ref[...] = (jnp.minimum(z, 0.0) - jnp.log1p(jnp.exp(-jnp.abs(z)))) * (1.0 / GLA_TAU)

    row = lax.broadcasted_iota(jnp.int32, (c_len, c_len), 0)
    col = lax.broadcasted_iota(jnp.int32, (c_len, c_len), 1)
    causal = col <= row
    tri = jnp.where(causal, 1.0, 0.0).astype(BF16)
    ng = ng_ref[...]

    def chunk(c, direct):
        r0 = c * c_len if isinstance(c, int) else pl.multiple_of(c * c_len, c_len)
        g = g_ref[pl.ds(r0, c_len), :]
        g_hi = g.astype(BF16)
        g_lo = (g - g_hi.astype(F32)).astype(BF16)
        b = (jnp.dot(tri, g_hi, preferred_element_type=F32)
             + jnp.dot(tri, g_lo, preferred_element_type=F32))
        b_last = b[c_len - 1:c_len, :]
        q = gq_ref[pl.ds(r0, c_len), :].astype(F32) * (GLA_DK ** -0.5)
        k = gk_ref[pl.ds(r0, c_len), :].astype(F32)
        qd = (q * jnp.exp(b)).astype(BF16)
        kl = (k * jnp.exp(b_last - b)).astype(BF16)
        e_last = jnp.exp(b_last)
        if direct:
            b_ref[...] = b
            kf_ref[...] = k
        else:
            kd = (k * jnp.exp(-b)).astype(BF16)
        for h in range(GLA_HEADS):
            ks = slice(h * GLA_DK_PAD, (h + 1) * GLA_DK_PAD)
            vs = slice(h * GLA_DV_PAD, (h + 1) * GLA_DV_PAD)
            vh = gv_ref[pl.ds(r0, c_len), vs]
            st = st_ref[h]
            if direct:
                qh, bh = q[:, ks], b[:, ks]

                def columns(grp, a):
                    g0 = pl.multiple_of(grp * 8, 8)
                    b8 = b_ref[pl.ds(g0, 8), ks]
                    k8 = kf_ref[pl.ds(g0, 8), ks]
                    for u in range(8):
                        w = qh * k8[u:u + 1] * jnp.exp(jnp.minimum(bh - b8[u:u + 1], 0.0))
                        a = jnp.where(col == g0 + u, jnp.sum(w, axis=1, keepdims=True), a)
                    return a

                a = lax.fori_loop(0, c_len // 8, columns, jnp.zeros((c_len, c_len), F32))
            else:
                a = lax.dot_general(qd[:, ks], kd[:, ks], (((1,), (1,)), ((), ())), preferred_element_type=F32)
            a = jnp.where(causal, a, 0.0).astype(BF16)
            o = jnp.dot(a, vh, preferred_element_type=F32)
            o = o + lax.dot_general(qd[:, ks], st.astype(BF16), (((1,), (1,)), ((), ())),
                                    preferred_element_type=F32)
            st_ref[h] = st * e_last[:, ks] + lax.dot_general(
                vh, kl[:, ks], (((0,), (0,)), ((), ())), preferred_element_type=F32)
            ms = jnp.sum(o * o, axis=-1, keepdims=True) * (1.0 / GLA_DV)
            on = o * lax.rsqrt(ms + EPS) * ng
            r = gr_ref[pl.ds(r0, c_len), vs].astype(F32)
            o_ref[pl.ds(r0, c_len), vs] = (on * (r / (1.0 + jnp.exp(-r)))).astype(o_ref.dtype)

    st0_ref[...] = st_ref[...]
    for c in range(n_chunks):
        chunk(c, direct=False)

    lowest = jnp.zeros((1, g_ref.shape[1]), F32)
    for c in range(n_chunks):
        lowest = jnp.minimum(lowest, jnp.sum(g_ref[c * c_len:(c + 1) * c_len, :], axis=0, keepdims=True))

    @pl.when(jnp.min(lowest) < -GLA_SAFE_LOG_DECAY)
    def _():
        st_ref[...] = st0_ref[...]

        def body(c, carry):
            chunk(c, direct=True)
            return carry
        lax.fori_loop(0, n_chunks, body, 0)


def _gla_mixer(proj, w2p, gbp, ngp, tile):
    t = proj.shape[0]
    kw = GLA_HEADS * GLA_DK_PAD
    state = pltpu.VMEM((GLA_HEADS, GLA_DV_PAD, GLA_DK_PAD), F32)
    return pl.pallas_call(
        _gla_kernel,
        out_shape=jax.ShapeDtypeStruct((t, MIX_C_WIDTH), BF16),
        grid=(t // tile,),
        in_specs=[
            pl.BlockSpec((tile, MIX_C_WIDTH), lambda i: (i, OFF_GV // MIX_C_WIDTH)),
            pl.BlockSpec((tile, MIX_C_WIDTH), lambda i: (i, OFF_GR // MIX_C_WIDTH)),
            pl.BlockSpec((tile, kw), lambda i: (i, OFF_GQ // kw)),
            pl.BlockSpec((tile, kw), lambda i: (i, OFF_GK // kw)),
            pl.BlockSpec((tile, LANES), lambda i: (i, OFF_CODE // LANES)),
            pl.BlockSpec((LANES, kw), lambda i: (0, 0)),
            pl.BlockSpec((1, kw), lambda i: (0, 0)),
            pl.BlockSpec((1, GLA_DV_PAD), lambda i: (0, 0)),
        ],
        out_specs=pl.BlockSpec((tile, MIX_C_WIDTH), lambda i: (i, 0)),
        scratch_shapes=[
            state,
            pltpu.VMEM((tile, kw), F32),
            pltpu.VMEM((GLA_CHUNK, kw), F32),
            pltpu.VMEM((GLA_CHUNK, kw), F32),
            state,
        ],
        compiler_params=_cparams(("arbitrary",)),
        name="gla_mixer",
    )(proj, proj, proj, proj, proj, w2p, gbp, ngp)


def _final_norm_kernel(x_ref, g_ref, o_ref):
    x = x_ref[...]
    ms = jnp.mean(x * x, axis=-1, keepdims=True)
    o_ref[...] = x * lax.rsqrt(ms + EPS) * g_ref[...]


def _final_norm(x, g, tm):
    t, d = x.shape
    return pl.pallas_call(
        _final_norm_kernel,
        out_shape=jax.ShapeDtypeStruct((t, d), F32),
        grid=(t // tm,),
        in_specs=[pl.BlockSpec((tm, d), lambda i: (i, 0)), pl.BlockSpec((1, d), lambda i: (0, 0))],
        out_specs=pl.BlockSpec((tm, d), lambda i: (i, 0)),
        compiler_params=_cparams(("parallel",)),
        name="final_norm",
    )(x, g)


def _pad_heads(w, heads, d, d_pad):
    lead = w.shape[:-1]
    w = w.reshape(lead + (heads, d))
    w = jnp.pad(w, [(0, 0)] * len(lead) + [(0, 0), (0, d_pad - d)])
    return w.reshape(lead + (heads * d_pad,))


def _head_pieces(w, axis, off, heads, d, d_pad):
    pieces = []
    for h in range(heads):
        sl = lax.slice_in_dim(w, off + h * d, off + (h + 1) * d, axis=axis)
        pieces.append(sl)
        zshape = list(w.shape)
        zshape[axis] = d_pad - d
        pieces.append(jnp.zeros(zshape, w.dtype))
    return pieces


def _layout_w_in(w_in):
    splits = [GMLP_WIDTH, GMLP_WIDTH, DIFF_WIDTH, DIFF_WIDTH, DIFF_WIDTH,
              GLA_KEY_WIDTH, GLA_KEY_WIDTH, GLA_WIDTH, GLA_WIDTH, GLA_RANK]
    idx = [0]
    for s in splits:
        idx.append(idx[-1] + s)
    o_u, o_v, o_dq, o_dk, o_dv, o_gq, o_gk, o_gv, o_gr, o_code, end = idx
    ax = w_in.ndim - 1
    pieces = (_head_pieces(w_in, ax, o_gv, GLA_HEADS, GLA_DV, GLA_DV_PAD)
              + _head_pieces(w_in, ax, o_gr, GLA_HEADS, GLA_DV, GLA_DV_PAD)
              + _head_pieces(w_in, ax, o_gq, GLA_HEADS, GLA_DK, GLA_DK_PAD)
              + _head_pieces(w_in, ax, o_gk, GLA_HEADS, GLA_DK, GLA_DK_PAD)
              + [lax.slice_in_dim(w_in, o_u, o_gq, axis=ax),
                 lax.slice_in_dim(w_in, o_code, end, axis=ax),
                 jnp.zeros(w_in.shape[:-1] + (PROJ_WIDTH - OFF_CODE - GLA_RANK,), w_in.dtype)])
    return jnp.concatenate(pieces, axis=ax).astype(BF16)


def _layout_w_out(w_out):
    wa = w_out[:, :GMLP_WIDTH, :].astype(BF16)
    wb = w_out[:, GMLP_WIDTH:GMLP_WIDTH + DIFF_WIDTH, :].astype(BF16)
    pieces = _head_pieces(w_out, 1, GMLP_WIDTH + DIFF_WIDTH, GLA_HEADS, GLA_DV, GLA_DV_PAD)
    return wa, wb, jnp.concatenate(pieces, axis=1).astype(BF16)


def kernel(x, norm1_g, w_in, gmlp_ln_g, spatial_w, spatial_b, diff_lambdas, diff_norm_g, gla_gate_w2,
           gla_gate_b, gla_norm_g, w_out, norm2_g, w_ffn_in, w_ffn_out, final_g):
    b, t, d = x.shape
    assert b == 1 and d == D_MODEL
    xs = x.reshape(t, d)

    w2p = jnp.pad(_pad_heads(gla_gate_w2, GLA_HEADS, GLA_DK, GLA_DK_PAD),
                  [(0, 0), (0, LANES - GLA_RANK), (0, 0)]).astype(BF16)
    gbp = _pad_heads(gla_gate_b, GLA_HEADS, GLA_DK, GLA_DK_PAD)[:, None, :]
    ngp = jnp.pad(gla_norm_g, [(0, 0), (0, GLA_DV_PAD - GLA_DV)])[:, None, :]
    sb_t = jnp.swapaxes(spatial_b, 1, 2)

    col = jnp.arange(PROJ_WIDTH)
    q_scale = (DIFF_HEAD_DIM ** -0.5) * math.log2(math.e)
    col_scale = jnp.where((col >= OFF_DQ) & (col < OFF_DK), q_scale, 1.0).astype(F32)[None, :]

    wa, wb, wc = _layout_w_out(w_out)
    wf_in = w_ffn_in.astype(BF16)
    wf_out = w_ffn_out.astype(BF16)

    for l in range(DEPTH):
        lambda_init = 0.8 - 0.6 * math.exp(-0.3 * l)
        w1 = _layout_w_in(w_in[l])
        proj = _norm_matmul(xs, norm1_g[l][None, :], w1, col_scale, min(512, t), PROJ_WIDTH // 2)
        a_out = _gmlp_mixer(proj, gmlp_ln_g[l][None, :], spatial_w[l], sb_t[l], min(512, t))
        b_out = _diff_attention(proj, diff_lambdas[l], diff_norm_g[l][None, :], lambda_init, min(512, t))
        c_out = _gla_mixer(proj, w2p[l], gbp[l], ngp[l], min(256, t))
        xs, h2 = _matmul_residual_norm([a_out, b_out, c_out], [wa, wb, wc], l, xs, norm2_g[l][None, :],
                                       min(512, t), "out_proj")
        hid = _swiglu(h2, wf_in, l, min(2048, t), 512)
        xs = _matmul_residual([hid], [wf_out], l, xs, min(1024, t), 512, "ffn_out")
    out = _final_norm(xs, final_g[None, :], min(512, t))
    return out.reshape(b, t, d)
```

```python
import functools
import math

import jax
import jax.numpy as jnp
from jax import lax
from jax.experimental import pallas as pl
from jax.experimental.pallas import tpu as pltpu

F32 = jnp.float32
BF16 = jnp.bfloat16

D_MODEL = 2048
DEPTH = 4
EPS = 1e-6
LANES = 128

GMLP_WIDTH = 512
GMLP_GROUPS = 4
GMLP_CHUNK = 128
DIFF_WIDTH = 768
DIFF_HEADS = 6
DIFF_HEAD_DIM = 64
GLA_HEADS = 4
GLA_DK = 96
GLA_DV = 192
GLA_DK_PAD = 128
GLA_DV_PAD = 256
GLA_KEY_WIDTH = GLA_HEADS * GLA_DK
GLA_WIDTH = GLA_HEADS * GLA_DV
GLA_RANK = 16
GLA_TAU = 16.0
GLA_CHUNK = 64
GLA_SAFE_LOG_DECAY = 80.0
FFN_HIDDEN = 5632

OFF_GV = 0
OFF_GR = OFF_GV + GLA_HEADS * GLA_DV_PAD
OFF_GQ = OFF_GR + GLA_HEADS * GLA_DV_PAD
OFF_GK = OFF_GQ + GLA_HEADS * GLA_DK_PAD
OFF_U = OFF_GK + GLA_HEADS * GLA_DK_PAD
OFF_V = OFF_U + GMLP_WIDTH
OFF_DQ = OFF_V + GMLP_WIDTH
OFF_DK = OFF_DQ + DIFF_WIDTH
OFF_DV = OFF_DK + DIFF_WIDTH
OFF_CODE = OFF_DV + DIFF_WIDTH
PROJ_TN = 512
PROJ_WIDTH = -(-(OFF_CODE + LANES) // PROJ_TN) * PROJ_TN
MIX_C_WIDTH = GLA_HEADS * GLA_DV_PAD

VMEM_LIMIT = 56 * 1024 * 1024


def _cparams(sem):
    return pltpu.CompilerParams(dimension_semantics=sem, vmem_limit_bytes=VMEM_LIMIT)


def _rmsnorm_rows(x_ref, g_ref, h_ref, rows):
    tm = x_ref.shape[0]
    g = g_ref[...]
    for r in range(0, tm, rows):
        x = x_ref[r:r + rows, :]
        ms = jnp.mean(x * x, axis=-1, keepdims=True)
        h_ref[r:r + rows, :] = (x * lax.rsqrt(ms + EPS) * g).astype(BF16)


def _norm_mm_kernel(x_ref, g_ref, w_ref, cs_ref, o_ref, h_ref):
    @pl.when(pl.program_id(1) == 0)
    def _():
        _rmsnorm_rows(x_ref, g_ref, h_ref, 256)

    acc = jnp.dot(h_ref[...], w_ref[...], preferred_element_type=F32)
    o_ref[...] = (acc * cs_ref[...]).astype(o_ref.dtype)


def _swiglu_kernel(h_ref, wg_ref, wu_ref, o_ref):
    h = h_ref[...]
    gate = jnp.dot(h, wg_ref[...], preferred_element_type=F32)
    up = jnp.dot(h, wu_ref[...], preferred_element_type=F32)
    o_ref[...] = (gate / (1.0 + jnp.exp(-gate)) * up).astype(o_ref.dtype)


def _norm_matmul(x, g, w, col_scale, tm, tn):
    t, d = x.shape
    n = w.shape[1]
    return pl.pallas_call(
        _norm_mm_kernel,
        out_shape=jax.ShapeDtypeStruct((t, n), BF16),
        grid=(t // tm, n // tn),
        in_specs=[
            pl.BlockSpec((tm, d), lambda i, j: (i, 0)),
            pl.BlockSpec((1, d), lambda i, j: (0, 0)),
            pl.BlockSpec((d, tn), lambda i, j: (0, j)),
            pl.BlockSpec((1, tn), lambda i, j: (0, j)),
        ],
        out_specs=pl.BlockSpec((tm, tn), lambda i, j: (i, j)),
        scratch_shapes=[pltpu.VMEM((tm, d), BF16)],
        compiler_params=_cparams(("parallel", "arbitrary")),
        name="norm_in_proj",
    )(x, g, w, col_scale)


def _swiglu(h, w, layer, tm, tn):
    t, d = h.shape
    hidden = w.shape[2] // 2
    nj = hidden // tn
    return pl.pallas_call(
        _swiglu_kernel,
        out_shape=jax.ShapeDtypeStruct((t, hidden), BF16),
        grid=(t // tm, nj),
        in_specs=[
            pl.BlockSpec((tm, d), lambda i, j: (i, 0)),
            pl.BlockSpec((None, d, tn), lambda i, j: (layer, 0, j)),
            pl.BlockSpec((None, d, tn), lambda i, j: (layer, 0, j + nj)),
        ],
        out_specs=pl.BlockSpec((tm, tn), lambda i, j: (i, j)),
        compiler_params=_cparams(("parallel", "arbitrary")),
        name="ffn_in",
    )(h, w, w)


def _mm_res_kernel(*refs, n_a):
    a_refs, w_refs = refs[:n_a], refs[n_a:2 * n_a]
    r_ref, o_ref = refs[2 * n_a], refs[2 * n_a + 1]
    acc = r_ref[...]
    for a_ref, w_ref in zip(a_refs, w_refs):
        acc = acc + jnp.dot(a_ref[...], w_ref[...], preferred_element_type=F32)
    o_ref[...] = acc


def _mm_res_norm_kernel(*refs, n_a):
    a_refs, w_refs = refs[:n_a], refs[n_a:2 * n_a]
    r_ref, g_ref, o_ref, h_ref = refs[2 * n_a:2 * n_a + 4]
    acc = r_ref[...]
    for a_ref, w_ref in zip(a_refs, w_refs):
        acc = acc + jnp.dot(a_ref[...], w_ref[...], preferred_element_type=F32)
    o_ref[...] = acc
    ms = jnp.mean(acc * acc, axis=-1, keepdims=True)
    h_ref[...] = (acc * lax.rsqrt(ms + EPS) * g_ref[...]).astype(h_ref.dtype)


def _matmul_residual_norm(a_list, w_list, layer, res, g, tm, name):
    t, n = res.shape
    n_a = len(a_list)
    in_specs = [pl.BlockSpec((tm, a.shape[1]), lambda i: (i, 0)) for a in a_list]
    in_specs += [pl.BlockSpec((None, w.shape[1], n), lambda i: (layer, 0, 0)) for w in w_list]
    in_specs += [pl.BlockSpec((tm, n), lambda i: (i, 0)), pl.BlockSpec((1, n), lambda i: (0, 0))]
    return pl.pallas_call(
        functools.partial(_mm_res_norm_kernel, n_a=n_a),
        out_shape=(jax.ShapeDtypeStruct((t, n), F32), jax.ShapeDtypeStruct((t, n), BF16)),
        grid=(t // tm,),
        in_specs=in_specs,
        out_specs=(pl.BlockSpec((tm, n), lambda i: (i, 0)), pl.BlockSpec((tm, n), lambda i: (i, 0))),
        compiler_params=_cparams(("parallel",)),
        name=name,
    )(*a_list, *w_list, res, g)


def _matmul_residual(a_list, w_list, layer, res, tm, tn, name):
    t, n = res.shape
    n_a = len(a_list)
    in_specs = [pl.BlockSpec((tm, a.shape[1]), lambda i, j: (i, 0)) for a in a_list]
    in_specs += [pl.BlockSpec((None, w.shape[1], tn), lambda i, j: (layer, 0, j)) for w in w_list]
    in_specs += [pl.BlockSpec((tm, tn), lambda i, j: (i, j))]
    return pl.pallas_call(
        functools.partial(_mm_res_kernel, n_a=n_a),
        out_shape=jax.ShapeDtypeStruct((t, n), F32),
        grid=(t // tm, n // tn),
        in_specs=in_specs,
        out_specs=pl.BlockSpec((tm, tn), lambda i, j: (i, j)),
        compiler_params=_cparams(("parallel", "arbitrary")),
        name=name,
    )(*a_list, *w_list, res)


def _gelu_tanh(x):
    return 0.5 * x * (1.0 + jnp.tanh(math.sqrt(2.0 / math.pi) * (x + 0.044715 * (x * x * x))))


def _gmlp_kernel(u_ref, v_ref, lng_ref, w_ref, b_ref, o_ref):
    tile = u_ref.shape[0]
    gd = GMLP_WIDTH // GMLP_GROUPS
    v = _gelu_tanh(v_ref[...].astype(F32))
    mu = jnp.mean(v, axis=-1, keepdims=True)
    vc = v - mu
    var = jnp.mean(vc * vc, axis=-1, keepdims=True)
    vn = (vc * lax.rsqrt(var + EPS) * lng_ref[...]).astype(BF16)
    row = lax.broadcasted_iota(jnp.int32, (GMLP_CHUNK, GMLP_CHUNK), 0)
    col = lax.broadcasted_iota(jnp.int32, (GMLP_CHUNK, GMLP_CHUNK), 1)
    causal = col <= row
    bias = b_ref[...]
    for g in range(GMLP_GROUPS):
        wg = jnp.where(causal, w_ref[g], 0.0).astype(BF16)
        bg = bias[:, g:g + 1]
        for c in range(tile // GMLP_CHUNK):
            rs = slice(c * GMLP_CHUNK, (c + 1) * GMLP_CHUNK)
            cs = slice(g * gd, (g + 1) * gd)
            mixed = jnp.dot(wg, vn[rs, cs], preferred_element_type=F32) + bg
            u = _gelu_tanh(u_ref[rs, cs].astype(F32))
            o_ref[rs, cs] = (u * mixed).astype(o_ref.dtype)


def _gmlp_mixer(proj, ln_g, spatial_w, spatial_b_t, tile):
    t = proj.shape[0]
    return pl.pallas_call(
        _gmlp_kernel,
        out_shape=jax.ShapeDtypeStruct((t, GMLP_WIDTH), BF16),
        grid=(t // tile,),
        in_specs=[
            pl.BlockSpec((tile, GMLP_WIDTH), lambda i: (i, OFF_U // GMLP_WIDTH)),
            pl.BlockSpec((tile, GMLP_WIDTH), lambda i: (i, OFF_V // GMLP_WIDTH)),
            pl.BlockSpec((1, GMLP_WIDTH), lambda i: (0, 0)),
            pl.BlockSpec((GMLP_GROUPS, GMLP_CHUNK, GMLP_CHUNK), lambda i: (0, 0, 0)),
            pl.BlockSpec((GMLP_CHUNK, GMLP_GROUPS), lambda i: (0, 0)),
        ],
        out_specs=pl.BlockSpec((tile, GMLP_WIDTH), lambda i: (i, 0)),
        compiler_params=_cparams(("parallel",)),
        name="gmlp_mixer",
    )(proj, proj, ln_g, spatial_w, spatial_b_t)


def _diff_attn_kernel(lam_ref, q_ref, k_ref, v_ref, g_ref, o_ref, vt_ref, qt_ref, sa_ref, sb_ref, pa_ref, pb_ref,
                      ca_ref, cb_ref, sta_ref, stb_ref, psa_ref, psb_ref, pc_ref, cc_ref, stc_ref, psc_ref,
                      m_ref, l_ref, acc_ref, *, blk, lambda_init):
    i = pl.program_id(1)
    n_blk = vt_ref.shape[0]

    @pl.when(i == 0)
    def _():
        def transpose_v(c, carry):
            r0 = pl.multiple_of(c * blk, blk)
            vt_ref[c] = v_ref[pl.ds(r0, blk), :].astype(F32).T.astype(BF16)
            return carry
        lax.fori_loop(0, n_blk, transpose_v, 0)

    qt = q_ref[...].astype(F32).T
    sub = lax.broadcasted_iota(jnp.int32, (LANES, blk), 0)
    qt_ref[0] = jnp.where(sub < DIFF_HEAD_DIM, qt, 0.0).astype(BF16)
    qt_ref[1] = jnp.where(sub >= DIFF_HEAD_DIM, qt, 0.0).astype(BF16)

    def reset_state():
        m_ref[...] = jnp.full(m_ref.shape, -jnp.inf, F32)
        l_ref[...] = jnp.zeros(l_ref.shape, F32)
        acc_ref[...] = jnp.zeros(acc_ref.shape, F32)

    def k_block(j):
        return k_ref[pl.ds(pl.multiple_of(j * blk, blk), blk), :]

    def causal_mask(s):
        row = lax.broadcasted_iota(jnp.int32, (blk, blk), 0)
        col = lax.broadcasted_iota(jnp.int32, (blk, blk), 1)
        return jnp.where(row <= col, s, -jnp.inf)

    def probs(j, cur, prev, first=False, masked=False):
        p_ref, st_ref, lse_ref, ps_ref = cur
        kb = k_block(j)
        for h in range(2):
            s = jnp.dot(kb, qt_ref[h], preferred_element_type=F32)
            if masked:
                s = causal_mask(s)
            if first:
                stab = jnp.max(s, axis=0, keepdims=True)
            else:
                stab = jnp.maximum(prev[1][h], prev[2][h])
            p = jnp.exp2(s - stab)
            ps = jnp.sum(p, axis=0, keepdims=True)
            ps_ref[h] = ps
            p_ref[h] = p.astype(BF16)
            lse_ref[h] = stab + jnp.log2(ps)
            st_ref[h] = stab

    def fold(j, cur):
        p_ref, st_ref, _, ps_ref = cur
        vtb = vt_ref[j]
        for h in range(2):
            stab = st_ref[h]
            alpha = jnp.exp2(m_ref[h] - stab)
            l_ref[h] = alpha * l_ref[h] + ps_ref[h]
            acc_ref[h] = alpha * acc_ref[h] + jnp.dot(vtb, p_ref[h], preferred_element_type=F32)
            m_ref[h] = stab

    buf_a = (pa_ref, sta_ref, ca_ref, psa_ref)
    buf_b = (pb_ref, stb_ref, cb_ref, psb_ref)
    buf_c = (pc_ref, stc_ref, cc_ref, psc_ref)

    def fast_two_blocks(j):
        probs(j + 1, buf_b, buf_a)
        fold(j, buf_a)
        probs(j + 2, buf_a, buf_b)
        fold(j + 1, buf_b)

    reset_state()
    n_pairs = jnp.maximum(i - 1, 0) // 2

    @pl.when(i > 0)
    def _():
        probs(0, buf_a, None, first=True)

    def fast_oct(p, carry):
        for u in range(4):
            fast_two_blocks(8 * p + 2 * u)
        return carry

    def fast_pair(p, carry):
        fast_two_blocks(8 * (n_pairs // 4) + 2 * p)
        return carry

    lax.fori_loop(0, n_pairs // 4, fast_oct, 0)
    lax.fori_loop(0, n_pairs % 4, fast_pair, 0)
    j0 = 2 * n_pairs

    @pl.when(i == 0)
    def _():
        probs(0, buf_a, None, first=True, masked=True)
        fold(0, buf_a)

    @pl.when(i % 2 == 1)
    def _():
        probs(i, buf_b, buf_a, masked=True)
        fold(j0, buf_a)
        fold(i, buf_b)

    @pl.when((i % 2 == 0) & (i > 0))
    def _():
        probs(j0 + 1, buf_b, buf_a)
        fold(j0, buf_a)
        probs(i, buf_c, buf_b, masked=True)
        fold(j0 + 1, buf_b)
        fold(i, buf_c)

    def scores(j, s_ref, c_ref):
        kb = k_block(j)
        for h in range(2):
            s = jnp.dot(kb, qt_ref[h], preferred_element_type=F32)
            s_ref[h] = s
            c_ref[h] = jnp.max(s, axis=0, keepdims=True)

    def accumulate(j, s_ref, c_ref, masked):
        vtb = vt_ref[j]
        for h in range(2):
            s = s_ref[h]
            if masked:
                s = causal_mask(s)
                c = jnp.max(s, axis=0, keepdims=True)
            else:
                c = c_ref[h]
            m_prev = m_ref[h]
            m_new = jnp.maximum(m_prev, c)
            alpha = jnp.exp2(m_prev - m_new)
            p = jnp.exp2(s - m_new)
            l_ref[h] = alpha * l_ref[h] + jnp.sum(p, axis=0, keepdims=True)
            acc_ref[h] = alpha * acc_ref[h] + jnp.dot(vtb, p.astype(BF16), preferred_element_type=F32)
            m_ref[h] = m_new

    def finalize():
        lf = lam_ref[...]
        lam = (jnp.exp(jnp.sum(lf[0:1] * lf[1:2], axis=1, keepdims=True))
               - jnp.exp(jnp.sum(lf[2:3] * lf[3:4], axis=1, keepdims=True)) + lambda_init)
        o = (acc_ref[0] / l_ref[0] - lam * (acc_ref[1] / l_ref[1])).T
        ms = jnp.mean(o * o, axis=-1, keepdims=True)
        o = o * lax.rsqrt(ms + EPS) * g_ref[...] * (1.0 - lambda_init)
        o_ref[...] = o.astype(o_ref.dtype)

    finalize()
    total = jnp.sum(l_ref[...]) + jnp.sum(acc_ref[...])
    overflowed = jnp.logical_not(jnp.abs(total) < jnp.inf)

    @pl.when(overflowed)
    def _():
        reset_state()
        scores(0, sa_ref, ca_ref)

        def slow_pair(p, carry):
            j = 2 * p
            scores(j + 1, sb_ref, cb_ref)
            accumulate(j, sa_ref, ca_ref, False)
            scores(j + 2, sa_ref, ca_ref)
            accumulate(j + 1, sb_ref, cb_ref, False)
            return carry

        lax.fori_loop(0, i // 2, slow_pair, 0)

        @pl.when(i % 2 == 1)
        def _():
            scores(i, sb_ref, cb_ref)
            accumulate(i - 1, sa_ref, ca_ref, False)
            accumulate(i, sb_ref, cb_ref, True)

        @pl.when(i % 2 == 0)
        def _():
            accumulate(i, sa_ref, ca_ref, True)

        finalize()


def _diff_attention(proj, lambdas, norm_g, lambda_init, blk):
    t = proj.shape[0]
    qo, ko, vo = OFF_DQ // LANES, OFF_DK // LANES, OFF_DV // LANES
    stat = pltpu.VMEM((2, 1, blk), F32)
    prob_buffer = [pltpu.VMEM((2, blk, blk), BF16), stat, stat, stat]
    return pl.pallas_call(
        functools.partial(_diff_attn_kernel, blk=blk, lambda_init=lambda_init),
        out_shape=jax.ShapeDtypeStruct((t, DIFF_WIDTH), BF16),
        grid=(DIFF_HEADS, t // blk),
        in_specs=[
            pl.BlockSpec((4, DIFF_HEAD_DIM), lambda h, i: (0, 0)),
            pl.BlockSpec((blk, LANES), lambda h, i: (i, qo + h)),
            pl.BlockSpec((t, LANES), lambda h, i: (0, ko + h)),
            pl.BlockSpec((t, LANES), lambda h, i: (0, vo + h)),
            pl.BlockSpec((1, LANES), lambda h, i: (0, 0)),
        ],
        out_specs=pl.BlockSpec((blk, LANES), lambda h, i: (i, h)),
        scratch_shapes=[
            pltpu.VMEM((t // blk, LANES, blk), BF16),
            pltpu.VMEM((2, LANES, blk), BF16),
            pltpu.VMEM((2, blk, blk), F32),
            pltpu.VMEM((2, blk, blk), F32),
            pltpu.VMEM((2, blk, blk), BF16),
            pltpu.VMEM((2, blk, blk), BF16),
            stat, stat,
            stat, stat,
            stat, stat,
            *prob_buffer,
            stat,
            stat,
            pltpu.VMEM((2, LANES, blk), F32),
        ],
        compiler_params=_cparams(("arbitrary", "arbitrary")),
        name="diff_attention",
    )(lambdas, proj, proj, proj, norm_g)


def _gla_kernel(gv_ref, gr_ref, gq_ref, gk_ref, code_ref, w2_ref, gb_ref, ng_ref, o_ref, st_ref, g_ref, b_ref,
                kf_ref, st0_ref):
    tile = gq_ref.shape[0]
    c_len = GLA_CHUNK
    n_chunks = tile // c_len

    @pl.when(pl.program_id(0) == 0)
    def _():
        st_ref[...] = jnp.zeros(st_ref.shape, F32)

    z = jnp.dot(code_ref[...], w2_ref[...], preferred_element_type=F32) + gb_ref[...]
    g_ref[...] = (jnp.minimum(z, 0.0) - jnp.log1p(jnp.exp(-jnp.abs(z)))) * (1.0 / GLA_TAU)

    row = lax.broadcasted_iota(jnp.int32, (c_len, c_len), 0)
    col = lax.broadcasted_iota(jnp.int32, (c_len, c_len), 1)
    causal = col <= row
    tri = jnp.where(causal, 1.0, 0.0).astype(BF16)
    ng = ng_ref[...]

    def chunk(c, direct):
        r0 = c * c_len if isinstance(c, int) else pl.multiple_of(c * c_len, c_len)
        g = g_ref[pl.ds(r0, c_len), :]
        g_hi = g.astype(BF16)
        g_lo = (g - g_hi.astype(F32)).astype(BF16)
        b = (jnp.dot(tri, g_hi, preferred_element_type=F32)
             + jnp.dot(tri, g_lo, preferred_element_type=F32))
        b_last = b[c_len - 1:c_len, :]
        q = gq_ref[pl.ds(r0, c_len), :].astype(F32) * (GLA_DK ** -0.5)
        k = gk_ref[pl.ds(r0, c_len), :].astype(F32)
        qd = (q * jnp.exp(b)).astype(BF16)
        kl = (k * jnp.exp(b_last - b)).astype(BF16)
        e_last = jnp.exp(b_last)
        if direct:
            b_ref[...] = b
            kf_ref[...] = k
        else:
            kd = (k * jnp.exp(-b)).astype(BF16)
        for h in range(GLA_HEADS):
            ks = slice(h * GLA_DK_PAD, (h + 1) * GLA_DK_PAD)
            vs = slice(h * GLA_DV_PAD, (h + 1) * GLA_DV_PAD)
            vh = gv_ref[pl.ds(r0, c_len), vs]
            st = st_ref[h]
            if direct:
                qh, bh = q[:, ks], b[:, ks]

                def columns(grp, a):
                    g0 = pl.multiple_of(grp * 8, 8)
                    b8 = b_ref[pl.ds(g0, 8), ks]
                    k8 = kf_ref[pl.ds(g0, 8), ks]
                    for u in range(8):
                        w = qh * k8[u:u + 1] * jnp.exp(jnp.minimum(bh - b8[u:u + 1], 0.0))
                        a = jnp.where(col == g0 + u, jnp.sum(w, axis=1, keepdims=True), a)
                    return a

                a = lax.fori_loop(0, c_len // 8, columns, jnp.zeros((c_len, c_len), F32))
            else:
                a = lax.dot_general(qd[:, ks], kd[:, ks], (((1,), (1,)), ((), ())), preferred_element_type=F32)
            a = jnp.where(causal, a, 0.0).astype(BF16)
            o = jnp.dot(a, vh, preferred_element_type=F32)
            o = o + lax.dot_general(qd[:, ks], st.astype(BF16), (((1,), (1,)), ((), ())),
                                    preferred_element_type=F32)
            st_ref[h] = st * e_last[:, ks] + lax.dot_general(
                vh, kl[:, ks], (((0,), (0,)), ((), ())), preferred_element_type=F32)
            ms = jnp.sum(o * o, axis=-1, keepdims=True) * (1.0 / GLA_DV)
            on = o * lax.rsqrt(ms + EPS) * ng
            r = gr_ref[pl.ds(r0, c_len), vs].astype(F32)
            o_ref[pl.ds(r0, c_len), vs] = (on * (r / (1.0 + jnp.exp(-r)))).astype(o_ref.dtype)

    st0_ref[...] = st_ref[...]
    for c in range(n_chunks):
        chunk(c, direct=False)

    lowest = jnp.zeros((1, g_ref.shape[1]), F32)
    for c in range(n_chunks):
        lowest = jnp.minimum(lowest, jnp.sum(g_ref[c * c_len:(c + 1) * c_len, :], axis=0, keepdims=True))

    @pl.when(jnp.min(lowest) < -GLA_SAFE_LOG_DECAY)
    def _():
        st_ref[...] = st0_ref[...]

        def body(c, carry):
            chunk(c, direct=True)
            return carry
        lax.fori_loop(0, n_chunks, body, 0)


def _gla_mixer(proj, w2p, gbp, ngp, tile):
    t = proj.shape[0]
    kw = GLA_HEADS * GLA_DK_PAD
    state = pltpu.VMEM((GLA_HEADS, GLA_DV_PAD, GLA_DK_PAD), F32)
    return pl.pallas_call(
        _gla_kernel,
        out_shape=jax.ShapeDtypeStruct((t, MIX_C_WIDTH), BF16),
        grid=(t // tile,),
        in_specs=[
            pl.BlockSpec((tile, MIX_C_WIDTH), lambda i: (i, OFF_GV // MIX_C_WIDTH)),
            pl.BlockSpec((tile, MIX_C_WIDTH), lambda i: (i, OFF_GR // MIX_C_WIDTH)),
            pl.BlockSpec((tile, kw), lambda i: (i, OFF_GQ // kw)),
            pl.BlockSpec((tile, kw), lambda i: (i, OFF_GK // kw)),
            pl.BlockSpec((tile, LANES), lambda i: (i, OFF_CODE // LANES)),
            pl.BlockSpec((LANES, kw), lambda i: (0, 0)),
            pl.BlockSpec((1, kw), lambda i: (0, 0)),
            pl.BlockSpec((1, GLA_DV_PAD), lambda i: (0, 0)),
        ],
        out_specs=pl.BlockSpec((tile, MIX_C_WIDTH), lambda i: (i, 0)),
        scratch_shapes=[
            state,
            pltpu.VMEM((tile, kw), F32),
            pltpu.VMEM((GLA_CHUNK, kw), F32),
            pltpu.VMEM((GLA_CHUNK, kw), F32),
            state,
        ],
        compiler_params=_cparams(("arbitrary",)),
        name="gla_mixer",
    )(proj, proj, proj, proj, proj, w2p, gbp, ngp)


def _final_norm_kernel(x_ref, g_ref, o_ref):
    x = x_ref[...]
    ms = jnp.mean(x * x, axis=-1, keepdims=True)
    o_ref[...] = x * lax.rsqrt(ms + EPS) * g_ref[...]


def _final_norm(x, g, tm):
    t, d = x.shape
    return pl.pallas_call(
        _final_norm_kernel,
        out_shape=jax.ShapeDtypeStruct((t, d), F32),
        grid=(t // tm,),
        in_specs=[pl.BlockSpec((tm, d), lambda i: (i, 0)), pl.BlockSpec((1, d), lambda i: (0, 0))],
        out_specs=pl.BlockSpec((tm, d), lambda i: (i, 0)),
        compiler_params=_cparams(("parallel",)),
        name="final_norm",
    )(x, g)


def _pad_heads(w, heads, d, d_pad):
    lead = w.shape[:-1]
    w = w.reshape(lead + (heads, d))
    w = jnp.pad(w, [(0, 0)] * len(lead) + [(0, 0), (0, d_pad - d)])
    return w.reshape(lead + (heads * d_pad,))


def _head_pieces(w, axis, off, heads, d, d_pad):
    pieces = []
    for h in range(heads):
        sl = lax.slice_in_dim(w, off + h * d, off + (h + 1) * d, axis=axis)
        pieces.append(sl)
        zshape = list(w.shape)
        zshape[axis] = d_pad - d
        pieces.append(jnp.zeros(zshape, w.dtype))
    return pieces


def _layout_w_in(w_in):
    splits = [GMLP_WIDTH, GMLP_WIDTH, DIFF_WIDTH, DIFF_WIDTH, DIFF_WIDTH,
              GLA_KEY_WIDTH, GLA_KEY_WIDTH, GLA_WIDTH, GLA_WIDTH, GLA_RANK]
    idx = [0]
    for s in splits:
        idx.append(idx[-1] + s)
    o_u, o_v, o_dq, o_dk, o_dv, o_gq, o_gk, o_gv, o_gr, o_code, end = idx
    ax = w_in.ndim - 1
    pieces = (_head_pieces(w_in, ax, o_gv, GLA_HEADS, GLA_DV, GLA_DV_PAD)
              + _head_pieces(w_in, ax, o_gr, GLA_HEADS, GLA_DV, GLA_DV_PAD)
              + _head_pieces(w_in, ax, o_gq, GLA_HEADS, GLA_DK, GLA_DK_PAD)
              + _head_pieces(w_in, ax, o_gk, GLA_HEADS, GLA_DK, GLA_DK_PAD)
              + [lax.slice_in_dim(w_in, o_u, o_gq, axis=ax),
                 lax.slice_in_dim(w_in, o_code, end, axis=ax),
                 jnp.zeros(w_in.shape[:-1] + (PROJ_WIDTH - OFF_CODE - GLA_RANK,), w_in.dtype)])
    return jnp.concatenate(pieces, axis=ax).astype(BF16)


def _layout_w_out(w_out):
    wa = w_out[:, :GMLP_WIDTH, :].astype(BF16)
    wb = w_out[:, GMLP_WIDTH:GMLP_WIDTH + DIFF_WIDTH, :].astype(BF16)
    pieces = _head_pieces(w_out, 1, GMLP_WIDTH + DIFF_WIDTH, GLA_HEADS, GLA_DV, GLA_DV_PAD)
    return wa, wb, jnp.concatenate(pieces, axis=1).astype(BF16)


def kernel(x, norm1_g, w_in, gmlp_ln_g, spatial_w, spatial_b, diff_lambdas, diff_norm_g, gla_gate_w2,
           gla_gate_b, gla_norm_g, w_out, norm2_g, w_ffn_in, w_ffn_out, final_g):
    b, t, d = x.shape
    assert b == 1 and d == D_MODEL
    xs = x.reshape(t, d)

    w2p = jnp.pad(_pad_heads(gla_gate_w2, GLA_HEADS, GLA_DK, GLA_DK_PAD),
                  [(0, 0), (0, LANES - GLA_RANK), (0, 0)]).astype(BF16)
    gbp = _pad_heads(gla_gate_b, GLA_HEADS, GLA_DK, GLA_DK_PAD)[:, None, :]
    ngp = jnp.pad(gla_norm_g, [(0, 0), (0, GLA_DV_PAD - GLA_DV)])[:, None, :]
    sb_t = jnp.swapaxes(spatial_b, 1, 2)

    col = jnp.arange(PROJ_WIDTH)
    q_scale = (DIFF_HEAD_DIM ** -0.5) * math.log2(math.e)
    col_scale = jnp.where((col >= OFF_DQ) & (col < OFF_DK), q_scale, 1.0).astype(F32)[None, :]

    wa, wb, wc = _layout_w_out(w_out)
    wf_in = w_ffn_in.astype(BF16)
    wf_out = w_ffn_out.astype(BF16)

    for l in range(DEPTH):
        lambda_init = 0.8 - 0.6 * math.exp(-0.3 * l)
        w1 = _layout_w_in(w_in[l])
        proj = _norm_matmul(xs, norm1_g[l][None, :], w1, col_scale, min(512, t), PROJ_WIDTH // 2)
        a_out = _gmlp_mixer(proj, gmlp_ln_g[l][None, :], spatial_w[l], sb_t[l], min(512, t))
        b_out = _diff_attention(proj, diff_lambdas[l], diff_norm_g[l][None, :], lambda_init, min(512, t))
        c_out = _gla_mixer(proj, w2p[l], gbp[l], ngp[l], min(256, t))
        xs, h2 = _matmul_residual_norm([a_out, b_out, c_out], [wa, wb, wc], l, xs, norm2_g[l][None, :],
                                       min(512, t), "out_proj")
        hid = _swiglu(h2, wf_in, l, min(2048, t), 512)
        xs = _matmul_residual([hid], [wf_out], l, xs, min(1024, t), 512, "ffn_out")
    out = _final_norm(xs, final_g[None, :], min(512, t))
    return out.reshape(b, t, d)
```

```python
import functools
import math

import jax
import jax.numpy as jnp
from jax import lax
from jax.experimental import pallas as pl
from jax.experimental.pallas import tpu as pltpu

F32 = jnp.float32
BF16 = jnp.bfloat16

D_MODEL = 2048
DEPTH = 4
EPS = 1e-6
LANES = 128

GMLP_WIDTH = 512
GMLP_GROUPS = 4
GMLP_CHUNK = 128
DIFF_WIDTH = 768
DIFF_HEADS = 6
DIFF_HEAD_DIM = 64
GLA_HEADS = 4
GLA_DK = 96
GLA_DV = 192
GLA_DK_PAD = 128
GLA_DV_PAD = 256
GLA_KEY_WIDTH = GLA_HEADS * GLA_DK
GLA_WIDTH = GLA_HEADS * GLA_DV
GLA_RANK = 16
GLA_TAU = 16.0
GLA_CHUNK = 64
GLA_SAFE_LOG_DECAY = 80.0
FFN_HIDDEN = 5632

OFF_GV = 0
OFF_GR = OFF_GV + GLA_HEADS * GLA_DV_PAD
OFF_GQ = OFF_GR + GLA_HEADS * GLA_DV_PAD
OFF_GK = OFF_GQ + GLA_HEADS * GLA_DK_PAD
OFF_U = OFF_GK + GLA_HEADS * GLA_DK_PAD
OFF_V = OFF_U + GMLP_WIDTH
OFF_DQ = OFF_V + GMLP_WIDTH
OFF_DK = OFF_DQ + DIFF_WIDTH
OFF_DV = OFF_DK + DIFF_WIDTH
OFF_CODE = OFF_DV + DIFF_WIDTH
PROJ_TN = 512
PROJ_WIDTH = -(-(OFF_CODE + LANES) // PROJ_TN) * PROJ_TN
MIX_C_WIDTH = GLA_HEADS * GLA_DV_PAD

VMEM_LIMIT = 56 * 1024 * 1024


def _cparams(sem):
    return pltpu.CompilerParams(dimension_semantics=sem, vmem_limit_bytes=VMEM_LIMIT)


def _rmsnorm_rows(x_ref, g_ref, h_ref, rows):
    tm = x_ref.shape[0]
    g = g_ref[...]
    for r in range(0, tm, rows):
        x = x_ref[r:r + rows, :]
        ms = jnp.mean(x * x, axis=-1, keepdims=True)
        h_ref[r:r + rows, :] = (x * lax.rsqrt(ms + EPS) * g).astype(BF16)


def _norm_mm_kernel(x_ref, g_ref, w_ref, cs_ref, o_ref, h_ref):
    @pl.when(pl.program_id(1) == 0)
    def _():
        _rmsnorm_rows(x_ref, g_ref, h_ref, 256)

    acc = jnp.dot(h_ref[...], w_ref[...], preferred_element_type=F32)
    o_ref[...] = (acc * cs_ref[...]).astype(o_ref.dtype)


def _swiglu_kernel(h_ref, wg_ref, wu_ref, o_ref):
    h = h_ref[...]
    gate = jnp.dot(h, wg_ref[...], preferred_element_type=F32)
    up = jnp.dot(h, wu_ref[...], preferred_element_type=F32)
    o_ref[...] = (gate / (1.0 + jnp.exp(-gate)) * up).astype(o_ref.dtype)


def _norm_matmul(x, g, w, col_scale, tm, tn):
    t, d = x.shape
    n = w.shape[1]
    return pl.pallas_call(
        _norm_mm_kernel,
        out_shape=jax.ShapeDtypeStruct((t, n), BF16),
        grid=(t // tm, n // tn),
        in_specs=[
            pl.BlockSpec((tm, d), lambda i, j: (i, 0)),
            pl.BlockSpec((1, d), lambda i, j: (0, 0)),
            pl.BlockSpec((d, tn), lambda i, j: (0, j)),
            pl.BlockSpec((1, tn), lambda i, j: (0, j)),
        ],
        out_specs=pl.BlockSpec((tm, tn), lambda i, j: (i, j)),
        scratch_shapes=[pltpu.VMEM((tm, d), BF16)],
        compiler_params=_cparams(("parallel", "arbitrary")),
        name="norm_in_proj",
    )(x, g, w, col_scale)


def _swiglu(h, w, layer, tm, tn):
    t, d = h.shape
    hidden = w.shape[2] // 2
    nj = hidden // tn
    return pl.pallas_call(
        _swiglu_kernel,
        out_shape=jax.ShapeDtypeStruct((t, hidden), BF16),
        grid=(t // tm, nj),
        in_specs=[
            pl.BlockSpec((tm, d), lambda i, j: (i, 0)),
            pl.BlockSpec((None, d, tn), lambda i, j: (layer, 0, j)),
            pl.BlockSpec((None, d, tn), lambda i, j: (layer, 0, j + nj)),
        ],
        out_specs=pl.BlockSpec((tm, tn), lambda i, j: (i, j)),
        compiler_params=_cparams(("parallel", "arbitrary")),
        name="ffn_in",
    )(h, w, w)


def _mm_res_kernel(*refs, n_a):
    a_refs, w_refs = refs[:n_a], refs[n_a:2 * n_a]
    r_ref, o_ref = refs[2 * n_a], refs[2 * n_a + 1]
    acc = r_ref[...]
    for a_ref, w_ref in zip(a_refs, w_refs):
        acc = acc + jnp.dot(a_ref[...], w_ref[...], preferred_element_type=F32)
    o_ref[...] = acc


def _mm_res_norm_kernel(*refs, n_a):
    a_refs, w_refs = refs[:n_a], refs[n_a:2 * n_a]
    r_ref, g_ref, o_ref, h_ref = refs[2 * n_a:2 * n_a + 4]
    acc = r_ref[...]
    for a_ref, w_ref in zip(a_refs, w_refs):
        acc = acc + jnp.dot(a_ref[...], w_ref[...], preferred_element_type=F32)
    o_ref[...] = acc
    ms = jnp.mean(acc * acc, axis=-1, keepdims=True)
    h_ref[...] = (acc * lax.rsqrt(ms + EPS) * g_ref[...]).astype(h_ref.dtype)


def _matmul_residual_norm(a_list, w_list, layer, res, g, tm, name):
    t, n = res.shape
    n_a = len(a_list)
    in_specs = [pl.BlockSpec((tm, a.shape[1]), lambda i: (i, 0)) for a in a_list]
    in_specs += [pl.BlockSpec((None, w.shape[1], n), lambda i: (layer, 0, 0)) for w in w_list]
    in_specs += [pl.BlockSpec((tm, n), lambda i: (i, 0)), pl.BlockSpec((1, n), lambda i: (0, 0))]
    return pl.pallas_call(
        functools.partial(_mm_res_norm_kernel, n_a=n_a),
        out_shape=(jax.ShapeDtypeStruct((t, n), F32), jax.ShapeDtypeStruct((t, n), BF16)),
        grid=(t // tm,),
        in_specs=in_specs,
        out_specs=(pl.BlockSpec((tm, n), lambda i: (i, 0)), pl.BlockSpec((tm, n), lambda i: (i, 0))),
        compiler_params=_cparams(("parallel",)),
        name=name,
    )(*a_list, *w_list, res, g)


def _matmul_residual(a_list, w_list, layer, res, tm, tn, name):
    t, n = res.shape
    n_a = len(a_list)
    in_specs = [pl.BlockSpec((tm, a.shape[1]), lambda i, j: (i, 0)) for a in a_list]
    in_specs += [pl.BlockSpec((None, w.shape[1], tn), lambda i, j: (layer, 0, j)) for w in w_list]
    in_specs += [pl.BlockSpec((tm, tn), lambda i, j: (i, j))]
    return pl.pallas_call(
        functools.partial(_mm_res_kernel, n_a=n_a),
        out_shape=jax.ShapeDtypeStruct((t, n), F32),
        grid=(t // tm, n // tn),
        in_specs=in_specs,
        out_specs=pl.BlockSpec((tm, tn), lambda i, j: (i, j)),
        compiler_params=_cparams(("parallel", "arbitrary")),
        name=name,
    )(*a_list, *w_list, res)


def _gelu_tanh(x):
    return 0.5 * x * (1.0 + jnp.tanh(math.sqrt(2.0 / math.pi) * (x + 0.044715 * (x * x * x))))


def _gmlp_kernel(u_ref, v_ref, lng_ref, w_ref, b_ref, o_ref):
    tile = u_ref.shape[0]
    gd = GMLP_WIDTH // GMLP_GROUPS
    v = _gelu_tanh(v_ref[...].astype(F32))
    mu = jnp.mean(v, axis=-1, keepdims=True)
    vc = v - mu
    var = jnp.mean(vc * vc, axis=-1, keepdims=True)
    vn = (vc * lax.rsqrt(var + EPS) * lng_ref[...]).astype(BF16)
    row = lax.broadcasted_iota(jnp.int32, (GMLP_CHUNK, GMLP_CHUNK), 0)
    col = lax.broadcasted_iota(jnp.int32, (GMLP_CHUNK, GMLP_CHUNK), 1)
    causal = col <= row
    bias = b_ref[...]
    for g in range(GMLP_GROUPS):
        wg = jnp.where(causal, w_ref[g], 0.0).astype(BF16)
        bg = bias[:, g:g + 1]
        for c in range(tile // GMLP_CHUNK):
            rs = slice(c * GMLP_CHUNK, (c + 1) * GMLP_CHUNK)
            cs = slice(g * gd, (g + 1) * gd)
            mixed = jnp.dot(wg, vn[rs, cs], preferred_element_type=F32) + bg
            u = _gelu_tanh(u_ref[rs, cs].astype(F32))
            o_ref[rs, cs] = (u * mixed).astype(o_ref.dtype)


def _gmlp_mixer(proj, ln_g, spatial_w, spatial_b_t, tile):
    t = proj.shape[0]
    return pl.pallas_call(
        _gmlp_kernel,
        out_shape=jax.ShapeDtypeStruct((t, GMLP_WIDTH), BF16),
        grid=(t // tile,),
        in_specs=[
            pl.BlockSpec((tile, GMLP_WIDTH), lambda i: (i, OFF_U // GMLP_WIDTH)),
            pl.BlockSpec((tile, GMLP_WIDTH), lambda i: (i, OFF_V // GMLP_WIDTH)),
            pl.BlockSpec((1, GMLP_WIDTH), lambda i: (0, 0)),
            pl.BlockSpec((GMLP_GROUPS, GMLP_CHUNK, GMLP_CHUNK), lambda i: (0, 0, 0)),
            pl.BlockSpec((GMLP_CHUNK, GMLP_GROUPS), lambda i: (0, 0)),
        ],
        out_specs=pl.BlockSpec((tile, GMLP_WIDTH), lambda i: (i, 0)),
        compiler_params=_cparams(("parallel",)),
        name="gmlp_mixer",
    )(proj, proj, ln_g, spatial_w, spatial_b_t)


def _diff_attn_kernel(lam_ref, q_ref, qn_ref, k_ref, v_ref, g_ref, o_ref, vt_ref, qt_ref, qtn_ref, sa_ref, sb_ref,
                      pa_ref, pb_ref, ca_ref, cb_ref, sta_ref, stb_ref, psa_ref, psb_ref, pc_ref, cc_ref, stc_ref,
                      psc_ref, m_ref, l_ref, acc_ref, *, blk, lambda_init):
    i = pl.program_id(1)
    n_blk = vt_ref.shape[0]

    def transposed_halves(q_blk, dst_ref):
        qt = q_blk.astype(F32).T
        sub = lax.broadcasted_iota(jnp.int32, (LANES, blk), 0)
        dst_ref[0] = jnp.where(sub < DIFF_HEAD_DIM, qt, 0.0).astype(BF16)
        dst_ref[1] = jnp.where(sub >= DIFF_HEAD_DIM, qt, 0.0).astype(BF16)

    @pl.when(i == 0)
    def _():
        def transpose_v(c, carry):
            r0 = pl.multiple_of(c * blk, blk)
            vt_ref[c] = v_ref[pl.ds(r0, blk), :].astype(F32).T.astype(BF16)
            return carry
        lax.fori_loop(0, n_blk, transpose_v, 0)

    transposed_halves(q_ref[...], qt_ref)

    def reset_state():
        m_ref[...] = jnp.full(m_ref.shape, -jnp.inf, F32)
        l_ref[...] = jnp.zeros(l_ref.shape, F32)
        acc_ref[...] = jnp.zeros(acc_ref.shape, F32)

    def k_block(j):
        return k_ref[pl.ds(pl.multiple_of(j * blk, blk), blk), :]

    def causal_mask(s):
        row = lax.broadcasted_iota(jnp.int32, (blk, blk), 0)
        col = lax.broadcasted_iota(jnp.int32, (blk, blk), 1)
        return jnp.where(row <= col, s, -jnp.inf)

    def probs(j, cur, prev, first=False, masked=False, qt_ref=qt_ref):
        p_ref, st_ref, lse_ref, ps_ref = cur
        kb = k_block(j)
        for h in range(2):
            s = jnp.dot(kb, qt_ref[h], preferred_element_type=F32)
            if masked:
                s = causal_mask(s)
            if first:
                stab = jnp.max(s, axis=0, keepdims=True)
            else:
                stab = jnp.maximum(prev[1][h], prev[2][h])
            p = jnp.exp2(s - stab)
            ps = jnp.sum(p, axis=0, keepdims=True)
            ps_ref[h] = ps
            p_ref[h] = p.astype(BF16)
            lse_ref[h] = stab + jnp.log2(ps)
            st_ref[h] = stab

    def fold(j, cur):
        p_ref, st_ref, _, ps_ref = cur
        vtb = vt_ref[j]
        for h in range(2):
            stab = st_ref[h]
            alpha = jnp.exp2(m_ref[h] - stab)
            l_ref[h] = alpha * l_ref[h] + ps_ref[h]
            acc_ref[h] = alpha * acc_ref[h] + jnp.dot(vtb, p_ref[h], preferred_element_type=F32)
            m_ref[h] = stab

    buf_a = (pa_ref, sta_ref, ca_ref, psa_ref)
    buf_b = (pb_ref, stb_ref, cb_ref, psb_ref)
    buf_c = (pc_ref, stc_ref, cc_ref, psc_ref)

    def fast_two_blocks(j):
        probs(j + 1, buf_b, buf_a)
        fold(j, buf_a)
        probs(j + 2, buf_a, buf_b)
        fold(j + 1, buf_b)

    reset_state()
    n_pairs = jnp.maximum(i - 1, 0) // 2

    def fast_oct(p, carry):
        for u in range(4):
            fast_two_blocks(8 * p + 2 * u)
        return carry

    def fast_pair(p, carry):
        fast_two_blocks(8 * (n_pairs // 4) + 2 * p)
        return carry

    lax.fori_loop(0, n_pairs // 4, fast_oct, 0)
    lax.fori_loop(0, n_pairs % 4, fast_pair, 0)
    j0 = 2 * n_pairs

    @pl.when(i == 0)
    def _():
        probs(0, buf_c, None, first=True, masked=True)
        fold(0, buf_c)

    @pl.when(i % 2 == 1)
    def _():
        probs(i, buf_b, buf_a, masked=True)
        fold(j0, buf_a)
        fold(i, buf_b)

    @pl.when((i % 2 == 0) & (i > 0))
    def _():
        probs(j0 + 1, buf_b, buf_a)
        fold(j0, buf_a)
        probs(i, buf_c, buf_b, masked=True)
        fold(j0 + 1, buf_b)
        fold(i, buf_c)

    def scores(j, s_ref, c_ref):
        kb = k_block(j)
        for h in range(2):
            s = jnp.dot(kb, qt_ref[h], preferred_element_type=F32)
            s_ref[h] = s
            c_ref[h] = jnp.max(s, axis=0, keepdims=True)

    def accumulate(j, s_ref, c_ref, masked):
        vtb = vt_ref[j]
        for h in range(2):
            s = s_ref[h]
            if masked:
                s = causal_mask(s)
                c = jnp.max(s, axis=0, keepdims=True)
            else:
                c = c_ref[h]
            m_prev = m_ref[h]
            m_new = jnp.maximum(m_prev, c)
            alpha = jnp.exp2(m_prev - m_new)
            p = jnp.exp2(s - m_new)
            l_ref[h] = alpha * l_ref[h] + jnp.sum(p, axis=0, keepdims=True)
            acc_ref[h] = alpha * acc_ref[h] + jnp.dot(vtb, p.astype(BF16), preferred_element_type=F32)
            m_ref[h] = m_new

    def finalize():
        lf = lam_ref[...]
        lam = (jnp.exp(jnp.sum(lf[0:1] * lf[1:2], axis=1, keepdims=True))
               - jnp.exp(jnp.sum(lf[2:3] * lf[3:4], axis=1, keepdims=True)) + lambda_init)
        o = (acc_ref[0] / l_ref[0] - lam * (acc_ref[1] / l_ref[1])).T
        ms = jnp.mean(o * o, axis=-1, keepdims=True)
        o = o * lax.rsqrt(ms + EPS) * g_ref[...] * (1.0 - lambda_init)
        o_ref[...] = o.astype(o_ref.dtype)

    finalize()
    transposed_halves(qn_ref[...], qtn_ref)
    probs(0, buf_a, None, first=True, qt_ref=qtn_ref)
    total = jnp.sum(l_ref[...]) + jnp.sum(acc_ref[...])
    overflowed = jnp.logical_not(jnp.abs(total) < jnp.inf)

    @pl.when(overflowed)
    def _():
        reset_state()
        scores(0, sa_ref, ca_ref)

        def slow_pair(p, carry):
            j = 2 * p
            scores(j + 1, sb_ref, cb_ref)
            accumulate(j, sa_ref, ca_ref, False)
            scores(j + 2, sa_ref, ca_ref)
            accumulate(j + 1, sb_ref, cb_ref, False)
            return carry

        lax.fori_loop(0, i // 2, slow_pair, 0)

        @pl.when(i % 2 == 1)
        def _():
            scores(i, sb_ref, cb_ref)
            accumulate(i - 1, sa_ref, ca_ref, False)
            accumulate(i, sb_ref, cb_ref, True)

        @pl.when(i % 2 == 0)
        def _():
            accumulate(i, sa_ref, ca_ref, True)

        finalize()


def _diff_attention(proj, lambdas, norm_g, lambda_init, blk):
    t = proj.shape[0]
    qo, ko, vo = OFF_DQ // LANES, OFF_DK // LANES, OFF_DV // LANES
    n_q = t // blk
    stat = pltpu.VMEM((2, 1, blk), F32)
    prob_buffer = [pltpu.VMEM((2, blk, blk), BF16), stat, stat, stat]
    return pl.pallas_call(
        functools.partial(_diff_attn_kernel, blk=blk, lambda_init=lambda_init),
        out_shape=jax.ShapeDtypeStruct((t, DIFF_WIDTH), BF16),
        grid=(DIFF_HEADS, t // blk),
        in_specs=[
            pl.BlockSpec((4, DIFF_HEAD_DIM), lambda h, i: (0, 0)),
            pl.BlockSpec((blk, LANES), lambda h, i: (i, qo + h)),
            pl.BlockSpec((blk, LANES), lambda h, i: (jnp.minimum(i + 1, n_q - 1), qo + h)),
            pl.BlockSpec((t, LANES), lambda h, i: (0, ko + h)),
            pl.BlockSpec((t, LANES), lambda h, i: (0, vo + h)),
            pl.BlockSpec((1, LANES), lambda h, i: (0, 0)),
        ],
        out_specs=pl.BlockSpec((blk, LANES), lambda h, i: (i, h)),
        scratch_shapes=[
            pltpu.VMEM((t // blk, LANES, blk), BF16),
            pltpu.VMEM((2, LANES, blk), BF16),
            pltpu.VMEM((2, LANES, blk), BF16),
            pltpu.VMEM((2, blk, blk), F32),
            pltpu.VMEM((2, blk, blk), F32),
            pltpu.VMEM((2, blk, blk), BF16),
            pltpu.VMEM((2, blk, blk), BF16),
            stat, stat,
            stat, stat,
            stat, stat,
            *prob_buffer,
            stat,
            stat,
            pltpu.VMEM((2, LANES, blk), F32),
        ],
        compiler_params=_cparams(("arbitrary", "arbitrary")),
        name="diff_attention",
    )(lambdas, proj, proj, proj, proj, norm_g)


def _gla_kernel(gv_ref, gr_ref, gq_ref, gk_ref, code_ref, w2_ref, gb_ref, ng_ref, o_ref, st_ref, g_ref, b_ref,
                kf_ref, st0_ref):
    tile = gq_ref.shape[0]
    c_len = GLA_CHUNK
    n_chunks = tile // c_len

    @pl.when(pl.program_id(0) == 0)
    def _():
        st_ref[...] = jnp.zeros(st_ref.shape, F32)

    z = jnp.dot(code_ref[...], w2_ref[...], preferred_element_type=F32) + gb_ref[...]
    g_ref[...] = (jnp.minimum(z, 0.0) - jnp.log1p(jnp.exp(-jnp.abs(z)))) * (1.0 / GLA_TAU)

    row = lax.broadcasted_iota(jnp.int32, (c_len, c_len), 0)
    col = lax.broadcasted_iota(jnp.int32, (c_len, c_len), 1)
    causal = col <= row
    tri = jnp.where(causal, 1.0, 0.0).astype(BF16)
    ng = ng_ref[...]

    def chunk(c, direct):
        r0 = c * c_len if isinstance(c, int) else pl.multiple_of(c * c_len, c_len)
        g = g_ref[pl.ds(r0, c_len), :]
        g_hi = g.astype(BF16)
        g_lo = (g - g_hi.astype(F32)).astype(BF16)
        b = (jnp.dot(tri, g_hi, preferred_element_type=F32)
             + jnp.dot(tri, g_lo, preferred_element_type=F32))
        b_last = b[c_len - 1:c_len, :]
        q = gq_ref[pl.ds(r0, c_len), :].astype(F32) * (GLA_DK ** -0.5)
        k = gk_ref[pl.ds(r0, c_len), :].astype(F32)
        qd = (q * jnp.exp(b)).astype(BF16)
        kl = (k * jnp.exp(b_last - b)).astype(BF16)
        e_last = jnp.exp(b_last)
        if direct:
            b_ref[...] = b
            kf_ref[...] = k
        else:
            kd = (k * jnp.exp(-b)).astype(BF16)
        for h in range(GLA_HEADS):
            ks = slice(h * GLA_DK_PAD, (h + 1) * GLA_DK_PAD)
            vs = slice(h * GLA_DV_PAD, (h + 1) * GLA_DV_PAD)
            vh = gv_ref[pl.ds(r0, c_len), vs]
            st = st_ref[h]
            if direct:
                qh, bh = q[:, ks], b[:, ks]

                def columns(grp, a):
                    g0 = pl.multiple_of(grp * 8, 8)
                    b8 = b_ref[pl.ds(g0, 8), ks]
                    k8 = kf_ref[pl.ds(g0, 8), ks]
                    for u in range(8):
                        w = qh * k8[u:u + 1] * jnp.exp(jnp.minimum(bh - b8[u:u + 1], 0.0))
                        a = jnp.where(col == g0 + u, jnp.sum(w, axis=1, keepdims=True), a)
                    return a

                a = lax.fori_loop(0, c_len // 8, columns, jnp.zeros((c_len, c_len), F32))
            else:
                a = lax.dot_general(qd[:, ks], kd[:, ks], (((1,), (1,)), ((), ())), preferred_element_type=F32)
            a = jnp.where(causal, a, 0.0).astype(BF16)
            o = jnp.dot(a, vh, preferred_element_type=F32)
            o = o + lax.dot_general(qd[:, ks], st.astype(BF16), (((1,), (1,)), ((), ())),
                                    preferred_element_type=F32)
            st_ref[h] = st * e_last[:, ks] + lax.dot_general(
                vh, kl[:, ks], (((0,), (0,)), ((), ())), preferred_element_type=F32)
            ms = jnp.sum(o * o, axis=-1, keepdims=True) * (1.0 / GLA_DV)
            on = o * lax.rsqrt(ms + EPS) * ng
            r = gr_ref[pl.ds(r0, c_len), vs].astype(F32)
            o_ref[pl.ds(r0, c_len), vs] = (on * (r / (1.0 + jnp.exp(-r)))).astype(o_ref.dtype)

    st0_ref[...] = st_ref[...]
    for c in range(n_chunks):
        chunk(c, direct=False)

    lowest = jnp.zeros((1, g_ref.shape[1]), F32)
    for c in range(n_chunks):
        lowest = jnp.minimum(lowest, jnp.sum(g_ref[c * c_len:(c + 1) * c_len, :], axis=0, keepdims=True))

    @pl.when(jnp.min(lowest) < -GLA_SAFE_LOG_DECAY)
    def _():
        st_ref[...] = st0_ref[...]

        def body(c, carry):
            chunk(c, direct=True)
            return carry
        lax.fori_loop(0, n_chunks, body, 0)


def _gla_mixer(proj, w2p, gbp, ngp, tile):
    t = proj.shape[0]
    kw = GLA_HEADS * GLA_DK_PAD
    state = pltpu.VMEM((GLA_HEADS, GLA_DV_PAD, GLA_DK_PAD), F32)
    return pl.pallas_call(
        _gla_kernel,
        out_shape=jax.ShapeDtypeStruct((t, MIX_C_WIDTH), BF16),
        grid=(t // tile,),
        in_specs=[
            pl.BlockSpec((tile, MIX_C_WIDTH), lambda i: (i, OFF_GV // MIX_C_WIDTH)),
            pl.BlockSpec((tile, MIX_C_WIDTH), lambda i: (i, OFF_GR // MIX_C_WIDTH)),
            pl.BlockSpec((tile, kw), lambda i: (i, OFF_GQ // kw)),
            pl.BlockSpec((tile, kw), lambda i: (i, OFF_GK // kw)),
            pl.BlockSpec((tile, LANES), lambda i: (i, OFF_CODE // LANES)),
            pl.BlockSpec((LANES, kw), lambda i: (0, 0)),
            pl.BlockSpec((1, kw), lambda i: (0, 0)),
            pl.BlockSpec((1, GLA_DV_PAD), lambda i: (0, 0)),
        ],
        out_specs=pl.BlockSpec((tile, MIX_C_WIDTH), lambda i: (i, 0)),
        scratch_shapes=[
            state,
            pltpu.VMEM((tile, kw), F32),
            pltpu.VMEM((GLA_CHUNK, kw), F32),
            pltpu.VMEM((GLA_CHUNK, kw), F32),
            state,
        ],
        compiler_params=_cparams(("arbitrary",)),
        name="gla_mixer",
    )(proj, proj, proj, proj, proj, w2p, gbp, ngp)


def _final_norm_kernel(x_ref, g_ref, o_ref):
    x = x_ref[...]
    ms = jnp.mean(x * x, axis=-1, keepdims=True)
    o_ref[...] = x * lax.rsqrt(ms + EPS) * g_ref[...]


def _final_norm(x, g, tm):
    t, d = x.shape
    return pl.pallas_call(
        _final_norm_kernel,
        out_shape=jax.ShapeDtypeStruct((t, d), F32),
        grid=(t // tm,),
        in_specs=[pl.BlockSpec((tm, d), lambda i: (i, 0)), pl.BlockSpec((1, d), lambda i: (0, 0))],
        out_specs=pl.BlockSpec((tm, d), lambda i: (i, 0)),
        compiler_params=_cparams(("parallel",)),
        name="final_norm",
    )(x, g)


def _pad_heads(w, heads, d, d_pad):
    lead = w.shape[:-1]
    w = w.reshape(lead + (heads, d))
    w = jnp.pad(w, [(0, 0)] * len(lead) + [(0, 0), (0, d_pad - d)])
    return w.reshape(lead + (heads * d_pad,))


def _head_pieces(w, axis, off, heads, d, d_pad):
    pieces = []
    for h in range(heads):
        sl = lax.slice_in_dim(w, off + h * d, off + (h + 1) * d, axis=axis)
        pieces.append(sl)
        zshape = list(w.shape)
        zshape[axis] = d_pad - d
        pieces.append(jnp.zeros(zshape, w.dtype))
    return pieces


def _layout_w_in(w_in):
    splits = [GMLP_WIDTH, GMLP_WIDTH, DIFF_WIDTH, DIFF_WIDTH, DIFF_WIDTH,
              GLA_KEY_WIDTH, GLA_KEY_WIDTH, GLA_WIDTH, GLA_WIDTH, GLA_RANK]
    idx = [0]
    for s in splits:
        idx.append(idx[-1] + s)
    o_u, o_v, o_dq, o_dk, o_dv, o_gq, o_gk, o_gv, o_gr, o_code, end = idx
    ax = w_in.ndim - 1
    pieces = (_head_pieces(w_in, ax, o_gv, GLA_HEADS, GLA_DV, GLA_DV_PAD)
              + _head_pieces(w_in, ax, o_gr, GLA_HEADS, GLA_DV, GLA_DV_PAD)
              + _head_pieces(w_in, ax, o_gq, GLA_HEADS, GLA_DK, GLA_DK_PAD)
              + _head_pieces(w_in, ax, o_gk, GLA_HEADS, GLA_DK, GLA_DK_PAD)
              + [lax.slice_in_dim(w_in, o_u, o_gq, axis=ax),
                 lax.slice_in_dim(w_in, o_code, end, axis=ax),
                 jnp.zeros(w_in.shape[:-1] + (PROJ_WIDTH - OFF_CODE - GLA_RANK,), w_in.dtype)])
    return jnp.concatenate(pieces, axis=ax).astype(BF16)


def _layout_w_out(w_out):
    wa = w_out[:, :GMLP_WIDTH, :].astype(BF16)
    wb = w_out[:, GMLP_WIDTH:GMLP_WIDTH + DIFF_WIDTH, :].astype(BF16)
    pieces = _head_pieces(w_out, 1, GMLP_WIDTH + DIFF_WIDTH, GLA_HEADS, GLA_DV, GLA_DV_PAD)
    return wa, wb, jnp.concatenate(pieces, axis=1).astype(BF16)


def kernel(x, norm1_g, w_in, gmlp_ln_g, spatial_w, spatial_b, diff_lambdas, diff_norm_g, gla_gate_w2,
           gla_gate_b, gla_norm_g, w_out, norm2_g, w_ffn_in, w_ffn_out, final_g):
    b, t, d = x.shape
    assert b == 1 and d == D_MODEL
    xs = x.reshape(t, d)

    w2p = jnp.pad(_pad_heads(gla_gate_w2, GLA_HEADS, GLA_DK, GLA_DK_PAD),
                  [(0, 0), (0, LANES - GLA_RANK), (0, 0)]).astype(BF16)
    gbp = _pad_heads(gla_gate_b, GLA_HEADS, GLA_DK, GLA_DK_PAD)[:, None, :]
    ngp = jnp.pad(gla_norm_g, [(0, 0), (0, GLA_DV_PAD - GLA_DV)])[:, None, :]
    sb_t = jnp.swapaxes(spatial_b, 1, 2)

    col = jnp.arange(PROJ_WIDTH)
    q_scale = (DIFF_HEAD_DIM ** -0.5) * math.log2(math.e)
    col_scale = jnp.where((col >= OFF_DQ) & (col < OFF_DK), q_scale, 1.0).astype(F32)[None, :]

    wa, wb, wc = _layout_w_out(w_out)
    wf_in = w_ffn_in.astype(BF16)
    wf_out = w_ffn_out.astype(BF16)

    for l in range(DEPTH):
        lambda_init = 0.8 - 0.6 * math.exp(-0.3 * l)
        w1 = _layout_w_in(w_in[l])
        proj = _norm_matmul(xs, norm1_g[l][None, :], w1, col_scale, min(512, t), PROJ_WIDTH // 2)
        a_out = _gmlp_mixer(proj, gmlp_ln_g[l][None, :], spatial_w[l], sb_t[l], min(512, t))
        b_out = _diff_attention(proj, diff_lambdas[l], diff_norm_g[l][None, :], lambda_init, min(512, t))
        c_out = _gla_mixer(proj, w2p[l], gbp[l], ngp[l], min(256, t))
        xs, h2 = _matmul_residual_norm([a_out, b_out, c_out], [wa, wb, wc], l, xs, norm2_g[l][None, :],
                                       min(512, t), "out_proj")
        hid = _swiglu(h2, wf_in, l, min(2048, t), 512)
        xs = _matmul_residual([hid], [wf_out], l, xs, min(1024, t), 512, "ffn_out")
    out = _final_norm(xs, final_g[None, :], min(512, t))
    return out.reshape(b, t, d)
```

```python
import functools
import math

import jax
import jax.numpy as jnp
from jax import lax
from jax.experimental import pallas as pl
from jax.experimental.pallas import tpu as pltpu

F32 = jnp.float32
BF16 = jnp.bfloat16

D_MODEL = 2048
DEPTH = 4
EPS = 1e-6
LANES = 128

GMLP_WIDTH = 512
GMLP_GROUPS = 4
GMLP_CHUNK = 128
DIFF_WIDTH = 768
DIFF_HEADS = 6
DIFF_HEAD_DIM = 64
GLA_HEADS = 4
GLA_DK = 96
GLA_DV = 192
GLA_DK_PAD = 128
GLA_DV_PAD = 256
GLA_KEY_WIDTH = GLA_HEADS * GLA_DK
GLA_WIDTH = GLA_HEADS * GLA_DV
GLA_RANK = 16
GLA_TAU = 16.0
GLA_CHUNK = 64
GLA_SAFE_LOG_DECAY = 80.0
FFN_HIDDEN = 5632

OFF_GV = 0
OFF_GR = OFF_GV + GLA_HEADS * GLA_DV_PAD
OFF_GQ = OFF_GR + GLA_HEADS * GLA_DV_PAD
OFF_GK = OFF_GQ + GLA_HEADS * GLA_DK_PAD
OFF_U = OFF_GK + GLA_HEADS * GLA_DK_PAD
OFF_V = OFF_U + GMLP_WIDTH
OFF_DQ = OFF_V + GMLP_WIDTH
OFF_DK = OFF_DQ + DIFF_WIDTH
OFF_DV = OFF_DK + DIFF_WIDTH
OFF_CODE = OFF_DV + DIFF_WIDTH
PROJ_TN = 512
PROJ_WIDTH = -(-(OFF_CODE + LANES) // PROJ_TN) * PROJ_TN
MIX_C_WIDTH = GLA_HEADS * GLA_DV_PAD

VMEM_LIMIT = 56 * 1024 * 1024


def _cparams(sem):
    return pltpu.CompilerParams(dimension_semantics=sem, vmem_limit_bytes=VMEM_LIMIT)


def _rmsnorm_rows(x_ref, g_ref, h_ref, rows):
    tm = x_ref.shape[0]
    g = g_ref[...]
    for r in range(0, tm, rows):
        x = x_ref[r:r + rows, :]
        ms = jnp.mean(x * x, axis=-1, keepdims=True)
        h_ref[r:r + rows, :] = (x * lax.rsqrt(ms + EPS) * g).astype(BF16)


def _norm_mm_kernel(x_ref, g_ref, w_ref, cs_ref, o_ref, h_ref):
    @pl.when(pl.program_id(1) == 0)
    def _():
        _rmsnorm_rows(x_ref, g_ref, h_ref, 256)

    acc = jnp.dot(h_ref[...], w_ref[...], preferred_element_type=F32)
    o_ref[...] = (acc * cs_ref[...]).astype(o_ref.dtype)


def _swiglu_kernel(h_ref, wg_ref, wu_ref, o_ref):
    h = h_ref[...]
    gate = jnp.dot(h, wg_ref[...], preferred_element_type=F32)
    up = jnp.dot(h, wu_ref[...], preferred_element_type=F32)
    o_ref[...] = (gate / (1.0 + jnp.exp(-gate)) * up).astype(o_ref.dtype)


def _norm_matmul(x, g, w, col_scale, tm, tn):
    t, d = x.shape
    n = w.shape[1]
    return pl.pallas_call(
        _norm_mm_kernel,
        out_shape=jax.ShapeDtypeStruct((t, n), BF16),
        grid=(t // tm, n // tn),
        in_specs=[
            pl.BlockSpec((tm, d), lambda i, j: (i, 0)),
            pl.BlockSpec((1, d), lambda i, j: (0, 0)),
            pl.BlockSpec((d, tn), lambda i, j: (0, j)),
            pl.BlockSpec((1, tn), lambda i, j: (0, j)),
        ],
        out_specs=pl.BlockSpec((tm, tn), lambda i, j: (i, j)),
        scratch_shapes=[pltpu.VMEM((tm, d), BF16)],
        compiler_params=_cparams(("parallel", "arbitrary")),
        name="norm_in_proj",
    )(x, g, w, col_scale)


def _swiglu(h, w, layer, tm, tn):
    t, d = h.shape
    hidden = w.shape[2] // 2
    nj = hidden // tn
    return pl.pallas_call(
        _swiglu_kernel,
        out_shape=jax.ShapeDtypeStruct((t, hidden), BF16),
        grid=(t // tm, nj),
        in_specs=[
            pl.BlockSpec((tm, d), lambda i, j: (i, 0)),
            pl.BlockSpec((None, d, tn), lambda i, j: (layer, 0, j)),
            pl.BlockSpec((None, d, tn), lambda i, j: (layer, 0, j + nj)),
        ],
        out_specs=pl.BlockSpec((tm, tn), lambda i, j: (i, j)),
        compiler_params=_cparams(("parallel", "arbitrary")),
        name="ffn_in",
    )(h, w, w)


def _mm_res_kernel(*refs, n_a):
    a_refs, w_refs = refs[:n_a], refs[n_a:2 * n_a]
    r_ref, o_ref = refs[2 * n_a], refs[2 * n_a + 1]
    acc = r_ref[...]
    for a_ref, w_ref in zip(a_refs, w_refs):
        acc = acc + jnp.dot(a_ref[...], w_ref[...], preferred_element_type=F32)
    o_ref[...] = acc


def _mm_res_norm_kernel(*refs, n_a):
    a_refs, w_refs = refs[:n_a], refs[n_a:2 * n_a]
    r_ref, g_ref, o_ref, h_ref = refs[2 * n_a:2 * n_a + 4]
    acc = r_ref[...]
    for a_ref, w_ref in zip(a_refs, w_refs):
        acc = acc + jnp.dot(a_ref[...], w_ref[...], preferred_element_type=F32)
    o_ref[...] = acc
    ms = jnp.mean(acc * acc, axis=-1, keepdims=True)
    h_ref[...] = (acc * lax.rsqrt(ms + EPS) * g_ref[...]).astype(h_ref.dtype)


def _matmul_residual_norm(a_list, w, w_row_blocks, layer, res, g, tm, name):
    t, n = res.shape
    n_a = len(a_list)
    in_specs = [pl.BlockSpec((tm, a.shape[1]), lambda i: (i, 0)) for a in a_list]
    in_specs += [pl.BlockSpec((None, a.shape[1], n), lambda i, rb=rb: (layer, rb, 0))
                 for a, rb in zip(a_list, w_row_blocks)]
    in_specs += [pl.BlockSpec((tm, n), lambda i: (i, 0)), pl.BlockSpec((1, n), lambda i: (0, 0))]
    w_list = [w] * n_a
    return pl.pallas_call(
        functools.partial(_mm_res_norm_kernel, n_a=n_a),
        out_shape=(jax.ShapeDtypeStruct((t, n), F32), jax.ShapeDtypeStruct((t, n), BF16)),
        grid=(t // tm,),
        in_specs=in_specs,
        out_specs=(pl.BlockSpec((tm, n), lambda i: (i, 0)), pl.BlockSpec((tm, n), lambda i: (i, 0))),
        compiler_params=_cparams(("parallel",)),
        name=name,
    )(*a_list, *w_list, res, g)


def _matmul_residual(a_list, w_list, layer, res, tm, tn, name):
    t, n = res.shape
    n_a = len(a_list)
    in_specs = [pl.BlockSpec((tm, a.shape[1]), lambda i, j: (i, 0)) for a in a_list]
    in_specs += [pl.BlockSpec((None, w.shape[1], tn), lambda i, j: (layer, 0, j)) for w in w_list]
    in_specs += [pl.BlockSpec((tm, tn), lambda i, j: (i, j))]
    return pl.pallas_call(
        functools.partial(_mm_res_kernel, n_a=n_a),
        out_shape=jax.ShapeDtypeStruct((t, n), F32),
        grid=(t // tm, n // tn),
        in_specs=in_specs,
        out_specs=pl.BlockSpec((tm, tn), lambda i, j: (i, j)),
        compiler_params=_cparams(("parallel", "arbitrary")),
        name=name,
    )(*a_list, *w_list, res)


def _gelu_tanh(x):
    return 0.5 * x * (1.0 + jnp.tanh(math.sqrt(2.0 / math.pi) * (x + 0.044715 * (x * x * x))))


def _gmlp_kernel(u_ref, v_ref, lng_ref, w_ref, b_ref, o_ref):
    tile = u_ref.shape[0]
    gd = GMLP_WIDTH // GMLP_GROUPS
    v = _gelu_tanh(v_ref[...].astype(F32))
    mu = jnp.mean(v, axis=-1, keepdims=True)
    vc = v - mu
    var = jnp.mean(vc * vc, axis=-1, keepdims=True)
    vn = (vc * lax.rsqrt(var + EPS) * lng_ref[...]).astype(BF16)
    row = lax.broadcasted_iota(jnp.int32, (GMLP_CHUNK, GMLP_CHUNK), 0)
    col = lax.broadcasted_iota(jnp.int32, (GMLP_CHUNK, GMLP_CHUNK), 1)
    causal = col <= row
    bias = b_ref[...]
    for g in range(GMLP_GROUPS):
        wg = jnp.where(causal, w_ref[g], 0.0).astype(BF16)
        bg = bias[:, g:g + 1]
        for c in range(tile // GMLP_CHUNK):
            rs = slice(c * GMLP_CHUNK, (c + 1) * GMLP_CHUNK)
            cs = slice(g * gd, (g + 1) * gd)
            mixed = jnp.dot(wg, vn[rs, cs], preferred_element_type=F32) + bg
            u = _gelu_tanh(u_ref[rs, cs].astype(F32))
            o_ref[rs, cs] = (u * mixed).astype(o_ref.dtype)


def _gmlp_mixer(proj, ln_g, spatial_w, spatial_b_t, tile):
    t = proj.shape[0]
    return pl.pallas_call(
        _gmlp_kernel,
        out_shape=jax.ShapeDtypeStruct((t, GMLP_WIDTH), BF16),
        grid=(t // tile,),
        in_specs=[
            pl.BlockSpec((tile, GMLP_WIDTH), lambda i: (i, OFF_U // GMLP_WIDTH)),
            pl.BlockSpec((tile, GMLP_WIDTH), lambda i: (i, OFF_V // GMLP_WIDTH)),
            pl.BlockSpec((1, GMLP_WIDTH), lambda i: (0, 0)),
            pl.BlockSpec((GMLP_GROUPS, GMLP_CHUNK, GMLP_CHUNK), lambda i: (0, 0, 0)),
            pl.BlockSpec((GMLP_CHUNK, GMLP_GROUPS), lambda i: (0, 0)),
        ],
        out_specs=pl.BlockSpec((tile, GMLP_WIDTH), lambda i: (i, 0)),
        compiler_params=_cparams(("parallel",)),
        name="gmlp_mixer",
    )(proj, proj, ln_g, spatial_w, spatial_b_t)


def _diff_attn_kernel(lam_ref, q_ref, qn_ref, k_ref, v_ref, g_ref, o_ref, vt_ref, qt_ref, qtn_ref, sa_ref, sb_ref,
                      pa_ref, pb_ref, ca_ref, cb_ref, sta_ref, stb_ref, psa_ref, psb_ref, pc_ref, cc_ref, stc_ref,
                      psc_ref, m_ref, l_ref, acc_ref, *, blk, lambda_init):
    i = pl.program_id(1)
    n_blk = vt_ref.shape[0]

    def transposed_halves(q_blk, dst_ref):
        qt = q_blk.astype(F32).T
        sub = lax.broadcasted_iota(jnp.int32, (LANES, blk), 0)
        dst_ref[0] = jnp.where(sub < DIFF_HEAD_DIM, qt, 0.0).astype(BF16)
        dst_ref[1] = jnp.where(sub >= DIFF_HEAD_DIM, qt, 0.0).astype(BF16)

    @pl.when(i == 0)
    def _():
        def transpose_v(c, carry):
            r0 = pl.multiple_of(c * blk, blk)
            vt_ref[c] = v_ref[pl.ds(r0, blk), :].astype(F32).T.astype(BF16)
            return carry
        lax.fori_loop(0, n_blk, transpose_v, 0)

    transposed_halves(q_ref[...], qt_ref)

    def reset_state():
        m_ref[...] = jnp.full(m_ref.shape, -jnp.inf, F32)
        l_ref[...] = jnp.zeros(l_ref.shape, F32)
        acc_ref[...] = jnp.zeros(acc_ref.shape, F32)

    def k_block(j):
        return k_ref[pl.ds(pl.multiple_of(j * blk, blk), blk), :]

    def causal_mask(s):
        row = lax.broadcasted_iota(jnp.int32, (blk, blk), 0)
        col = lax.broadcasted_iota(jnp.int32, (blk, blk), 1)
        return jnp.where(row <= col, s, -jnp.inf)

    def probs(j, cur, prev, first=False, masked=False, qt_ref=qt_ref):
        p_ref, st_ref, lse_ref, ps_ref = cur
        kb = k_block(j)
        for h in range(2):
            s = jnp.dot(kb, qt_ref[h], preferred_element_type=F32)
            if masked:
                s = causal_mask(s)
            if first:
                stab = jnp.max(s, axis=0, keepdims=True)
            else:
                stab = jnp.maximum(prev[1][h], prev[2][h])
            p = jnp.exp2(s - stab)
            ps = jnp.sum(p, axis=0, keepdims=True)
            ps_ref[h] = ps
            p_ref[h] = p.astype(BF16)
            lse_ref[h] = stab + jnp.log2(ps)
            st_ref[h] = stab

    def fold(j, cur):
        p_ref, st_ref, _, ps_ref = cur
        vtb = vt_ref[j]
        for h in range(2):
            stab = st_ref[h]
            alpha = jnp.exp2(m_ref[h] - stab)
            l_ref[h] = alpha * l_ref[h] + ps_ref[h]
            acc_ref[h] = alpha * acc_ref[h] + jnp.dot(vtb, p_ref[h], preferred_element_type=F32)
            m_ref[h] = stab

    buf_a = (pa_ref, sta_ref, ca_ref, psa_ref)
    buf_b = (pb_ref, stb_ref, cb_ref, psb_ref)
    buf_c = (pc_ref, stc_ref, cc_ref, psc_ref)

    def fast_two_blocks(j):
        probs(j + 1, buf_b, buf_a)
        fold(j, buf_a)
        probs(j + 2, buf_a, buf_b)
        fold(j + 1, buf_b)

    reset_state()
    n_pairs = jnp.maximum(i - 1, 0) // 2

    def fast_oct(p, carry):
        for u in range(4):
            fast_two_blocks(8 * p + 2 * u)
        return carry

    def fast_pair(p, carry):
        fast_two_blocks(8 * (n_pairs // 4) + 2 * p)
        return carry

    lax.fori_loop(0, n_pairs // 4, fast_oct, 0)
    lax.fori_loop(0, n_pairs % 4, fast_pair, 0)
    j0 = 2 * n_pairs

    @pl.when(i == 0)
    def _():
        probs(0, buf_c, None, first=True, masked=True)
        fold(0, buf_c)

    @pl.when(i % 2 == 1)
    def _():
        probs(i, buf_b, buf_a, masked=True)
        fold(j0, buf_a)
        fold(i, buf_b)

    @pl.when((i % 2 == 0) & (i > 0))
    def _():
        probs(j0 + 1, buf_b, buf_a)
        fold(j0, buf_a)
        probs(i, buf_c, buf_b, masked=True)
        fold(j0 + 1, buf_b)
        fold(i, buf_c)

    def scores(j, s_ref, c_ref):
        kb = k_block(j)
        for h in range(2):
            s = jnp.dot(kb, qt_ref[h], preferred_element_type=F32)
            s_ref[h] = s
            c_ref[h] = jnp.max(s, axis=0, keepdims=True)

    def accumulate(j, s_ref, c_ref, masked):
        vtb = vt_ref[j]
        for h in range(2):
            s = s_ref[h]
            if masked:
                s = causal_mask(s)
                c = jnp.max(s, axis=0, keepdims=True)
            else:
                c = c_ref[h]
            m_prev = m_ref[h]
            m_new = jnp.maximum(m_prev, c)
            alpha = jnp.exp2(m_prev - m_new)
            p = jnp.exp2(s - m_new)
            l_ref[h] = alpha * l_ref[h] + jnp.sum(p, axis=0, keepdims=True)
            acc_ref[h] = alpha * acc_ref[h] + jnp.dot(vtb, p.astype(BF16), preferred_element_type=F32)
            m_ref[h] = m_new

    def finalize():
        lf = lam_ref[...]
        lam = (jnp.exp(jnp.sum(lf[0:1] * lf[1:2], axis=1, keepdims=True))
               - jnp.exp(jnp.sum(lf[2:3] * lf[3:4], axis=1, keepdims=True)) + lambda_init)
        o = (acc_ref[0] / l_ref[0] - lam * (acc_ref[1] / l_ref[1])).T
        ms = jnp.mean(o * o, axis=-1, keepdims=True)
        o = o * lax.rsqrt(ms + EPS) * g_ref[...] * (1.0 - lambda_init)
        o_ref[...] = o.astype(o_ref.dtype)

    finalize()
    transposed_halves(qn_ref[...], qtn_ref)
    probs(0, buf_a, None, first=True, qt_ref=qtn_ref)
    total = jnp.sum(l_ref[...]) + jnp.sum(acc_ref[...])
    overflowed = jnp.logical_not(jnp.abs(total) < jnp.inf)

    @pl.when(overflowed)
    def _():
        reset_state()
        scores(0, sa_ref, ca_ref)

        def slow_pair(p, carry):
            j = 2 * p
            scores(j + 1, sb_ref, cb_ref)
            accumulate(j, sa_ref, ca_ref, False)
            scores(j + 2, sa_ref, ca_ref)
            accumulate(j + 1, sb_ref, cb_ref, False)
            return carry

        lax.fori_loop(0, i // 2, slow_pair, 0)

        @pl.when(i % 2 == 1)
        def _():
            scores(i, sb_ref, cb_ref)
            accumulate(i - 1, sa_ref, ca_ref, False)
            accumulate(i, sb_ref, cb_ref, True)

        @pl.when(i % 2 == 0)
        def _():
            accumulate(i, sa_ref, ca_ref, True)

        finalize()


def _diff_attention(proj, lambdas, norm_g, lambda_init, blk):
    t = proj.shape[0]
    qo, ko, vo = OFF_DQ // LANES, OFF_DK // LANES, OFF_DV // LANES
    n_q = t // blk
    stat = pltpu.VMEM((2, 1, blk), F32)
    prob_buffer = [pltpu.VMEM((2, blk, blk), BF16), stat, stat, stat]
    return pl.pallas_call(
        functools.partial(_diff_attn_kernel, blk=blk, lambda_init=lambda_init),
        out_shape=jax.ShapeDtypeStruct((t, DIFF_WIDTH), BF16),
        grid=(DIFF_HEADS, t // blk),
        in_specs=[
            pl.BlockSpec((4, DIFF_HEAD_DIM), lambda h, i: (0, 0)),
            pl.BlockSpec((blk, LANES), lambda h, i: (i, qo + h)),
            pl.BlockSpec((blk, LANES), lambda h, i: (jnp.minimum(i + 1, n_q - 1), qo + h)),
            pl.BlockSpec((t, LANES), lambda h, i: (0, ko + h)),
            pl.BlockSpec((t, LANES), lambda h, i: (0, vo + h)),
            pl.BlockSpec((1, LANES), lambda h, i: (0, 0)),
        ],
        out_specs=pl.BlockSpec((blk, LANES), lambda h, i: (i, h)),
        scratch_shapes=[
            pltpu.VMEM((t // blk, LANES, blk), BF16),
            pltpu.VMEM((2, LANES, blk), BF16),
            pltpu.VMEM((2, LANES, blk), BF16),
            pltpu.VMEM((2, blk, blk), F32),
            pltpu.VMEM((2, blk, blk), F32),
            pltpu.VMEM((2, blk, blk), BF16),
            pltpu.VMEM((2, blk, blk), BF16),
            stat, stat,
            stat, stat,
            stat, stat,
            *prob_buffer,
            stat,
            stat,
            pltpu.VMEM((2, LANES, blk), F32),
        ],
        compiler_params=_cparams(("arbitrary", "arbitrary")),
        name="diff_attention",
    )(lambdas, proj, proj, proj, proj, norm_g)


def _gla_kernel(gv_ref, gr_ref, gq_ref, gk_ref, code_ref, w2_ref, gb_ref, ng_ref, o_ref, st_ref, g_ref, b_ref,
                kf_ref, st0_ref):
    tile = gq_ref.shape[0]
    c_len = GLA_CHUNK
    n_chunks = tile // c_len

    @pl.when(pl.program_id(0) == 0)
    def _():
        st_ref[...] = jnp.zeros(st_ref.shape, F32)

    z = jnp.dot(code_ref[...], w2_ref[...], preferred_element_type=F32) + gb_ref[...]
    g_ref[...] = (jnp.minimum(z, 0.0) - jnp.log1p(jnp.exp(-jnp.abs(z)))) * (1.0 / GLA_TAU)

    row = lax.broadcasted_iota(jnp.int32, (c_len, c_len), 0)
    col = lax.broadcasted_iota(jnp.int32, (c_len, c_len), 1)
    causal = col <= row
    tri = jnp.where(causal, 1.0, 0.0).astype(BF16)
    ng = ng_ref[...]

    def chunk(c, direct):
        r0 = c * c_len if isinstance(c, int) else pl.multiple_of(c * c_len, c_len)
        g = g_ref[pl.ds(r0, c_len), :]
        g_hi = g.astype(BF16)
        g_lo = (g - g_hi.astype(F32)).astype(BF16)
        b = (jnp.dot(tri, g_hi, preferred_element_type=F32)
             + jnp.dot(tri, g_lo, preferred_element_type=F32))
        b_last = b[c_len - 1:c_len, :]
        q = gq_ref[pl.ds(r0, c_len), :].astype(F32) * (GLA_DK ** -0.5)
        k = gk_ref[pl.ds(r0, c_len), :].astype(F32)
        qd = (q * jnp.exp(b)).astype(BF16)
        kl = (k * jnp.exp(b_last - b)).astype(BF16)
        e_last = jnp.exp(b_last)
        if direct:
            b_ref[...] = b
            kf_ref[...] = k
        else:
            kd = (k * jnp.exp(-b)).astype(BF16)
        for h in range(GLA_HEADS):
            ks = slice(h * GLA_DK_PAD, (h + 1) * GLA_DK_PAD)
            vs = slice(h * GLA_DV_PAD, (h + 1) * GLA_DV_PAD)
            vh = gv_ref[pl.ds(r0, c_len), vs]
            st = st_ref[h]
            if direct:
                qh, bh = q[:, ks], b[:, ks]

                def columns(grp, a):
                    g0 = pl.multiple_of(grp * 8, 8)
                    b8 = b_ref[pl.ds(g0, 8), ks]
                    k8 = kf_ref[pl.ds(g0, 8), ks]
                    for u in range(8):
                        w = qh * k8[u:u + 1] * jnp.exp(jnp.minimum(bh - b8[u:u + 1], 0.0))
                        a = jnp.where(col == g0 + u, jnp.sum(w, axis=1, keepdims=True), a)
                    return a

                a = lax.fori_loop(0, c_len // 8, columns, jnp.zeros((c_len, c_len), F32))
            else:
                a = lax.dot_general(qd[:, ks], kd[:, ks], (((1,), (1,)), ((), ())), preferred_element_type=F32)
            a = jnp.where(causal, a, 0.0).astype(BF16)
            o = jnp.dot(a, vh, preferred_element_type=F32)
            o = o + lax.dot_general(qd[:, ks], st.astype(BF16), (((1,), (1,)), ((), ())),
                                    preferred_element_type=F32)
            st_ref[h] = st * e_last[:, ks] + lax.dot_general(
                vh, kl[:, ks], (((0,), (0,)), ((), ())), preferred_element_type=F32)
            ms = jnp.sum(o * o, axis=-1, keepdims=True) * (1.0 / GLA_DV)
            on = o * lax.rsqrt(ms + EPS) * ng
            r = gr_ref[pl.ds(r0, c_len), vs].astype(F32)
            o_ref[pl.ds(r0, c_len), vs] = (on * (r / (1.0 + jnp.exp(-r)))).astype(o_ref.dtype)

    st0_ref[...] = st_ref[...]
    for c in range(n_chunks):
        chunk(c, direct=False)

    lowest = jnp.zeros((1, g_ref.shape[1]), F32)
    for c in range(n_chunks):
        lowest = jnp.minimum(lowest, jnp.sum(g_ref[c * c_len:(c + 1) * c_len, :], axis=0, keepdims=True))

    @pl.when(jnp.min(lowest) < -GLA_SAFE_LOG_DECAY)
    def _():
        st_ref[...] = st0_ref[...]

        def body(c, carry):
            chunk(c, direct=True)
            return carry
        lax.fori_loop(0, n_chunks, body, 0)


def _gla_mixer(proj, w2p, gbp, ngp, tile):
    t = proj.shape[0]
    kw = GLA_HEADS * GLA_DK_PAD
    state = pltpu.VMEM((GLA_HEADS, GLA_DV_PAD, GLA_DK_PAD), F32)
    return pl.pallas_call(
        _gla_kernel,
        out_shape=jax.ShapeDtypeStruct((t, MIX_C_WIDTH), BF16),
        grid=(t // tile,),
        in_specs=[
            pl.BlockSpec((tile, MIX_C_WIDTH), lambda i: (i, OFF_GV // MIX_C_WIDTH)),
            pl.BlockSpec((tile, MIX_C_WIDTH), lambda i: (i, OFF_GR // MIX_C_WIDTH)),
            pl.BlockSpec((tile, kw), lambda i: (i, OFF_GQ // kw)),
            pl.BlockSpec((tile, kw), lambda i: (i, OFF_GK // kw)),
            pl.BlockSpec((tile, LANES), lambda i: (i, OFF_CODE // LANES)),
            pl.BlockSpec((LANES, kw), lambda i: (0, 0)),
            pl.BlockSpec((1, kw), lambda i: (0, 0)),
            pl.BlockSpec((1, GLA_DV_PAD), lambda i: (0, 0)),
        ],
        out_specs=pl.BlockSpec((tile, MIX_C_WIDTH), lambda i: (i, 0)),
        scratch_shapes=[
            state,
            pltpu.VMEM((tile, kw), F32),
            pltpu.VMEM((GLA_CHUNK, kw), F32),
            pltpu.VMEM((GLA_CHUNK, kw), F32),
            state,
        ],
        compiler_params=_cparams(("arbitrary",)),
        name="gla_mixer",
    )(proj, proj, proj, proj, proj, w2p, gbp, ngp)


def _final_norm_kernel(x_ref, g_ref, o_ref):
    x = x_ref[...]
    ms = jnp.mean(x * x, axis=-1, keepdims=True)
    o_ref[...] = x * lax.rsqrt(ms + EPS) * g_ref[...]


def _final_norm(x, g, tm):
    t, d = x.shape
    return pl.pallas_call(
        _final_norm_kernel,
        out_shape=jax.ShapeDtypeStruct((t, d), F32),
        grid=(t // tm,),
        in_specs=[pl.BlockSpec((tm, d), lambda i: (i, 0)), pl.BlockSpec((1, d), lambda i: (0, 0))],
        out_specs=pl.BlockSpec((tm, d), lambda i: (i, 0)),
        compiler_params=_cparams(("parallel",)),
        name="final_norm",
    )(x, g)


def _pad_heads(w, heads, d, d_pad):
    lead = w.shape[:-1]
    w = w.reshape(lead + (heads, d))
    w = jnp.pad(w, [(0, 0)] * len(lead) + [(0, 0), (0, d_pad - d)])
    return w.reshape(lead + (heads * d_pad,))


def _head_pieces(w, axis, off, heads, d, d_pad):
    pieces = []
    for h in range(heads):
        sl = lax.slice_in_dim(w, off + h * d, off + (h + 1) * d, axis=axis)
        pieces.append(sl)
        zshape = list(w.shape)
        zshape[axis] = d_pad - d
        pieces.append(jnp.zeros(zshape, w.dtype))
    return pieces


def _layout_w_in(w_in):
    splits = [GMLP_WIDTH, GMLP_WIDTH, DIFF_WIDTH, DIFF_WIDTH, DIFF_WIDTH,
              GLA_KEY_WIDTH, GLA_KEY_WIDTH, GLA_WIDTH, GLA_WIDTH, GLA_RANK]
    idx = [0]
    for s in splits:
        idx.append(idx[-1] + s)
    o_u, o_v, o_dq, o_dk, o_dv, o_gq, o_gk, o_gv, o_gr, o_code, end = idx
    ax = w_in.ndim - 1
    pieces = (_head_pieces(w_in, ax, o_gv, GLA_HEADS, GLA_DV, GLA_DV_PAD)
              + _head_pieces(w_in, ax, o_gr, GLA_HEADS, GLA_DV, GLA_DV_PAD)
              + _head_pieces(w_in, ax, o_gq, GLA_HEADS, GLA_DK, GLA_DK_PAD)
              + _head_pieces(w_in, ax, o_gk, GLA_HEADS, GLA_DK, GLA_DK_PAD)
              + [lax.slice_in_dim(w_in, o_u, o_gq, axis=ax),
                 lax.slice_in_dim(w_in, o_code, end, axis=ax),
                 jnp.zeros(w_in.shape[:-1] + (PROJ_WIDTH - OFF_CODE - GLA_RANK,), w_in.dtype)])
    return jnp.concatenate(pieces, axis=ax).astype(BF16)


W_OUT_ROW_BLOCKS = (1024 // GMLP_WIDTH, 1536 // DIFF_WIDTH, 0)


def _layout_w_out(w_out):
    pieces = _head_pieces(w_out, 1, GMLP_WIDTH + DIFF_WIDTH, GLA_HEADS, GLA_DV, GLA_DV_PAD)
    pieces.append(w_out[:, :GMLP_WIDTH + DIFF_WIDTH, :])
    return jnp.concatenate(pieces, axis=1).astype(BF16)


def kernel(x, norm1_g, w_in, gmlp_ln_g, spatial_w, spatial_b, diff_lambdas, diff_norm_g, gla_gate_w2,
           gla_gate_b, gla_norm_g, w_out, norm2_g, w_ffn_in, w_ffn_out, final_g):
    b, t, d = x.shape
    assert b == 1 and d == D_MODEL
    xs = x.reshape(t, d)

    w2p = jnp.pad(_pad_heads(gla_gate_w2, GLA_HEADS, GLA_DK, GLA_DK_PAD),
                  [(0, 0), (0, LANES - GLA_RANK), (0, 0)]).astype(BF16)
    gbp = _pad_heads(gla_gate_b, GLA_HEADS, GLA_DK, GLA_DK_PAD)[:, None, :]
    ngp = jnp.pad(gla_norm_g, [(0, 0), (0, GLA_DV_PAD - GLA_DV)])[:, None, :]
    sb_t = jnp.swapaxes(spatial_b, 1, 2)

    col = jnp.arange(PROJ_WIDTH)
    q_scale = (DIFF_HEAD_DIM ** -0.5) * math.log2(math.e)
    col_scale = jnp.where((col >= OFF_DQ) & (col < OFF_DK), q_scale, 1.0).astype(F32)[None, :]

    wo = _layout_w_out(w_out)
    wf_in = w_ffn_in.astype(BF16)
    wf_out = w_ffn_out.astype(BF16)

    for l in range(DEPTH):
        lambda_init = 0.8 - 0.6 * math.exp(-0.3 * l)
        w1 = _layout_w_in(w_in[l])
        proj = _norm_matmul(xs, norm1_g[l][None, :], w1, col_scale, min(512, t), PROJ_WIDTH // 2)
        a_out = _gmlp_mixer(proj, gmlp_ln_g[l][None, :], spatial_w[l], sb_t[l], min(512, t))
        b_out = _diff_attention(proj, diff_lambdas[l], diff_norm_g[l][None, :], lambda_init, min(512, t))
        c_out = _gla_mixer(proj, w2p[l], gbp[l], ngp[l], min(512, t))
        xs, h2 = _matmul_residual_norm([a_out, b_out, c_out], wo, W_OUT_ROW_BLOCKS, l, xs, norm2_g[l][None, :],
                                       min(512, t), "out_proj")
        hid = _swiglu(h2, wf_in, l, min(2048, t), 512)
        xs = _matmul_residual([hid], [wf_out], l, xs, min(1024, t), 512, "ffn_out")
    out = _final_norm(xs, final_g[None, :], min(512, t))
    return out.reshape(b, t, d)
```

```python
import functools
import math

import jax
import jax.numpy as jnp
from jax import lax
from jax.experimental import pallas as pl
from jax.experimental.pallas import tpu as pltpu

F32 = jnp.float32
BF16 = jnp.bfloat16

D_MODEL = 2048
DEPTH = 4
EPS = 1e-6
LANES = 128

GMLP_WIDTH = 512
GMLP_GROUPS = 4
GMLP_CHUNK = 128
DIFF_WIDTH = 768
DIFF_HEADS = 6
DIFF_HEAD_DIM = 64
GLA_HEADS = 4
GLA_DK = 96
GLA_DV = 192
GLA_DK_PAD = 128
GLA_DV_PAD = 256
GLA_KEY_WIDTH = GLA_HEADS * GLA_DK
GLA_WIDTH = GLA_HEADS * GLA_DV
GLA_RANK = 16
GLA_TAU = 16.0
GLA_CHUNK = 64
GLA_SAFE_LOG_DECAY = 80.0

OFF_GV = 0
OFF_GR = OFF_GV + GLA_HEADS * GLA_DV_PAD
OFF_GQ = OFF_GR + GLA_HEADS * GLA_DV_PAD
OFF_GK = OFF_GQ + GLA_HEADS * GLA_DK_PAD
OFF_U = OFF_GK + GLA_HEADS * GLA_DK_PAD
OFF_V = OFF_U + GMLP_WIDTH
OFF_DQ = OFF_V + GMLP_WIDTH
OFF_DK = OFF_DQ + DIFF_WIDTH
OFF_DV = OFF_DK + DIFF_WIDTH
OFF_CODE = OFF_DV + DIFF_WIDTH
PROJ_TN = 512
PROJ_WIDTH = -(-(OFF_CODE + LANES) // PROJ_TN) * PROJ_TN
MIX_C_WIDTH = GLA_HEADS * GLA_DV_PAD

VMEM_LIMIT = 56 * 1024 * 1024

TM_IN_PROJ, TN_IN_PROJ = 512, PROJ_WIDTH // 2
TM_OUT_PROJ = 512
TM_FFN_IN, TN_FFN_IN = 2048, 512
TM_FFN_OUT, TN_FFN_OUT = 1024, 512
TM_FINAL_NORM = 512
TILE_GMLP = 512
TILE_GLA = 512
BLK_ATTN = 512


def _cparams(sem):
    return pltpu.CompilerParams(dimension_semantics=sem, vmem_limit_bytes=VMEM_LIMIT)


def _rmsnorm_rows(x_ref, g_ref, h_ref, rows):
    tm = x_ref.shape[0]
    g = g_ref[...]
    for r in range(0, tm, rows):
        x = x_ref[r:r + rows, :]
        ms = jnp.mean(x * x, axis=-1, keepdims=True)
        h_ref[r:r + rows, :] = (x * lax.rsqrt(ms + EPS) * g).astype(BF16)


def _norm_mm_kernel(x_ref, g_ref, w_ref, cs_ref, o_ref, h_ref):
    @pl.when(pl.program_id(1) == 0)
    def _():
        _rmsnorm_rows(x_ref, g_ref, h_ref, 256)

    acc = jnp.dot(h_ref[...], w_ref[...], preferred_element_type=F32)
    o_ref[...] = (acc * cs_ref[...]).astype(o_ref.dtype)


def _swiglu_kernel(h_ref, wg_ref, wu_ref, o_ref):
    h = h_ref[...]
    gate = jnp.dot(h, wg_ref[...].astype(BF16), preferred_element_type=F32)
    up = jnp.dot(h, wu_ref[...].astype(BF16), preferred_element_type=F32)
    o_ref[...] = (gate / (1.0 + jnp.exp(-gate)) * up).astype(o_ref.dtype)


def _norm_matmul(x, g, w, col_scale, tm, tn):
    t, d = x.shape
    n = w.shape[1]
    return pl.pallas_call(
        _norm_mm_kernel,
        out_shape=jax.ShapeDtypeStruct((t, n), BF16),
        grid=(t // tm, n // tn),
        in_specs=[
            pl.BlockSpec((tm, d), lambda i, j: (i, 0)),
            pl.BlockSpec((1, d), lambda i, j: (0, 0)),
            pl.BlockSpec((d, tn), lambda i, j: (0, j)),
            pl.BlockSpec((1, tn), lambda i, j: (0, j)),
        ],
        out_specs=pl.BlockSpec((tm, tn), lambda i, j: (i, j)),
        scratch_shapes=[pltpu.VMEM((tm, d), BF16)],
        compiler_params=_cparams(("parallel", "arbitrary")),
        name="norm_in_proj",
    )(x, g, w, col_scale)


def _swiglu(h, w, layer, tm, tn):
    t, d = h.shape
    hidden = w.shape[2] // 2
    nj = hidden // tn
    return pl.pallas_call(
        _swiglu_kernel,
        out_shape=jax.ShapeDtypeStruct((t, hidden), BF16),
        grid=(t // tm, nj),
        in_specs=[
            pl.BlockSpec((tm, d), lambda i, j: (i, 0)),
            pl.BlockSpec((None, d, tn), lambda i, j: (layer, 0, j)),
            pl.BlockSpec((None, d, tn), lambda i, j: (layer, 0, j + nj)),
        ],
        out_specs=pl.BlockSpec((tm, tn), lambda i, j: (i, j)),
        compiler_params=_cparams(("parallel", "arbitrary")),
        name="ffn_in",
    )(h, w, w)


def _mm_res_kernel(*refs, n_a):
    a_refs, w_refs = refs[:n_a], refs[n_a:2 * n_a]
    r_ref, o_ref = refs[2 * n_a], refs[2 * n_a + 1]
    acc = r_ref[...]
    for a_ref, w_ref in zip(a_refs, w_refs):
        acc = acc + jnp.dot(a_ref[...], w_ref[...], preferred_element_type=F32)
    o_ref[...] = acc


def _mm_res_norm_kernel(*refs, n_a):
    a_refs, w_refs = refs[:n_a], refs[n_a:2 * n_a]
    r_ref, g_ref, o_ref, h_ref = refs[2 * n_a:2 * n_a + 4]
    acc = r_ref[...]
    for a_ref, w_ref in zip(a_refs, w_refs):
        acc = acc + jnp.dot(a_ref[...], w_ref[...], preferred_element_type=F32)
    o_ref[...] = acc
    ms = jnp.mean(acc * acc, axis=-1, keepdims=True)
    h_ref[...] = (acc * lax.rsqrt(ms + EPS) * g_ref[...]).astype(h_ref.dtype)


def _matmul_residual_norm(a_list, w, w_row_blocks, layer, res, g, tm, name):
    t, n = res.shape
    n_a = len(a_list)
    in_specs = [pl.BlockSpec((tm, a.shape[1]), lambda i: (i, 0)) for a in a_list]
    in_specs += [pl.BlockSpec((None, a.shape[1], n), lambda i, rb=rb: (layer, rb, 0))
                 for a, rb in zip(a_list, w_row_blocks)]
    in_specs += [pl.BlockSpec((tm, n), lambda i: (i, 0)), pl.BlockSpec((1, n), lambda i: (0, 0))]
    w_list = [w] * n_a
    return pl.pallas_call(
        functools.partial(_mm_res_norm_kernel, n_a=n_a),
        out_shape=(jax.ShapeDtypeStruct((t, n), F32), jax.ShapeDtypeStruct((t, n), BF16)),
        grid=(t // tm,),
        in_specs=in_specs,
        out_specs=(pl.BlockSpec((tm, n), lambda i: (i, 0)), pl.BlockSpec((tm, n), lambda i: (i, 0))),
        compiler_params=_cparams(("parallel",)),
        name=name,
    )(*a_list, *w_list, res, g)


def _matmul_residual(a_list, w_list, layer, res, tm, tn, name):
    t, n = res.shape
    n_a = len(a_list)
    in_specs = [pl.BlockSpec((tm, a.shape[1]), lambda i, j: (i, 0)) for a in a_list]
    in_specs += [pl.BlockSpec((None, w.shape[1], tn), lambda i, j: (layer, 0, j)) for w in w_list]
    in_specs += [pl.BlockSpec((tm, tn), lambda i, j: (i, j))]
    return pl.pallas_call(
        functools.partial(_mm_res_kernel, n_a=n_a),
        out_shape=jax.ShapeDtypeStruct((t, n), F32),
        grid=(t // tm, n // tn),
        in_specs=in_specs,
        out_specs=pl.BlockSpec((tm, tn), lambda i, j: (i, j)),
        compiler_params=_cparams(("parallel", "arbitrary")),
        name=name,
    )(*a_list, *w_list, res)


def _gelu_tanh(x):
    return 0.5 * x * (1.0 + jnp.tanh(math.sqrt(2.0 / math.pi) * (x + 0.044715 * (x * x * x))))


def _gmlp_kernel(u_ref, v_ref, lng_ref, w_ref, b_ref, o_ref):
    tile = u_ref.shape[0]
    gd = GMLP_WIDTH // GMLP_GROUPS
    v = _gelu_tanh(v_ref[...].astype(F32))
    mu = jnp.mean(v, axis=-1, keepdims=True)
    vc = v - mu
    var = jnp.mean(vc * vc, axis=-1, keepdims=True)
    vn = (vc * lax.rsqrt(var + EPS) * lng_ref[...]).astype(BF16)
    row = lax.broadcasted_iota(jnp.int32, (GMLP_CHUNK, GMLP_CHUNK), 0)
    col = lax.broadcasted_iota(jnp.int32, (GMLP_CHUNK, GMLP_CHUNK), 1)
    causal = col <= row
    bias = b_ref[...]
    for g in range(GMLP_GROUPS):
        wg = jnp.where(causal, w_ref[g], 0.0).astype(BF16)
        bg = bias[:, g:g + 1]
        for c in range(tile // GMLP_CHUNK):
            rs = slice(c * GMLP_CHUNK, (c + 1) * GMLP_CHUNK)
            cs = slice(g * gd, (g + 1) * gd)
            mixed = jnp.dot(wg, vn[rs, cs], preferred_element_type=F32) + bg
            u = _gelu_tanh(u_ref[rs, cs].astype(F32))
            o_ref[rs, cs] = (u * mixed).astype(o_ref.dtype)


def _gmlp_mixer(proj, ln_g, spatial_w, spatial_b_t, tile):
    t = proj.shape[0]
    return pl.pallas_call(
        _gmlp_kernel,
        out_shape=jax.ShapeDtypeStruct((t, GMLP_WIDTH), BF16),
        grid=(t // tile,),
        in_specs=[
            pl.BlockSpec((tile, GMLP_WIDTH), lambda i: (i, OFF_U // GMLP_WIDTH)),
            pl.BlockSpec((tile, GMLP_WIDTH), lambda i: (i, OFF_V // GMLP_WIDTH)),
            pl.BlockSpec((1, GMLP_WIDTH), lambda i: (0, 0)),
            pl.BlockSpec((GMLP_GROUPS, GMLP_CHUNK, GMLP_CHUNK), lambda i: (0, 0, 0)),
            pl.BlockSpec((GMLP_CHUNK, GMLP_GROUPS), lambda i: (0, 0)),
        ],
        out_specs=pl.BlockSpec((tile, GMLP_WIDTH), lambda i: (i, 0)),
        compiler_params=_cparams(("parallel",)),
        name="gmlp_mixer",
    )(proj, proj, ln_g, spatial_w, spatial_b_t)


def _diff_attn_kernel(lam_ref, q_ref, qn_ref, k_ref, v_ref, g_ref, o_ref, vt_ref, qt_ref, qtn_ref, sa_ref, sb_ref,
                      pa_ref, pb_ref, ca_ref, cb_ref, sta_ref, stb_ref, psa_ref, psb_ref, pc_ref, cc_ref, stc_ref,
                      psc_ref, m_ref, l_ref, acc_ref, *, blk, lambda_init):
    i = pl.program_id(1)
    n_blk = vt_ref.shape[0]

    def transposed_halves(q_blk, dst_ref):
        qt = q_blk.astype(F32).T
        sub = lax.broadcasted_iota(jnp.int32, (LANES, blk), 0)
        dst_ref[0] = jnp.where(sub < DIFF_HEAD_DIM, qt, 0.0).astype(BF16)
        dst_ref[1] = jnp.where(sub >= DIFF_HEAD_DIM, qt, 0.0).astype(BF16)

    @pl.when(i == 0)
    def _():
        def transpose_v(c, carry):
            r0 = pl.multiple_of(c * blk, blk)
            vt_ref[c] = v_ref[pl.ds(r0, blk), :].astype(F32).T.astype(BF16)
            return carry
        lax.fori_loop(0, n_blk, transpose_v, 0)

    transposed_halves(q_ref[...], qt_ref)

    def reset_state():
        m_ref[...] = jnp.full(m_ref.shape, -jnp.inf, F32)
        l_ref[...] = jnp.zeros(l_ref.shape, F32)
        acc_ref[...] = jnp.zeros(acc_ref.shape, F32)

    def k_block(j):
        return k_ref[pl.ds(pl.multiple_of(j * blk, blk), blk), :]

    def causal_mask(s):
        row = lax.broadcasted_iota(jnp.int32, (blk, blk), 0)
        col = lax.broadcasted_iota(jnp.int32, (blk, blk), 1)
        return jnp.where(row <= col, s, -jnp.inf)

    def probs(j, cur, prev, first=False, masked=False, qt_ref=qt_ref):
        p_ref, st_ref, lse_ref, ps_ref = cur
        kb = k_block(j)
        for h in range(2):
            s = jnp.dot(kb, qt_ref[h], preferred_element_type=F32)
            if masked:
                s = causal_mask(s)
            if first:
                stab = jnp.max(s, axis=0, keepdims=True)
            else:
                stab = jnp.maximum(prev[1][h], prev[2][h])
            p = jnp.exp2(s - stab)
            ps = jnp.sum(p, axis=0, keepdims=True)
            ps_ref[h] = ps
            p_ref[h] = p.astype(BF16)
            lse_ref[h] = stab + jnp.log2(ps)
            st_ref[h] = stab

    def fold(j, cur):
        p_ref, st_ref, _, ps_ref = cur
        vtb = vt_ref[j]
        for h in range(2):
            stab = st_ref[h]
            alpha = jnp.exp2(m_ref[h] - stab)
            l_ref[h] = alpha * l_ref[h] + ps_ref[h]
            acc_ref[h] = alpha * acc_ref[h] + jnp.dot(vtb, p_ref[h], preferred_element_type=F32)
            m_ref[h] = stab

    buf_a = (pa_ref, sta_ref, ca_ref, psa_ref)
    buf_b = (pb_ref, stb_ref, cb_ref, psb_ref)
    buf_c = (pc_ref, stc_ref, cc_ref, psc_ref)

    def fast_two_blocks(j):
        probs(j + 1, buf_b, buf_a)
        fold(j, buf_a)
        probs(j + 2, buf_a, buf_b)
        fold(j + 1, buf_b)

    reset_state()
    n_pairs = jnp.maximum(i - 1, 0) // 2

    def fast_oct(p, carry):
        for u in range(4):
            fast_two_blocks(8 * p + 2 * u)
        return carry

    def fast_pair(p, carry):
        fast_two_blocks(8 * (n_pairs // 4) + 2 * p)
        return carry

    lax.fori_loop(0, n_pairs // 4, fast_oct, 0)
    lax.fori_loop(0, n_pairs % 4, fast_pair, 0)
    j0 = 2 * n_pairs

    @pl.when(i == 0)
    def _():
        probs(0, buf_c, None, first=True, masked=True)
        fold(0, buf_c)

    @pl.when(i % 2 == 1)
    def _():
        probs(i, buf_b, buf_a, masked=True)
        fold(j0, buf_a)
        fold(i, buf_b)

    @pl.when((i % 2 == 0) & (i > 0))
    def _():
        probs(j0 + 1, buf_b, buf_a)
        fold(j0, buf_a)
        probs(i, buf_c, buf_b, masked=True)
        fold(j0 + 1, buf_b)
        fold(i, buf_c)

    def scores(j, s_ref, c_ref):
        kb = k_block(j)
        for h in range(2):
            s = jnp.dot(kb, qt_ref[h], preferred_element_type=F32)
            s_ref[h] = s
            c_ref[h] = jnp.max(s, axis=0, keepdims=True)

    def accumulate(j, s_ref, c_ref, masked):
        vtb = vt_ref[j]
        for h in range(2):
            s = s_ref[h]
            if masked:
                s = causal_mask(s)
                c = jnp.max(s, axis=0, keepdims=True)
            else:
                c = c_ref[h]
            m_prev = m_ref[h]
            m_new = jnp.maximum(m_prev, c)
            alpha = jnp.exp2(m_prev - m_new)
            p = jnp.exp2(s - m_new)
            l_ref[h] = alpha * l_ref[h] + jnp.sum(p, axis=0, keepdims=True)
            acc_ref[h] = alpha * acc_ref[h] + jnp.dot(vtb, p.astype(BF16), preferred_element_type=F32)
            m_ref[h] = m_new

    def finalize():
        lf = lam_ref[...]
        lam = (jnp.exp(jnp.sum(lf[0:1] * lf[1:2], axis=1, keepdims=True))
               - jnp.exp(jnp.sum(lf[2:3] * lf[3:4], axis=1, keepdims=True)) + lambda_init)
        o = (acc_ref[0] / l_ref[0] - lam * (acc_ref[1] / l_ref[1])).T
        ms = jnp.mean(o * o, axis=-1, keepdims=True)
        o = o * lax.rsqrt(ms + EPS) * g_ref[...] * (1.0 - lambda_init)
        o_ref[...] = o.astype(o_ref.dtype)

    finalize()
    transposed_halves(qn_ref[...], qtn_ref)
    probs(0, buf_a, None, first=True, qt_ref=qtn_ref)
    total = jnp.sum(l_ref[...]) + jnp.sum(acc_ref[...])
    overflowed = jnp.logical_not(jnp.abs(total) < jnp.inf)

    @pl.when(overflowed)
    def _():
        reset_state()
        scores(0, sa_ref, ca_ref)

        def slow_pair(p, carry):
            j = 2 * p
            scores(j + 1, sb_ref, cb_ref)
            accumulate(j, sa_ref, ca_ref, False)
            scores(j + 2, sa_ref, ca_ref)
            accumulate(j + 1, sb_ref, cb_ref, False)
            return carry

        lax.fori_loop(0, i // 2, slow_pair, 0)

        @pl.when(i % 2 == 1)
        def _():
            scores(i, sb_ref, cb_ref)
            accumulate(i - 1, sa_ref, ca_ref, False)
            accumulate(i, sb_ref, cb_ref, True)

        @pl.when(i % 2 == 0)
        def _():
            accumulate(i, sa_ref, ca_ref, True)

        finalize()


def _diff_attention(proj, lambdas, norm_g, lambda_init, blk):
    t = proj.shape[0]
    qo, ko, vo = OFF_DQ // LANES, OFF_DK // LANES, OFF_DV // LANES
    n_q = t // blk
    stat = pltpu.VMEM((2, 1, blk), F32)
    prob_buffer = [pltpu.VMEM((2, blk, blk), BF16), stat, stat, stat]
    return pl.pallas_call(
        functools.partial(_diff_attn_kernel, blk=blk, lambda_init=lambda_init),
        out_shape=jax.ShapeDtypeStruct((t, DIFF_WIDTH), BF16),
        grid=(DIFF_HEADS, t // blk),
        in_specs=[
            pl.BlockSpec((4, DIFF_HEAD_DIM), lambda h, i: (0, 0)),
            pl.BlockSpec((blk, LANES), lambda h, i: (i, qo + h)),
            pl.BlockSpec((blk, LANES), lambda h, i: (jnp.minimum(i + 1, n_q - 1), qo + h)),
            pl.BlockSpec((t, LANES), lambda h, i: (0, ko + h)),
            pl.BlockSpec((t, LANES), lambda h, i: (0, vo + h)),
            pl.BlockSpec((1, LANES), lambda h, i: (0, 0)),
        ],
        out_specs=pl.BlockSpec((blk, LANES), lambda h, i: (i, h)),
        scratch_shapes=[
            pltpu.VMEM((t // blk, LANES, blk), BF16),
            pltpu.VMEM((2, LANES, blk), BF16),
            pltpu.VMEM((2, LANES, blk), BF16),
            pltpu.VMEM((2, blk, blk), F32),
            pltpu.VMEM((2, blk, blk), F32),
            pltpu.VMEM((2, blk, blk), BF16),
            pltpu.VMEM((2, blk, blk), BF16),
            stat, stat,
            stat, stat,
            stat, stat,
            *prob_buffer,
            stat,
            stat,
            pltpu.VMEM((2, LANES, blk), F32),
        ],
        compiler_params=_cparams(("arbitrary", "arbitrary")),
        name="diff_attention",
    )(lambdas, proj, proj, proj, proj, norm_g)


def _gla_kernel(gv_ref, gr_ref, gq_ref, gk_ref, code_ref, w2_ref, gb_ref, ng_ref, o_ref, st_ref, g_ref, b_ref,
                kf_ref, st0_ref):
    tile = gq_ref.shape[0]
    c_len = GLA_CHUNK
    n_chunks = tile // c_len

    @pl.when(pl.program_id(0) == 0)
    def _():
        st_ref[...] = jnp.zeros(st_ref.shape, F32)

    z = jnp.dot(code_ref[...], w2_ref[...], preferred_element_type=F32) + gb_ref[...]
    g_ref[...] = (jnp.minimum(z, 0.0) - jnp.log1p(jnp.exp(-jnp.abs(z)))) * (1.0 / GLA_TAU)

    row = lax.broadcasted_iota(jnp.int32, (c_len, c_len), 0)
    col = lax.broadcasted_iota(jnp.int32, (c_len, c_len), 1)
    causal = col <= row
    tri = jnp.where(causal, 1.0, 0.0).astype(BF16)
    ng = ng_ref[...]

    def chunk(c, direct):
        r0 = c * c_len if isinstance(c, int) else pl.multiple_of(c * c_len, c_len)
        g = g_ref[pl.ds(r0, c_len), :]
        g_hi = g.astype(BF16)
        g_lo = (g - g_hi.astype(F32)).astype(BF16)
        b = (jnp.dot(tri, g_hi, preferred_element_type=F32)
             + jnp.dot(tri, g_lo, preferred_element_type=F32))
        b_last = b[c_len - 1:c_len, :]
        q = gq_ref[pl.ds(r0, c_len), :].astype(F32) * (GLA_DK ** -0.5)
        k = gk_ref[pl.ds(r0, c_len), :].astype(F32)
        qd = (q * jnp.exp(b)).astype(BF16)
        kl = (k * jnp.exp(b_last - b)).astype(BF16)
        e_last = jnp.exp(b_last)
        if direct:
            b_ref[...] = b
            kf_ref[...] = k
        else:
            kd = (k * jnp.exp(-b)).astype(BF16)
        for h in range(GLA_HEADS):
            ks = slice(h * GLA_DK_PAD, (h + 1) * GLA_DK_PAD)
            vs = slice(h * GLA_DV_PAD, (h + 1) * GLA_DV_PAD)
            vh = gv_ref[pl.ds(r0, c_len), vs]
            st = st_ref[h]
            if direct:
                qh, bh = q[:, ks], b[:, ks]

                def columns(grp, a):
                    g0 = pl.multiple_of(grp * 8, 8)
                    b8 = b_ref[pl.ds(g0, 8), ks]
                    k8 = kf_ref[pl.ds(g0, 8), ks]
                    for u in range(8):
                        w = qh * k8[u:u + 1] * jnp.exp(jnp.minimum(bh - b8[u:u + 1], 0.0))
                        a = jnp.where(col == g0 + u, jnp.sum(w, axis=1, keepdims=True), a)
                    return a

                a = lax.fori_loop(0, c_len // 8, columns, jnp.zeros((c_len, c_len), F32))
            else:
                a = lax.dot_general(qd[:, ks], kd[:, ks], (((1,), (1,)), ((), ())), preferred_element_type=F32)
            a = jnp.where(causal, a, 0.0).astype(BF16)
            o = jnp.dot(a, vh, preferred_element_type=F32)
            o = o + lax.dot_general(qd[:, ks], st.astype(BF16), (((1,), (1,)), ((), ())),
                                    preferred_element_type=F32)
            st_ref[h] = st * e_last[:, ks] + lax.dot_general(
                vh, kl[:, ks], (((0,), (0,)), ((), ())), preferred_element_type=F32)
            ms = jnp.sum(o * o, axis=-1, keepdims=True) * (1.0 / GLA_DV)
            on = o * lax.rsqrt(ms + EPS) * ng
            r = gr_ref[pl.ds(r0, c_len), vs].astype(F32)
            o_ref[pl.ds(r0, c_len), vs] = (on * (r / (1.0 + jnp.exp(-r)))).astype(o_ref.dtype)

    st0_ref[...] = st_ref[...]
    for c in range(n_chunks):
        chunk(c, direct=False)

    lowest = jnp.zeros((1, g_ref.shape[1]), F32)
    for c in range(n_chunks):
        lowest = jnp.minimum(lowest, jnp.sum(g_ref[c * c_len:(c + 1) * c_len, :], axis=0, keepdims=True))

    @pl.when(jnp.min(lowest) < -GLA_SAFE_LOG_DECAY)
    def _():
        st_ref[...] = st0_ref[...]

        def body(c, carry):
            chunk(c, direct=True)
            return carry
        lax.fori_loop(0, n_chunks, body, 0)


def _gla_mixer(proj, w2p, gbp, ngp, tile):
    t = proj.shape[0]
    kw = GLA_HEADS * GLA_DK_PAD
    state = pltpu.VMEM((GLA_HEADS, GLA_DV_PAD, GLA_DK_PAD), F32)
    return pl.pallas_call(
        _gla_kernel,
        out_shape=jax.ShapeDtypeStruct((t, MIX_C_WIDTH), BF16),
        grid=(t // tile,),
        in_specs=[
            pl.BlockSpec((tile, MIX_C_WIDTH), lambda i: (i, OFF_GV // MIX_C_WIDTH)),
            pl.BlockSpec((tile, MIX_C_WIDTH), lambda i: (i, OFF_GR // MIX_C_WIDTH)),
            pl.BlockSpec((tile, kw), lambda i: (i, OFF_GQ // kw)),
            pl.BlockSpec((tile, kw), lambda i: (i, OFF_GK // kw)),
            pl.BlockSpec((tile, LANES), lambda i: (i, OFF_CODE // LANES)),
            pl.BlockSpec((LANES, kw), lambda i: (0, 0)),
            pl.BlockSpec((1, kw), lambda i: (0, 0)),
            pl.BlockSpec((1, GLA_DV_PAD), lambda i: (0, 0)),
        ],
        out_specs=pl.BlockSpec((tile, MIX_C_WIDTH), lambda i: (i, 0)),
        scratch_shapes=[
            state,
            pltpu.VMEM((tile, kw), F32),
            pltpu.VMEM((GLA_CHUNK, kw), F32),
            pltpu.VMEM((GLA_CHUNK, kw), F32),
            state,
        ],
        compiler_params=_cparams(("arbitrary",)),
        name="gla_mixer",
    )(proj, proj, proj, proj, proj, w2p, gbp, ngp)


def _final_norm_kernel(x_ref, g_ref, o_ref):
    x = x_ref[...]
    ms = jnp.mean(x * x, axis=-1, keepdims=True)
    o_ref[...] = x * lax.rsqrt(ms + EPS) * g_ref[...]


def _final_norm(x, g, tm):
    t, d = x.shape
    return pl.pallas_call(
        _final_norm_kernel,
        out_shape=jax.ShapeDtypeStruct((t, d), F32),
        grid=(t // tm,),
        in_specs=[pl.BlockSpec((tm, d), lambda i: (i, 0)), pl.BlockSpec((1, d), lambda i: (0, 0))],
        out_specs=pl.BlockSpec((tm, d), lambda i: (i, 0)),
        compiler_params=_cparams(("parallel",)),
        name="final_norm",
    )(x, g)


def _pad_heads(w, heads, d, d_pad):
    lead = w.shape[:-1]
    w = w.reshape(lead + (heads, d))
    w = jnp.pad(w, [(0, 0)] * len(lead) + [(0, 0), (0, d_pad - d)])
    return w.reshape(lead + (heads * d_pad,))


def _head_pieces(w, axis, off, heads, d, d_pad):
    pieces = []
    for h in range(heads):
        sl = lax.slice_in_dim(w, off + h * d, off + (h + 1) * d, axis=axis)
        pieces.append(sl)
        zshape = list(w.shape)
        zshape[axis] = d_pad - d
        pieces.append(jnp.zeros(zshape, w.dtype))
    return pieces


def _layout_w_in(w_in):
    splits = [GMLP_WIDTH, GMLP_WIDTH, DIFF_WIDTH, DIFF_WIDTH, DIFF_WIDTH,
              GLA_KEY_WIDTH, GLA_KEY_WIDTH, GLA_WIDTH, GLA_WIDTH, GLA_RANK]
    idx = [0]
    for s in splits:
        idx.append(idx[-1] + s)
    o_u, o_v, o_dq, o_dk, o_dv, o_gq, o_gk, o_gv, o_gr, o_code, end = idx
    ax = w_in.ndim - 1
    pieces = (_head_pieces(w_in, ax, o_gv, GLA_HEADS, GLA_DV, GLA_DV_PAD)
              + _head_pieces(w_in, ax, o_gr, GLA_HEADS, GLA_DV, GLA_DV_PAD)
              + _head_pieces(w_in, ax, o_gq, GLA_HEADS, GLA_DK, GLA_DK_PAD)
              + _head_pieces(w_in, ax, o_gk, GLA_HEADS, GLA_DK, GLA_DK_PAD)
              + [lax.slice_in_dim(w_in, o_u, o_gq, axis=ax),
                 lax.slice_in_dim(w_in, o_code, end, axis=ax),
                 jnp.zeros(w_in.shape[:-1] + (PROJ_WIDTH - OFF_CODE - GLA_RANK,), w_in.dtype)])
    return jnp.concatenate(pieces, axis=ax).astype(BF16)


W_OUT_ROW_BLOCKS = (1024 // GMLP_WIDTH, 1536 // DIFF_WIDTH, 0)


def _layout_w_out(w_out):
    pieces = _head_pieces(w_out, 1, GMLP_WIDTH + DIFF_WIDTH, GLA_HEADS, GLA_DV, GLA_DV_PAD)
    pieces.append(w_out[:, :GMLP_WIDTH + DIFF_WIDTH, :])
    return jnp.concatenate(pieces, axis=1).astype(BF16)


def kernel(x, norm1_g, w_in, gmlp_ln_g, spatial_w, spatial_b, diff_lambdas, diff_norm_g, gla_gate_w2,
           gla_gate_b, gla_norm_g, w_out, norm2_g, w_ffn_in, w_ffn_out, final_g):
    b, t, d = x.shape
    assert b == 1 and d == D_MODEL
    xs = x.reshape(t, d)

    w2p = jnp.pad(_pad_heads(gla_gate_w2, GLA_HEADS, GLA_DK, GLA_DK_PAD),
                  [(0, 0), (0, LANES - GLA_RANK), (0, 0)]).astype(BF16)
    gbp = _pad_heads(gla_gate_b, GLA_HEADS, GLA_DK, GLA_DK_PAD)[:, None, :]
    ngp = jnp.pad(gla_norm_g, [(0, 0), (0, GLA_DV_PAD - GLA_DV)])[:, None, :]
    sb_t = jnp.swapaxes(spatial_b, 1, 2)

    col = jnp.arange(PROJ_WIDTH)
    q_scale = (DIFF_HEAD_DIM ** -0.5) * math.log2(math.e)
    col_scale = jnp.where((col >= OFF_DQ) & (col < OFF_DK), q_scale, 1.0).astype(F32)[None, :]

    wo = _layout_w_out(w_out)
    wf_out = w_ffn_out.astype(BF16)

    for l in range(DEPTH):
        lambda_init = 0.8 - 0.6 * math.exp(-0.3 * l)
        w1 = _layout_w_in(w_in[l])
        proj = _norm_matmul(xs, norm1_g[l][None, :], w1, col_scale, min(TM_IN_PROJ, t), TN_IN_PROJ)
        a_out = _gmlp_mixer(proj, gmlp_ln_g[l][None, :], spatial_w[l], sb_t[l], min(TILE_GMLP, t))
        b_out = _diff_attention(proj, diff_lambdas[l], diff_norm_g[l][None, :], lambda_init, min(BLK_ATTN, t))
        c_out = _gla_mixer(proj, w2p[l], gbp[l], ngp[l], min(TILE_GLA, t))
        xs, h2 = _matmul_residual_norm([a_out, b_out, c_out], wo, W_OUT_ROW_BLOCKS, l, xs, norm2_g[l][None, :],
                                       min(TM_OUT_PROJ, t), "out_proj")
        hid = _swiglu(h2, w_ffn_in, l, min(TM_FFN_IN, t), TN_FFN_IN)
        xs = _matmul_residual([hid], [wf_out], l, xs, min(TM_FFN_OUT, t), TN_FFN_OUT, "ffn_out")
    out = _final_norm(xs, final_g[None, :], min(TM_FINAL_NORM, t))
    return out.reshape(b, t, d)
```

```python
import functools
import math

import jax
import jax.numpy as jnp
from jax import lax
from jax.experimental import pallas as pl
from jax.experimental.pallas import tpu as pltpu

F32 = jnp.float32
BF16 = jnp.bfloat16

D_MODEL = 2048
DEPTH = 4
EPS = 1e-6
LANES = 128

GMLP_WIDTH = 512
GMLP_GROUPS = 4
GMLP_CHUNK = 128
DIFF_WIDTH = 768
DIFF_HEADS = 6
DIFF_HEAD_DIM = 64
GLA_HEADS = 4
GLA_DK = 96
GLA_DV = 192
GLA_DK_PAD = 128
GLA_DV_PAD = 256
GLA_KEY_WIDTH = GLA_HEADS * GLA_DK
GLA_WIDTH = GLA_HEADS * GLA_DV
GLA_RANK = 16
GLA_TAU = 16.0
GLA_CHUNK = 64
GLA_SAFE_LOG_DECAY = 80.0

OFF_GV = 0
OFF_GR = OFF_GV + GLA_HEADS * GLA_DV_PAD
OFF_GQ = OFF_GR + GLA_HEADS * GLA_DV_PAD
OFF_GK = OFF_GQ + GLA_HEADS * GLA_DK_PAD
OFF_U = OFF_GK + GLA_HEADS * GLA_DK_PAD
OFF_V = OFF_U + GMLP_WIDTH
OFF_DQ = OFF_V + GMLP_WIDTH
OFF_DK = OFF_DQ + DIFF_WIDTH
OFF_DV = OFF_DK + DIFF_WIDTH
OFF_CODE = OFF_DV + DIFF_WIDTH
PROJ_TN = 512
PROJ_WIDTH = -(-(OFF_CODE + LANES) // PROJ_TN) * PROJ_TN
MIX_C_WIDTH = GLA_HEADS * GLA_DV_PAD

VMEM_LIMIT = 56 * 1024 * 1024

TM_IN_PROJ, TN_IN_PROJ = 512, PROJ_WIDTH // 2
TM_OUT_PROJ = 512
TM_FFN_IN, TN_FFN_IN = 2048, 512
TM_FFN_OUT, TN_FFN_OUT = 1024, 512
TM_FFN_OUT_LAST = 256
TILE_GMLP = 512
TILE_GLA = 512
BLK_ATTN = 512


def _cparams(sem):
    return pltpu.CompilerParams(dimension_semantics=sem, vmem_limit_bytes=VMEM_LIMIT)


def _rmsnorm_rows(x_ref, g_ref, h_ref, rows):
    tm = x_ref.shape[0]
    g = g_ref[...]
    for r in range(0, tm, rows):
        x = x_ref[r:r + rows, :]
        ms = jnp.mean(x * x, axis=-1, keepdims=True)
        h_ref[r:r + rows, :] = (x * lax.rsqrt(ms + EPS) * g).astype(BF16)


def _norm_mm_kernel(x_ref, g_ref, w_ref, cs_ref, o_ref, h_ref):
    @pl.when(pl.program_id(1) == 0)
    def _():
        _rmsnorm_rows(x_ref, g_ref, h_ref, 256)

    acc = jnp.dot(h_ref[...], w_ref[...], preferred_element_type=F32)
    o_ref[...] = (acc * cs_ref[...]).astype(o_ref.dtype)


def _swiglu_kernel(h_ref, wg_ref, wu_ref, o_ref):
    h = h_ref[...]
    gate = jnp.dot(h, wg_ref[...].astype(BF16), preferred_element_type=F32)
    up = jnp.dot(h, wu_ref[...].astype(BF16), preferred_element_type=F32)
    o_ref[...] = (gate / (1.0 + jnp.exp(-gate)) * up).astype(o_ref.dtype)


def _norm_matmul(x, g, w, col_scale, tm, tn):
    t, d = x.shape
    n = w.shape[1]
    return pl.pallas_call(
        _norm_mm_kernel,
        out_shape=jax.ShapeDtypeStruct((t, n), BF16),
        grid=(t // tm, n // tn),
        in_specs=[
            pl.BlockSpec((tm, d), lambda i, j: (i, 0)),
            pl.BlockSpec((1, d), lambda i, j: (0, 0)),
            pl.BlockSpec((d, tn), lambda i, j: (0, j)),
            pl.BlockSpec((1, tn), lambda i, j: (0, j)),
        ],
        out_specs=pl.BlockSpec((tm, tn), lambda i, j: (i, j)),
        scratch_shapes=[pltpu.VMEM((tm, d), BF16)],
        compiler_params=_cparams(("parallel", "arbitrary")),
        name="norm_in_proj",
    )(x, g, w, col_scale)


def _swiglu(h, w, layer, tm, tn):
    t, d = h.shape
    hidden = w.shape[2] // 2
    nj = hidden // tn
    return pl.pallas_call(
        _swiglu_kernel,
        out_shape=jax.ShapeDtypeStruct((t, hidden), BF16),
        grid=(t // tm, nj),
        in_specs=[
            pl.BlockSpec((tm, d), lambda i, j: (i, 0)),
            pl.BlockSpec((None, d, tn), lambda i, j: (layer, 0, j)),
            pl.BlockSpec((None, d, tn), lambda i, j: (layer, 0, j + nj)),
        ],
        out_specs=pl.BlockSpec((tm, tn), lambda i, j: (i, j)),
        compiler_params=_cparams(("parallel", "arbitrary")),
        name="ffn_in",
    )(h, w, w)


def _mm_res_kernel(*refs, n_a):
    a_refs, w_refs = refs[:n_a], refs[n_a:2 * n_a]
    r_ref, o_ref = refs[2 * n_a], refs[2 * n_a + 1]
    acc = r_ref[...]
    for a_ref, w_ref in zip(a_refs, w_refs):
        acc = acc + jnp.dot(a_ref[...], w_ref[...], preferred_element_type=F32)
    o_ref[...] = acc


def _mm_res_norm_kernel(*refs, n_a):
    a_refs, w_refs = refs[:n_a], refs[n_a:2 * n_a]
    r_ref, g_ref, o_ref, h_ref = refs[2 * n_a:2 * n_a + 4]
    acc = r_ref[...]
    for a_ref, w_ref in zip(a_refs, w_refs):
        acc = acc + jnp.dot(a_ref[...], w_ref[...], preferred_element_type=F32)
    o_ref[...] = acc
    ms = jnp.mean(acc * acc, axis=-1, keepdims=True)
    h_ref[...] = (acc * lax.rsqrt(ms + EPS) * g_ref[...]).astype(h_ref.dtype)


def _matmul_residual_norm(a_list, w, w_row_blocks, layer, res, g, tm, name):
    t, n = res.shape
    n_a = len(a_list)
    in_specs = [pl.BlockSpec((tm, a.shape[1]), lambda i: (i, 0)) for a in a_list]
    in_specs += [pl.BlockSpec((None, a.shape[1], n), lambda i, rb=rb: (layer, rb, 0))
                 for a, rb in zip(a_list, w_row_blocks)]
    in_specs += [pl.BlockSpec((tm, n), lambda i: (i, 0)), pl.BlockSpec((1, n), lambda i: (0, 0))]
    w_list = [w] * n_a
    return pl.pallas_call(
        functools.partial(_mm_res_norm_kernel, n_a=n_a),
        out_shape=(jax.ShapeDtypeStruct((t, n), F32), jax.ShapeDtypeStruct((t, n), BF16)),
        grid=(t // tm,),
        in_specs=in_specs,
        out_specs=(pl.BlockSpec((tm, n), lambda i: (i, 0)), pl.BlockSpec((tm, n), lambda i: (i, 0))),
        compiler_params=_cparams(("parallel",)),
        name=name,
    )(*a_list, *w_list, res, g)


def _matmul_residual(a_list, w_list, layer, res, tm, tn, name):
    t, n = res.shape
    n_a = len(a_list)
    in_specs = [pl.BlockSpec((tm, a.shape[1]), lambda i, j: (i, 0)) for a in a_list]
    in_specs += [pl.BlockSpec((None, w.shape[1], tn), lambda i, j: (layer, 0, j)) for w in w_list]
    in_specs += [pl.BlockSpec((tm, tn), lambda i, j: (i, j))]
    return pl.pallas_call(
        functools.partial(_mm_res_kernel, n_a=n_a),
        out_shape=jax.ShapeDtypeStruct((t, n), F32),
        grid=(t // tm, n // tn),
        in_specs=in_specs,
        out_specs=pl.BlockSpec((tm, tn), lambda i, j: (i, j)),
        compiler_params=_cparams(("parallel", "arbitrary")),
        name=name,
    )(*a_list, *w_list, res)


def _gelu_tanh(x):
    return 0.5 * x * (1.0 + jnp.tanh(math.sqrt(2.0 / math.pi) * (x + 0.044715 * (x * x * x))))


def _gmlp_kernel(u_ref, v_ref, lng_ref, w_ref, b_ref, o_ref):
    tile = u_ref.shape[0]
    gd = GMLP_WIDTH // GMLP_GROUPS
    v = _gelu_tanh(v_ref[...].astype(F32))
    mu = jnp.mean(v, axis=-1, keepdims=True)
    vc = v - mu
    var = jnp.mean(vc * vc, axis=-1, keepdims=True)
    vn = (vc * lax.rsqrt(var + EPS) * lng_ref[...]).astype(BF16)
    row = lax.broadcasted_iota(jnp.int32, (GMLP_CHUNK, GMLP_CHUNK), 0)
    col = lax.broadcasted_iota(jnp.int32, (GMLP_CHUNK, GMLP_CHUNK), 1)
    causal = col <= row
    bias = b_ref[...]
    for g in range(GMLP_GROUPS):
        wg = jnp.where(causal, w_ref[g], 0.0).astype(BF16)
        bg = bias[:, g:g + 1]
        for c in range(tile // GMLP_CHUNK):
            rs = slice(c * GMLP_CHUNK, (c + 1) * GMLP_CHUNK)
            cs = slice(g * gd, (g + 1) * gd)
            mixed = jnp.dot(wg, vn[rs, cs], preferred_element_type=F32) + bg
            u = _gelu_tanh(u_ref[rs, cs].astype(F32))
            o_ref[rs, cs] = (u * mixed).astype(o_ref.dtype)


def _gmlp_mixer(proj, ln_g, spatial_w, spatial_b_t, tile):
    t = proj.shape[0]
    return pl.pallas_call(
        _gmlp_kernel,
        out_shape=jax.ShapeDtypeStruct((t, GMLP_WIDTH), BF16),
        grid=(t // tile,),
        in_specs=[
            pl.BlockSpec((tile, GMLP_WIDTH), lambda i: (i, OFF_U // GMLP_WIDTH)),
            pl.BlockSpec((tile, GMLP_WIDTH), lambda i: (i, OFF_V // GMLP_WIDTH)),
            pl.BlockSpec((1, GMLP_WIDTH), lambda i: (0, 0)),
            pl.BlockSpec((GMLP_GROUPS, GMLP_CHUNK, GMLP_CHUNK), lambda i: (0, 0, 0)),
            pl.BlockSpec((GMLP_CHUNK, GMLP_GROUPS), lambda i: (0, 0)),
        ],
        out_specs=pl.BlockSpec((tile, GMLP_WIDTH), lambda i: (i, 0)),
        compiler_params=_cparams(("parallel",)),
        name="gmlp_mixer",
    )(proj, proj, ln_g, spatial_w, spatial_b_t)


def _diff_attn_kernel(lam_ref, q_ref, qn_ref, k_ref, v_ref, g_ref, o_ref, vt_ref, qt_ref, qtn_ref, sa_ref, sb_ref,
                      pa_ref, pb_ref, ca_ref, cb_ref, sta_ref, stb_ref, psa_ref, psb_ref, pc_ref, cc_ref, stc_ref,
                      psc_ref, m_ref, l_ref, acc_ref, *, blk, lambda_init):
    i = pl.program_id(1)
    n_blk = vt_ref.shape[0]

    def transposed_halves(q_blk, dst_ref):
        qt = q_blk.astype(F32).T
        sub = lax.broadcasted_iota(jnp.int32, (LANES, blk), 0)
        dst_ref[0] = jnp.where(sub < DIFF_HEAD_DIM, qt, 0.0).astype(BF16)
        dst_ref[1] = jnp.where(sub >= DIFF_HEAD_DIM, qt, 0.0).astype(BF16)

    @pl.when(i == 0)
    def _():
        def transpose_v(c, carry):
            r0 = pl.multiple_of(c * blk, blk)
            vt_ref[c] = v_ref[pl.ds(r0, blk), :].astype(F32).T.astype(BF16)
            return carry
        lax.fori_loop(0, n_blk, transpose_v, 0)

    transposed_halves(q_ref[...], qt_ref)

    def reset_state():
        m_ref[...] = jnp.full(m_ref.shape, -jnp.inf, F32)
        l_ref[...] = jnp.zeros(l_ref.shape, F32)
        acc_ref[...] = jnp.zeros(acc_ref.shape, F32)

    def k_block(j):
        return k_ref[pl.ds(pl.multiple_of(j * blk, blk), blk), :]

    def causal_mask(s):
        row = lax.broadcasted_iota(jnp.int32, (blk, blk), 0)
        col = lax.broadcasted_iota(jnp.int32, (blk, blk), 1)
        return jnp.where(row <= col, s, -jnp.inf)

    def probs(j, cur, prev, first=False, masked=False, qt_ref=qt_ref):
        p_ref, st_ref, lse_ref, ps_ref = cur
        kb = k_block(j)
        for h in range(2):
            s = jnp.dot(kb, qt_ref[h], preferred_element_type=F32)
            if masked:
                s = causal_mask(s)
            if first:
                stab = jnp.max(s, axis=0, keepdims=True)
            else:
                stab = jnp.maximum(prev[1][h], prev[2][h])
            p = jnp.exp2(s - stab)
            ps = jnp.sum(p, axis=0, keepdims=True)
            ps_ref[h] = ps
            p_ref[h] = p.astype(BF16)
            lse_ref[h] = stab + jnp.log2(ps)
            st_ref[h] = stab

    def fold(j, cur):
        p_ref, st_ref, _, ps_ref = cur
        vtb = vt_ref[j]
        for h in range(2):
            stab = st_ref[h]
            alpha = jnp.exp2(m_ref[h] - stab)
            l_ref[h] = alpha * l_ref[h] + ps_ref[h]
            acc_ref[h] = alpha * acc_ref[h] + jnp.dot(vtb, p_ref[h], preferred_element_type=F32)
            m_ref[h] = stab

    buf_a = (pa_ref, sta_ref, ca_ref, psa_ref)
    buf_b = (pb_ref, stb_ref, cb_ref, psb_ref)
    buf_c = (pc_ref, stc_ref, cc_ref, psc_ref)

    def fast_two_blocks(j):
        probs(j + 1, buf_b, buf_a)
        fold(j, buf_a)
        probs(j + 2, buf_a, buf_b)
        fold(j + 1, buf_b)

    reset_state()
    n_pairs = jnp.maximum(i - 1, 0) // 2

    def fast_oct(p, carry):
        for u in range(4):
            fast_two_blocks(8 * p + 2 * u)
        return carry

    def fast_pair(p, carry):
        fast_two_blocks(8 * (n_pairs // 4) + 2 * p)
        return carry

    lax.fori_loop(0, n_pairs // 4, fast_oct, 0)
    lax.fori_loop(0, n_pairs % 4, fast_pair, 0)
    j0 = 2 * n_pairs

    @pl.when(i == 0)
    def _():
        probs(0, buf_c, None, first=True, masked=True)
        fold(0, buf_c)

    @pl.when(i % 2 == 1)
    def _():
        probs(i, buf_b, buf_a, masked=True)
        fold(j0, buf_a)
        fold(i, buf_b)

    @pl.when((i % 2 == 0) & (i > 0))
    def _():
        probs(j0 + 1, buf_b, buf_a)
        fold(j0, buf_a)
        probs(i, buf_c, buf_b, masked=True)
        fold(j0 + 1, buf_b)
        fold(i, buf_c)

    def scores(j, s_ref, c_ref):
        kb = k_block(j)
        for h in range(2):
            s = jnp.dot(kb, qt_ref[h], preferred_element_type=F32)
            s_ref[h] = s
            c_ref[h] = jnp.max(s, axis=0, keepdims=True)

    def accumulate(j, s_ref, c_ref, masked):
        vtb = vt_ref[j]
        for h in range(2):
            s = s_ref[h]
            if masked:
                s = causal_mask(s)
                c = jnp.max(s, axis=0, keepdims=True)
            else:
                c = c_ref[h]
            m_prev = m_ref[h]
            m_new = jnp.maximum(m_prev, c)
            alpha = jnp.exp2(m_prev - m_new)
            p = jnp.exp2(s - m_new)
            l_ref[h] = alpha * l_ref[h] + jnp.sum(p, axis=0, keepdims=True)
            acc_ref[h] = alpha * acc_ref[h] + jnp.dot(vtb, p.astype(BF16), preferred_element_type=F32)
            m_ref[h] = m_new

    def finalize():
        lf = lam_ref[...]
        lam = (jnp.exp(jnp.sum(lf[0:1] * lf[1:2], axis=1, keepdims=True))
               - jnp.exp(jnp.sum(lf[2:3] * lf[3:4], axis=1, keepdims=True)) + lambda_init)
        o = (acc_ref[0] / l_ref[0] - lam * (acc_ref[1] / l_ref[1])).T
        ms = jnp.mean(o * o, axis=-1, keepdims=True)
        o = o * lax.rsqrt(ms + EPS) * g_ref[...] * (1.0 - lambda_init)
        o_ref[...] = o.astype(o_ref.dtype)

    finalize()
    transposed_halves(qn_ref[...], qtn_ref)
    probs(0, buf_a, None, first=True, qt_ref=qtn_ref)
    total = jnp.sum(l_ref[...]) + jnp.sum(acc_ref[...])
    overflowed = jnp.logical_not(jnp.abs(total) < jnp.inf)

    @pl.when(overflowed)
    def _():
        reset_state()
        scores(0, sa_ref, ca_ref)

        def slow_pair(p, carry):
            j = 2 * p
            scores(j + 1, sb_ref, cb_ref)
            accumulate(j, sa_ref, ca_ref, False)
            scores(j + 2, sa_ref, ca_ref)
            accumulate(j + 1, sb_ref, cb_ref, False)
            return carry

        lax.fori_loop(0, i // 2, slow_pair, 0)

        @pl.when(i % 2 == 1)
        def _():
            scores(i, sb_ref, cb_ref)
            accumulate(i - 1, sa_ref, ca_ref, False)
            accumulate(i, sb_ref, cb_ref, True)

        @pl.when(i % 2 == 0)
        def _():
            accumulate(i, sa_ref, ca_ref, True)

        finalize()


def _diff_attention(proj, lambdas, norm_g, lambda_init, blk):
    t = proj.shape[0]
    qo, ko, vo = OFF_DQ // LANES, OFF_DK // LANES, OFF_DV // LANES
    n_q = t // blk
    stat = pltpu.VMEM((2, 1, blk), F32)
    prob_buffer = [pltpu.VMEM((2, blk, blk), BF16), stat, stat, stat]
    return pl.pallas_call(
        functools.partial(_diff_attn_kernel, blk=blk, lambda_init=lambda_init),
        out_shape=jax.ShapeDtypeStruct((t, DIFF_WIDTH), BF16),
        grid=(DIFF_HEADS, t // blk),
        in_specs=[
            pl.BlockSpec((4, DIFF_HEAD_DIM), lambda h, i: (0, 0)),
            pl.BlockSpec((blk, LANES), lambda h, i: (i, qo + h)),
            pl.BlockSpec((blk, LANES), lambda h, i: (jnp.minimum(i + 1, n_q - 1), qo + h)),
            pl.BlockSpec((t, LANES), lambda h, i: (0, ko + h)),
            pl.BlockSpec((t, LANES), lambda h, i: (0, vo + h)),
            pl.BlockSpec((1, LANES), lambda h, i: (0, 0)),
        ],
        out_specs=pl.BlockSpec((blk, LANES), lambda h, i: (i, h)),
        scratch_shapes=[
            pltpu.VMEM((t // blk, LANES, blk), BF16),
            pltpu.VMEM((2, LANES, blk), BF16),
            pltpu.VMEM((2, LANES, blk), BF16),
            pltpu.VMEM((2, blk, blk), F32),
            pltpu.VMEM((2, blk, blk), F32),
            pltpu.VMEM((2, blk, blk), BF16),
            pltpu.VMEM((2, blk, blk), BF16),
            stat, stat,
            stat, stat,
            stat, stat,
            *prob_buffer,
            stat,
            stat,
            pltpu.VMEM((2, LANES, blk), F32),
        ],
        compiler_params=_cparams(("arbitrary", "arbitrary")),
        name="diff_attention",
    )(lambdas, proj, proj, proj, proj, norm_g)


def _gla_kernel(gv_ref, gr_ref, gq_ref, gk_ref, code_ref, w2_ref, gb_ref, ng_ref, o_ref, st_ref, g_ref, b_ref,
                kf_ref, st0_ref):
    tile = gq_ref.shape[0]
    c_len = GLA_CHUNK
    n_chunks = tile // c_len

    @pl.when(pl.program_id(0) == 0)
    def _():
        st_ref[...] = jnp.zeros(st_ref.shape, F32)

    z = jnp.dot(code_ref[...], w2_ref[...], preferred_element_type=F32) + gb_ref[...]
    g_ref[...] = (jnp.minimum(z, 0.0) - jnp.log1p(jnp.exp(-jnp.abs(z)))) * (1.0 / GLA_TAU)

    row = lax.broadcasted_iota(jnp.int32, (c_len, c_len), 0)
    col = lax.broadcasted_iota(jnp.int32, (c_len, c_len), 1)
    causal = col <= row
    tri = jnp.where(causal, 1.0, 0.0).astype(BF16)
    ng = ng_ref[...]

    def chunk(c, direct):
        r0 = c * c_len if isinstance(c, int) else pl.multiple_of(c * c_len, c_len)
        g = g_ref[pl.ds(r0, c_len), :]
        g_hi = g.astype(BF16)
        g_lo = (g - g_hi.astype(F32)).astype(BF16)
        b = (jnp.dot(tri, g_hi, preferred_element_type=F32)
             + jnp.dot(tri, g_lo, preferred_element_type=F32))
        b_last = b[c_len - 1:c_len, :]
        q = gq_ref[pl.ds(r0, c_len), :].astype(F32) * (GLA_DK ** -0.5)
        k = gk_ref[pl.ds(r0, c_len), :].astype(F32)
        qd = (q * jnp.exp(b)).astype(BF16)
        kl = (k * jnp.exp(b_last - b)).astype(BF16)
        e_last = jnp.exp(b_last)
        if direct:
            b_ref[...] = b
            kf_ref[...] = k
        else:
            kd = (k * jnp.exp(-b)).astype(BF16)
        for h in range(GLA_HEADS):
            ks = slice(h * GLA_DK_PAD, (h + 1) * GLA_DK_PAD)
            vs = slice(h * GLA_DV_PAD, (h + 1) * GLA_DV_PAD)
            vh = gv_ref[pl.ds(r0, c_len), vs]
            st = st_ref[h]
            if direct:
                qh, bh = q[:, ks], b[:, ks]

                def columns(grp, a):
                    g0 = pl.multiple_of(grp * 8, 8)
                    b8 = b_ref[pl.ds(g0, 8), ks]
                    k8 = kf_ref[pl.ds(g0, 8), ks]
                    for u in range(8):
                        w = qh * k8[u:u + 1] * jnp.exp(jnp.minimum(bh - b8[u:u + 1], 0.0))
                        a = jnp.where(col == g0 + u, jnp.sum(w, axis=1, keepdims=True), a)
                    return a

                a = lax.fori_loop(0, c_len // 8, columns, jnp.zeros((c_len, c_len), F32))
            else:
                a = lax.dot_general(qd[:, ks], kd[:, ks], (((1,), (1,)), ((), ())), preferred_element_type=F32)
            a = jnp.where(causal, a, 0.0).astype(BF16)
            o = jnp.dot(a, vh, preferred_element_type=F32)
            o = o + lax.dot_general(qd[:, ks], st.astype(BF16), (((1,), (1,)), ((), ())),
                                    preferred_element_type=F32)
            st_ref[h] = st * e_last[:, ks] + lax.dot_general(
                vh, kl[:, ks], (((0,), (0,)), ((), ())), preferred_element_type=F32)
            ms = jnp.sum(o * o, axis=-1, keepdims=True) * (1.0 / GLA_DV)
            on = o * lax.rsqrt(ms + EPS) * ng
            r = gr_ref[pl.ds(r0, c_len), vs].astype(F32)
            o_ref[pl.ds(r0, c_len), vs] = (on * (r / (1.0 + jnp.exp(-r)))).astype(o_ref.dtype)

    st0_ref[...] = st_ref[...]
    for c in range(n_chunks):
        chunk(c, direct=False)

    lowest = jnp.zeros((1, g_ref.shape[1]), F32)
    for c in range(n_chunks):
        lowest = jnp.minimum(lowest, jnp.sum(g_ref[c * c_len:(c + 1) * c_len, :], axis=0, keepdims=True))

    @pl.when(jnp.min(lowest) < -GLA_SAFE_LOG_DECAY)
    def _():
        st_ref[...] = st0_ref[...]

        def body(c, carry):
            chunk(c, direct=True)
            return carry
        lax.fori_loop(0, n_chunks, body, 0)


def _gla_mixer(proj, w2p, gbp, ngp, tile):
    t = proj.shape[0]
    kw = GLA_HEADS * GLA_DK_PAD
    state = pltpu.VMEM((GLA_HEADS, GLA_DV_PAD, GLA_DK_PAD), F32)
    return pl.pallas_call(
        _gla_kernel,
        out_shape=jax.ShapeDtypeStruct((t, MIX_C_WIDTH), BF16),
        grid=(t // tile,),
        in_specs=[
            pl.BlockSpec((tile, MIX_C_WIDTH), lambda i: (i, OFF_GV // MIX_C_WIDTH)),
            pl.BlockSpec((tile, MIX_C_WIDTH), lambda i: (i, OFF_GR // MIX_C_WIDTH)),
            pl.BlockSpec((tile, kw), lambda i: (i, OFF_GQ // kw)),
            pl.BlockSpec((tile, kw), lambda i: (i, OFF_GK // kw)),
            pl.BlockSpec((tile, LANES), lambda i: (i, OFF_CODE // LANES)),
            pl.BlockSpec((LANES, kw), lambda i: (0, 0)),
            pl.BlockSpec((1, kw), lambda i: (0, 0)),
            pl.BlockSpec((1, GLA_DV_PAD), lambda i: (0, 0)),
        ],
        out_specs=pl.BlockSpec((tile, MIX_C_WIDTH), lambda i: (i, 0)),
        scratch_shapes=[
            state,
            pltpu.VMEM((tile, kw), F32),
            pltpu.VMEM((GLA_CHUNK, kw), F32),
            pltpu.VMEM((GLA_CHUNK, kw), F32),
            state,
        ],
        compiler_params=_cparams(("arbitrary",)),
        name="gla_mixer",
    )(proj, proj, proj, proj, proj, w2p, gbp, ngp)


def _mm_res_final_kernel(a_ref, w_ref, r_ref, g_ref, o_ref):
    acc = r_ref[...] + jnp.dot(a_ref[...], w_ref[...], preferred_element_type=F32)
    ms = jnp.mean(acc * acc, axis=-1, keepdims=True)
    o_ref[...] = acc * lax.rsqrt(ms + EPS) * g_ref[...]


def _matmul_residual_final_norm(a, w, layer, res, g, tm):
    t, n = res.shape
    k = a.shape[1]
    return pl.pallas_call(
        _mm_res_final_kernel,
        out_shape=jax.ShapeDtypeStruct((t, n), F32),
        grid=(t // tm,),
        in_specs=[
            pl.BlockSpec((tm, k), lambda i: (i, 0)),
            pl.BlockSpec((None, k, n), lambda i: (layer, 0, 0), pipeline_mode=pl.Buffered(1)),
            pl.BlockSpec((tm, n), lambda i: (i, 0)),
            pl.BlockSpec((1, n), lambda i: (0, 0)),
        ],
        out_specs=pl.BlockSpec((tm, n), lambda i: (i, 0)),
        compiler_params=_cparams(("parallel",)),
        name="ffn_out_final",
    )(a, w, res, g)


def _pad_heads(w, heads, d, d_pad):
    lead = w.shape[:-1]
    w = w.reshape(lead + (heads, d))
    w = jnp.pad(w, [(0, 0)] * len(lead) + [(0, 0), (0, d_pad - d)])
    return w.reshape(lead + (heads * d_pad,))


def _head_pieces(w, axis, off, heads, d, d_pad):
    pieces = []
    for h in range(heads):
        sl = lax.slice_in_dim(w, off + h * d, off + (h + 1) * d, axis=axis)
        pieces.append(sl)
        zshape = list(w.shape)
        zshape[axis] = d_pad - d
        pieces.append(jnp.zeros(zshape, w.dtype))
    return pieces


def _layout_w_in(w_in):
    splits = [GMLP_WIDTH, GMLP_WIDTH, DIFF_WIDTH, DIFF_WIDTH, DIFF_WIDTH,
              GLA_KEY_WIDTH, GLA_KEY_WIDTH, GLA_WIDTH, GLA_WIDTH, GLA_RANK]
    idx = [0]
    for s in splits:
        idx.append(idx[-1] + s)
    o_u, o_v, o_dq, o_dk, o_dv, o_gq, o_gk, o_gv, o_gr, o_code, end = idx
    ax = w_in.ndim - 1
    pieces = (_head_pieces(w_in, ax, o_gv, GLA_HEADS, GLA_DV, GLA_DV_PAD)
              + _head_pieces(w_in, ax, o_gr, GLA_HEADS, GLA_DV, GLA_DV_PAD)
              + _head_pieces(w_in, ax, o_gq, GLA_HEADS, GLA_DK, GLA_DK_PAD)
              + _head_pieces(w_in, ax, o_gk, GLA_HEADS, GLA_DK, GLA_DK_PAD)
              + [lax.slice_in_dim(w_in, o_u, o_gq, axis=ax),
                 lax.slice_in_dim(w_in, o_code, end, axis=ax),
                 jnp.zeros(w_in.shape[:-1] + (PROJ_WIDTH - OFF_CODE - GLA_RANK,), w_in.dtype)])
    return jnp.concatenate(pieces, axis=ax).astype(BF16)


W_OUT_ROW_BLOCKS = (1024 // GMLP_WIDTH, 1536 // DIFF_WIDTH, 0)


def _layout_w_out(w_out):
    pieces = _head_pieces(w_out, 1, GMLP_WIDTH + DIFF_WIDTH, GLA_HEADS, GLA_DV, GLA_DV_PAD)
    pieces.append(w_out[:, :GMLP_WIDTH + DIFF_WIDTH, :])
    return jnp.concatenate(pieces, axis=1).astype(BF16)


def kernel(x, norm1_g, w_in, gmlp_ln_g, spatial_w, spatial_b, diff_lambdas, diff_norm_g, gla_gate_w2,
           gla_gate_b, gla_norm_g, w_out, norm2_g, w_ffn_in, w_ffn_out, final_g):
    b, t, d = x.shape
    assert b == 1 and d == D_MODEL
    xs = x.reshape(t, d)

    w2p = jnp.pad(_pad_heads(gla_gate_w2, GLA_HEADS, GLA_DK, GLA_DK_PAD),
                  [(0, 0), (0, LANES - GLA_RANK), (0, 0)]).astype(BF16)
    gbp = _pad_heads(gla_gate_b, GLA_HEADS, GLA_DK, GLA_DK_PAD)[:, None, :]
    ngp = jnp.pad(gla_norm_g, [(0, 0), (0, GLA_DV_PAD - GLA_DV)])[:, None, :]
    sb_t = jnp.swapaxes(spatial_b, 1, 2)

    col = jnp.arange(PROJ_WIDTH)
    q_scale = (DIFF_HEAD_DIM ** -0.5) * math.log2(math.e)
    col_scale = jnp.where((col >= OFF_DQ) & (col < OFF_DK), q_scale, 1.0).astype(F32)[None, :]

    wo = _layout_w_out(w_out)
    wf_out = w_ffn_out.astype(BF16)

    for l in range(DEPTH):
        lambda_init = 0.8 - 0.6 * math.exp(-0.3 * l)
        w1 = _layout_w_in(w_in[l])
        proj = _norm_matmul(xs, norm1_g[l][None, :], w1, col_scale, min(TM_IN_PROJ, t), TN_IN_PROJ)
        a_out = _gmlp_mixer(proj, gmlp_ln_g[l][None, :], spatial_w[l], sb_t[l], min(TILE_GMLP, t))
        b_out = _diff_attention(proj, diff_lambdas[l], diff_norm_g[l][None, :], lambda_init, min(BLK_ATTN, t))
        c_out = _gla_mixer(proj, w2p[l], gbp[l], ngp[l], min(TILE_GLA, t))
        xs, h2 = _matmul_residual_norm([a_out, b_out, c_out], wo, W_OUT_ROW_BLOCKS, l, xs, norm2_g[l][None, :],
                                       min(TM_OUT_PROJ, t), "out_proj")
        hid = _swiglu(h2, w_ffn_in, l, min(TM_FFN_IN, t), TN_FFN_IN)
        if l < DEPTH - 1:
            xs = _matmul_residual([hid], [wf_out], l, xs, min(TM_FFN_OUT, t), TN_FFN_OUT, "ffn_out")
        else:
            out = _matmul_residual_final_norm(hid, wf_out, l, xs, final_g[None, :], min(TM_FFN_OUT_LAST, t))
    return out.reshape(b, t, d)
```

```python
import functools
import math

import jax
import jax.numpy as jnp
from jax import lax
from jax.experimental import pallas as pl
from jax.experimental.pallas import tpu as pltpu

F32 = jnp.float32
BF16 = jnp.bfloat16

D_MODEL = 2048
DEPTH = 4
EPS = 1e-6
LANES = 128

GMLP_WIDTH = 512
GMLP_GROUPS = 4
GMLP_CHUNK = 128
DIFF_WIDTH = 768
DIFF_HEADS = 6
DIFF_HEAD_DIM = 64
GLA_HEADS = 4
GLA_DK = 96
GLA_DV = 192
GLA_DK_PAD = 128
GLA_DV_PAD = 256
GLA_KEY_WIDTH = GLA_HEADS * GLA_DK
GLA_WIDTH = GLA_HEADS * GLA_DV
GLA_RANK = 16
GLA_TAU = 16.0
GLA_CHUNK = 64
GLA_SAFE_LOG_DECAY = 80.0

OFF_GV = 0
OFF_GR = OFF_GV + GLA_HEADS * GLA_DV_PAD
OFF_GQ = OFF_GR + GLA_HEADS * GLA_DV_PAD
OFF_GK = OFF_GQ + GLA_HEADS * GLA_DK_PAD
OFF_U = OFF_GK + GLA_HEADS * GLA_DK_PAD
OFF_V = OFF_U + GMLP_WIDTH
OFF_DQ = OFF_V + GMLP_WIDTH
OFF_DK = OFF_DQ + DIFF_WIDTH
OFF_DV = OFF_DK + DIFF_WIDTH
OFF_CODE = OFF_DV + DIFF_WIDTH
PROJ_TN = 512
PROJ_WIDTH = -(-(OFF_CODE + LANES) // PROJ_TN) * PROJ_TN
MIX_C_WIDTH = GLA_HEADS * GLA_DV_PAD

VMEM_LIMIT = 56 * 1024 * 1024

TM_IN_PROJ, TN_IN_PROJ = 512, PROJ_WIDTH // 2
TM_OUT_PROJ = 512
TM_FFN_IN, TN_FFN_IN = 2048, 512
TM_FFN_OUT = 256
TILE_GMLP = 512
TILE_GLA = 512
BLK_ATTN = 512


def _cparams(sem):
    return pltpu.CompilerParams(dimension_semantics=sem, vmem_limit_bytes=VMEM_LIMIT)


def _rmsnorm_rows(x_ref, g_ref, h_ref, rows):
    tm = x_ref.shape[0]
    g = g_ref[...]
    for r in range(0, tm, rows):
        x = x_ref[r:r + rows, :]
        ms = jnp.mean(x * x, axis=-1, keepdims=True)
        h_ref[r:r + rows, :] = (x * lax.rsqrt(ms + EPS) * g).astype(BF16)


def _norm_mm_kernel(x_ref, g_ref, w_ref, cs_ref, o_ref, h_ref):
    @pl.when(pl.program_id(1) == 0)
    def _():
        _rmsnorm_rows(x_ref, g_ref, h_ref, 256)

    acc = jnp.dot(h_ref[...], w_ref[...], preferred_element_type=F32)
    o_ref[...] = (acc * cs_ref[...]).astype(o_ref.dtype)


def _swiglu_kernel(h_ref, wg_ref, wu_ref, o_ref):
    h = h_ref[...]
    gate = jnp.dot(h, wg_ref[...].astype(BF16), preferred_element_type=F32)
    up = jnp.dot(h, wu_ref[...].astype(BF16), preferred_element_type=F32)
    o_ref[...] = (gate / (1.0 + jnp.exp(-gate)) * up).astype(o_ref.dtype)


def _norm_matmul(x, g, w, col_scale, tm, tn):
    t, d = x.shape
    n = w.shape[1]
    return pl.pallas_call(
        _norm_mm_kernel,
        out_shape=jax.ShapeDtypeStruct((t, n), BF16),
        grid=(t // tm, n // tn),
        in_specs=[
            pl.BlockSpec((tm, d), lambda i, j: (i, 0)),
            pl.BlockSpec((1, d), lambda i, j: (0, 0)),
            pl.BlockSpec((d, tn), lambda i, j: (0, j)),
            pl.BlockSpec((1, tn), lambda i, j: (0, j)),
        ],
        out_specs=pl.BlockSpec((tm, tn), lambda i, j: (i, j)),
        scratch_shapes=[pltpu.VMEM((tm, d), BF16)],
        compiler_params=_cparams(("parallel", "arbitrary")),
        name="norm_in_proj",
    )(x, g, w, col_scale)


def _swiglu(h, w, layer, tm, tn):
    t, d = h.shape
    hidden = w.shape[2] // 2
    nj = hidden // tn
    return pl.pallas_call(
        _swiglu_kernel,
        out_shape=jax.ShapeDtypeStruct((t, hidden), BF16),
        grid=(t // tm, nj),
        in_specs=[
            pl.BlockSpec((tm, d), lambda i, j: (i, 0)),
            pl.BlockSpec((None, d, tn), lambda i, j: (layer, 0, j)),
            pl.BlockSpec((None, d, tn), lambda i, j: (layer, 0, j + nj)),
        ],
        out_specs=pl.BlockSpec((tm, tn), lambda i, j: (i, j)),
        compiler_params=_cparams(("parallel", "arbitrary")),
        name="ffn_in",
    )(h, w, w)


def _mm_res_norm_kernel(*refs, n_a):
    a_refs, w_refs = refs[:n_a], refs[n_a:2 * n_a]
    r_ref, g_ref, o_ref, h_ref = refs[2 * n_a:2 * n_a + 4]
    acc = r_ref[...]
    for a_ref, w_ref in zip(a_refs, w_refs):
        acc = acc + jnp.dot(a_ref[...], w_ref[...], preferred_element_type=F32)
    o_ref[...] = acc
    ms = jnp.mean(acc * acc, axis=-1, keepdims=True)
    h_ref[...] = (acc * lax.rsqrt(ms + EPS) * g_ref[...]).astype(h_ref.dtype)


def _matmul_residual_norm(a_list, w, w_row_blocks, layer, res, g, tm, name):
    t, n = res.shape
    n_a = len(a_list)
    in_specs = [pl.BlockSpec((tm, a.shape[1]), lambda i: (i, 0)) for a in a_list]
    in_specs += [pl.BlockSpec((None, a.shape[1], n), lambda i, rb=rb: (layer, rb, 0))
                 for a, rb in zip(a_list, w_row_blocks)]
    in_specs += [pl.BlockSpec((tm, n), lambda i: (i, 0)), pl.BlockSpec((1, n), lambda i: (0, 0))]
    w_list = [w] * n_a
    return pl.pallas_call(
        functools.partial(_mm_res_norm_kernel, n_a=n_a),
        out_shape=(jax.ShapeDtypeStruct((t, n), F32), jax.ShapeDtypeStruct((t, n), BF16)),
        grid=(t // tm,),
        in_specs=in_specs,
        out_specs=(pl.BlockSpec((tm, n), lambda i: (i, 0)), pl.BlockSpec((tm, n), lambda i: (i, 0))),
        compiler_params=_cparams(("parallel",)),
        name=name,
    )(*a_list, *w_list, res, g)


def _mm_res_kernel(a_ref, w_ref, r_ref, g_ref, o_ref, *, final_norm):
    acc = r_ref[...] + jnp.dot(a_ref[...], w_ref[...], preferred_element_type=F32)
    if final_norm:
        ms = jnp.mean(acc * acc, axis=-1, keepdims=True)
        acc = acc * lax.rsqrt(ms + EPS) * g_ref[...]
    o_ref[...] = acc


def _matmul_residual(a, w, layer, res, g, tm, final_norm):
    t, n = res.shape
    k = a.shape[1]
    return pl.pallas_call(
        functools.partial(_mm_res_kernel, final_norm=final_norm),
        out_shape=jax.ShapeDtypeStruct((t, n), F32),
        grid=(t // tm,),
        in_specs=[
            pl.BlockSpec((tm, k), lambda i: (i, 0)),
            pl.BlockSpec((None, k, n), lambda i: (layer, 0, 0), pipeline_mode=pl.Buffered(1)),
            pl.BlockSpec((tm, n), lambda i: (i, 0)),
            pl.BlockSpec((1, n), lambda i: (0, 0)),
        ],
        out_specs=pl.BlockSpec((tm, n), lambda i: (i, 0)),
        compiler_params=_cparams(("parallel",)),
        name="ffn_out",
    )(a, w, res, g)


def _gelu_tanh(x):
    return 0.5 * x * (1.0 + jnp.tanh(math.sqrt(2.0 / math.pi) * (x + 0.044715 * (x * x * x))))


def _gmlp_kernel(u_ref, v_ref, lng_ref, w_ref, b_ref, o_ref):
    tile = u_ref.shape[0]
    gd = GMLP_WIDTH // GMLP_GROUPS
    v = _gelu_tanh(v_ref[...].astype(F32))
    mu = jnp.mean(v, axis=-1, keepdims=True)
    vc = v - mu
    var = jnp.mean(vc * vc, axis=-1, keepdims=True)
    vn = (vc * lax.rsqrt(var + EPS) * lng_ref[...]).astype(BF16)
    row = lax.broadcasted_iota(jnp.int32, (GMLP_CHUNK, GMLP_CHUNK), 0)
    col = lax.broadcasted_iota(jnp.int32, (GMLP_CHUNK, GMLP_CHUNK), 1)
    causal = col <= row
    bias = b_ref[...]
    for g in range(GMLP_GROUPS):
        wg = jnp.where(causal, w_ref[g], 0.0).astype(BF16)
        bg = bias[:, g:g + 1]
        for c in range(tile // GMLP_CHUNK):
            rs = slice(c * GMLP_CHUNK, (c + 1) * GMLP_CHUNK)
            cs = slice(g * gd, (g + 1) * gd)
            mixed = jnp.dot(wg, vn[rs, cs], preferred_element_type=F32) + bg
            u = _gelu_tanh(u_ref[rs, cs].astype(F32))
            o_ref[rs, cs] = (u * mixed).astype(o_ref.dtype)


def _gmlp_mixer(proj, ln_g, spatial_w, spatial_b_t, tile):
    t = proj.shape[0]
    return pl.pallas_call(
        _gmlp_kernel,
        out_shape=jax.ShapeDtypeStruct((t, GMLP_WIDTH), BF16),
        grid=(t // tile,),
        in_specs=[
            pl.BlockSpec((tile, GMLP_WIDTH), lambda i: (i, OFF_U // GMLP_WIDTH)),
            pl.BlockSpec((tile, GMLP_WIDTH), lambda i: (i, OFF_V // GMLP_WIDTH)),
            pl.BlockSpec((1, GMLP_WIDTH), lambda i: (0, 0)),
            pl.BlockSpec((GMLP_GROUPS, GMLP_CHUNK, GMLP_CHUNK), lambda i: (0, 0, 0)),
            pl.BlockSpec((GMLP_CHUNK, GMLP_GROUPS), lambda i: (0, 0)),
        ],
        out_specs=pl.BlockSpec((tile, GMLP_WIDTH), lambda i: (i, 0)),
        compiler_params=_cparams(("parallel",)),
        name="gmlp_mixer",
    )(proj, proj, ln_g, spatial_w, spatial_b_t)


def _diff_attn_kernel(lam_ref, q_ref, qn_ref, k_ref, v_ref, g_ref, o_ref, vt_ref, qt_ref, qtn_ref, sa_ref, sb_ref,
                      pa_ref, pb_ref, ca_ref, cb_ref, sta_ref, stb_ref, psa_ref, psb_ref, pc_ref, cc_ref, stc_ref,
                      psc_ref, m_ref, l_ref, acc_ref, *, blk, lambda_init):
    i = pl.program_id(1)
    n_blk = vt_ref.shape[0]

    def transposed_halves(q_blk, dst_ref):
        qt = q_blk.astype(F32).T
        sub = lax.broadcasted_iota(jnp.int32, (LANES, blk), 0)
        dst_ref[0] = jnp.where(sub < DIFF_HEAD_DIM, qt, 0.0).astype(BF16)
        dst_ref[1] = jnp.where(sub >= DIFF_HEAD_DIM, qt, 0.0).astype(BF16)

    @pl.when(i == 0)
    def _():
        def transpose_v(c, carry):
            r0 = pl.multiple_of(c * blk, blk)
            vt_ref[c] = v_ref[pl.ds(r0, blk), :].astype(F32).T.astype(BF16)
            return carry
        lax.fori_loop(0, n_blk, transpose_v, 0)

    transposed_halves(q_ref[...], qt_ref)

    def reset_state():
        m_ref[...] = jnp.full(m_ref.shape, -jnp.inf, F32)
        l_ref[...] = jnp.zeros(l_ref.shape, F32)
        acc_ref[...] = jnp.zeros(acc_ref.shape, F32)

    def k_block(j):
        return k_ref[pl.ds(pl.multiple_of(j * blk, blk), blk), :]

    def causal_mask(s):
        row = lax.broadcasted_iota(jnp.int32, (blk, blk), 0)
        col = lax.broadcasted_iota(jnp.int32, (blk, blk), 1)
        return jnp.where(row <= col, s, -jnp.inf)

    def probs(j, cur, prev, first=False, masked=False, qt_ref=qt_ref):
        p_ref, st_ref, lse_ref, ps_ref = cur
        kb = k_block(j)
        for h in range(2):
            s = jnp.dot(kb, qt_ref[h], preferred_element_type=F32)
            if masked:
                s = causal_mask(s)
            if first:
                stab = jnp.max(s, axis=0, keepdims=True)
            else:
                stab = jnp.maximum(prev[1][h], prev[2][h])
            p = jnp.exp2(s - stab)
            ps = jnp.sum(p, axis=0, keepdims=True)
            ps_ref[h] = ps
            p_ref[h] = p.astype(BF16)
            lse_ref[h] = stab + jnp.log2(ps)
            st_ref[h] = stab

    def fold(j, cur):
        p_ref, st_ref, _, ps_ref = cur
        vtb = vt_ref[j]
        for h in range(2):
            stab = st_ref[h]
            alpha = jnp.exp2(m_ref[h] - stab)
            l_ref[h] = alpha * l_ref[h] + ps_ref[h]
            acc_ref[h] = alpha * acc_ref[h] + jnp.dot(vtb, p_ref[h], preferred_element_type=F32)
            m_ref[h] = stab

    buf_a = (pa_ref, sta_ref, ca_ref, psa_ref)
    buf_b = (pb_ref, stb_ref, cb_ref, psb_ref)
    buf_c = (pc_ref, stc_ref, cc_ref, psc_ref)

    def fast_two_blocks(j):
        probs(j + 1, buf_b, buf_a)
        fold(j, buf_a)
        probs(j + 2, buf_a, buf_b)
        fold(j + 1, buf_b)

    reset_state()
    n_pairs = jnp.maximum(i - 1, 0) // 2

    def fast_oct(p, carry):
        for u in range(4):
            fast_two_blocks(8 * p + 2 * u)
        return carry

    def fast_pair(p, carry):
        fast_two_blocks(8 * (n_pairs // 4) + 2 * p)
        return carry

    lax.fori_loop(0, n_pairs // 4, fast_oct, 0)
    lax.fori_loop(0, n_pairs % 4, fast_pair, 0)
    j0 = 2 * n_pairs

    @pl.when(i == 0)
    def _():
        probs(0, buf_c, None, first=True, masked=True)
        fold(0, buf_c)

    @pl.when(i % 2 == 1)
    def _():
        probs(i, buf_b, buf_a, masked=True)
        fold(j0, buf_a)
        fold(i, buf_b)

    @pl.when((i % 2 == 0) & (i > 0))
    def _():
        probs(j0 + 1, buf_b, buf_a)
        fold(j0, buf_a)
        probs(i, buf_c, buf_b, masked=True)
        fold(j0 + 1, buf_b)
        fold(i, buf_c)

    def scores(j, s_ref, c_ref):
        kb = k_block(j)
        for h in range(2):
            s = jnp.dot(kb, qt_ref[h], preferred_element_type=F32)
            s_ref[h] = s
            c_ref[h] = jnp.max(s, axis=0, keepdims=True)

    def accumulate(j, s_ref, c_ref, masked):
        vtb = vt_ref[j]
        for h in range(2):
            s = s_ref[h]
            if masked:
                s = causal_mask(s)
                c = jnp.max(s, axis=0, keepdims=True)
            else:
                c = c_ref[h]
            m_prev = m_ref[h]
            m_new = jnp.maximum(m_prev, c)
            alpha = jnp.exp2(m_prev - m_new)
            p = jnp.exp2(s - m_new)
            l_ref[h] = alpha * l_ref[h] + jnp.sum(p, axis=0, keepdims=True)
            acc_ref[h] = alpha * acc_ref[h] + jnp.dot(vtb, p.astype(BF16), preferred_element_type=F32)
            m_ref[h] = m_new

    def finalize():
        lf = lam_ref[...]
        lam = (jnp.exp(jnp.sum(lf[0:1] * lf[1:2], axis=1, keepdims=True))
               - jnp.exp(jnp.sum(lf[2:3] * lf[3:4], axis=1, keepdims=True)) + lambda_init)
        o = (acc_ref[0] / l_ref[0] - lam * (acc_ref[1] / l_ref[1])).T
        ms = jnp.mean(o * o, axis=-1, keepdims=True)
        o = o * lax.rsqrt(ms + EPS) * g_ref[...] * (1.0 - lambda_init)
        o_ref[...] = o.astype(o_ref.dtype)

    finalize()
    transposed_halves(qn_ref[...], qtn_ref)
    probs(0, buf_a, None, first=True, qt_ref=qtn_ref)
    total = jnp.sum(l_ref[...]) + jnp.sum(acc_ref[...])
    overflowed = jnp.logical_not(jnp.abs(total) < jnp.inf)

    @pl.when(overflowed)
    def _():
        reset_state()
        scores(0, sa_ref, ca_ref)

        def slow_pair(p, carry):
            j = 2 * p
            scores(j + 1, sb_ref, cb_ref)
            accumulate(j, sa_ref, ca_ref, False)
            scores(j + 2, sa_ref, ca_ref)
            accumulate(j + 1, sb_ref, cb_ref, False)
            return carry

        lax.fori_loop(0, i // 2, slow_pair, 0)

        @pl.when(i % 2 == 1)
        def _():
            scores(i, sb_ref, cb_ref)
            accumulate(i - 1, sa_ref, ca_ref, False)
            accumulate(i, sb_ref, cb_ref, True)

        @pl.when(i % 2 == 0)
        def _():
            accumulate(i, sa_ref, ca_ref, True)

        finalize()


def _diff_attention(proj, lambdas, norm_g, lambda_init, blk):
    t = proj.shape[0]
    qo, ko, vo = OFF_DQ // LANES, OFF_DK // LANES, OFF_DV // LANES
    n_q = t // blk
    stat = pltpu.VMEM((2, 1, blk), F32)
    prob_buffer = [pltpu.VMEM((2, blk, blk), BF16), stat, stat, stat]
    return pl.pallas_call(
        functools.partial(_diff_attn_kernel, blk=blk, lambda_init=lambda_init),
        out_shape=jax.ShapeDtypeStruct((t, DIFF_WIDTH), BF16),
        grid=(DIFF_HEADS, t // blk),
        in_specs=[
            pl.BlockSpec((4, DIFF_HEAD_DIM), lambda h, i: (0, 0)),
            pl.BlockSpec((blk, LANES), lambda h, i: (i, qo + h)),
            pl.BlockSpec((blk, LANES), lambda h, i: (jnp.minimum(i + 1, n_q - 1), qo + h)),
            pl.BlockSpec((t, LANES), lambda h, i: (0, ko + h)),
            pl.BlockSpec((t, LANES), lambda h, i: (0, vo + h)),
            pl.BlockSpec((1, LANES), lambda h, i: (0, 0)),
        ],
        out_specs=pl.BlockSpec((blk, LANES), lambda h, i: (i, h)),
        scratch_shapes=[
            pltpu.VMEM((t // blk, LANES, blk), BF16),
            pltpu.VMEM((2, LANES, blk), BF16),
            pltpu.VMEM((2, LANES, blk), BF16),
            pltpu.VMEM((2, blk, blk), F32),
            pltpu.VMEM((2, blk, blk), F32),
            pltpu.VMEM((2, blk, blk), BF16),
            pltpu.VMEM((2, blk, blk), BF16),
            stat, stat,
            stat, stat,
            stat, stat,
            *prob_buffer,
            stat,
            stat,
            pltpu.VMEM((2, LANES, blk), F32),
        ],
        compiler_params=_cparams(("arbitrary", "arbitrary")),
        name="diff_attention",
    )(lambdas, proj, proj, proj, proj, norm_g)


def _gla_kernel(gv_ref, gr_ref, gq_ref, gk_ref, code_ref, w2_ref, gb_ref, ng_ref, o_ref, st_ref, g_ref, b_ref,
                kf_ref, st0_ref):
    tile = gq_ref.shape[0]
    c_len = GLA_CHUNK
    n_chunks = tile // c_len

    @pl.when(pl.program_id(0) == 0)
    def _():
        st_ref[...] = jnp.zeros(st_ref.shape, F32)

    z = jnp.dot(code_ref[...], w2_ref[...], preferred_element_type=F32) + gb_ref[...]
    g_ref[...] = (jnp.minimum(z, 0.0) - jnp.log1p(jnp.exp(-jnp.abs(z)))) * (1.0 / GLA_TAU)

    row = lax.broadcasted_iota(jnp.int32, (c_len, c_len), 0)
    col = lax.broadcasted_iota(jnp.int32, (c_len, c_len), 1)
    causal = col <= row
    tri = jnp.where(causal, 1.0, 0.0).astype(BF16)
    ng = ng_ref[...]

    def chunk(c, direct):
        r0 = c * c_len if isinstance(c, int) else pl.multiple_of(c * c_len, c_len)
        g = g_ref[pl.ds(r0, c_len), :]
        g_hi = g.astype(BF16)
        g_lo = (g - g_hi.astype(F32)).astype(BF16)
        b = (jnp.dot(tri, g_hi, preferred_element_type=F32)
             + jnp.dot(tri, g_lo, preferred_element_type=F32))
        b_last = b[c_len - 1:c_len, :]
        q = gq_ref[pl.ds(r0, c_len), :].astype(F32) * (GLA_DK ** -0.5)
        k = gk_ref[pl.ds(r0, c_len), :].astype(F32)
        qd = (q * jnp.exp(b)).astype(BF16)
        kl = (k * jnp.exp(b_last - b)).astype(BF16)
        e_last = jnp.exp(b_last)
        if direct:
            b_ref[...] = b
            kf_ref[...] = k
        else:
            kd = (k * jnp.exp(-b)).astype(BF16)
        for h in range(GLA_HEADS):
            ks = slice(h * GLA_DK_PAD, (h + 1) * GLA_DK_PAD)
            vs = slice(h * GLA_DV_PAD, (h + 1) * GLA_DV_PAD)
            vh = gv_ref[pl.ds(r0, c_len), vs]
            st = st_ref[h]
            if direct:
                qh, bh = q[:, ks], b[:, ks]

                def columns(grp, a):
                    g0 = pl.multiple_of(grp * 8, 8)
                    b8 = b_ref[pl.ds(g0, 8), ks]
                    k8 = kf_ref[pl.ds(g0, 8), ks]
                    for u in range(8):
                        w = qh * k8[u:u + 1] * jnp.exp(jnp.minimum(bh - b8[u:u + 1], 0.0))
                        a = jnp.where(col == g0 + u, jnp.sum(w, axis=1, keepdims=True), a)
                    return a

                a = lax.fori_loop(0, c_len // 8, columns, jnp.zeros((c_len, c_len), F32))
            else:
                a = lax.dot_general(qd[:, ks], kd[:, ks], (((1,), (1,)), ((), ())), preferred_element_type=F32)
            a = jnp.where(causal, a, 0.0).astype(BF16)
            o = jnp.dot(a, vh, preferred_element_type=F32)
            o = o + lax.dot_general(qd[:, ks], st.astype(BF16), (((1,), (1,)), ((), ())),
                                    preferred_element_type=F32)
            st_ref[h] = st * e_last[:, ks] + lax.dot_general(
                vh, kl[:, ks], (((0,), (0,)), ((), ())), preferred_element_type=F32)
            ms = jnp.sum(o * o, axis=-1, keepdims=True) * (1.0 / GLA_DV)
            on = o * lax.rsqrt(ms + EPS) * ng
            r = gr_ref[pl.ds(r0, c_len), vs].astype(F32)
            o_ref[pl.ds(r0, c_len), vs] = (on * (r / (1.0 + jnp.exp(-r)))).astype(o_ref.dtype)

    st0_ref[...] = st_ref[...]
    for c in range(n_chunks):
        chunk(c, direct=False)

    lowest = jnp.zeros((1, g_ref.shape[1]), F32)
    for c in range(n_chunks):
        lowest = jnp.minimum(lowest, jnp.sum(g_ref[c * c_len:(c + 1) * c_len, :], axis=0, keepdims=True))

    @pl.when(jnp.min(lowest) < -GLA_SAFE_LOG_DECAY)
    def _():
        st_ref[...] = st0_ref[...]

        def body(c, carry):
            chunk(c, direct=True)
            return carry
        lax.fori_loop(0, n_chunks, body, 0)


def _gla_mixer(proj, w2p, gbp, ngp, tile):
    t = proj.shape[0]
    kw = GLA_HEADS * GLA_DK_PAD
    state = pltpu.VMEM((GLA_HEADS, GLA_DV_PAD, GLA_DK_PAD), F32)
    return pl.pallas_call(
        _gla_kernel,
        out_shape=jax.ShapeDtypeStruct((t, MIX_C_WIDTH), BF16),
        grid=(t // tile,),
        in_specs=[
            pl.BlockSpec((tile, MIX_C_WIDTH), lambda i: (i, OFF_GV // MIX_C_WIDTH)),
            pl.BlockSpec((tile, MIX_C_WIDTH), lambda i: (i, OFF_GR // MIX_C_WIDTH)),
            pl.BlockSpec((tile, kw), lambda i: (i, OFF_GQ // kw)),
            pl.BlockSpec((tile, kw), lambda i: (i, OFF_GK // kw)),
            pl.BlockSpec((tile, LANES), lambda i: (i, OFF_CODE // LANES)),
            pl.BlockSpec((LANES, kw), lambda i: (0, 0)),
            pl.BlockSpec((1, kw), lambda i: (0, 0)),
            pl.BlockSpec((1, GLA_DV_PAD), lambda i: (0, 0)),
        ],
        out_specs=pl.BlockSpec((tile, MIX_C_WIDTH), lambda i: (i, 0)),
        scratch_shapes=[
            state,
            pltpu.VMEM((tile, kw), F32),
            pltpu.VMEM((GLA_CHUNK, kw), F32),
            pltpu.VMEM((GLA_CHUNK, kw), F32),
            state,
        ],
        compiler_params=_cparams(("arbitrary",)),
        name="gla_mixer",
    )(proj, proj, proj, proj, proj, w2p, gbp, ngp)


def _pad_heads(w, heads, d, d_pad):
    lead = w.shape[:-1]
    w = w.reshape(lead + (heads, d))
    w = jnp.pad(w, [(0, 0)] * len(lead) + [(0, 0), (0, d_pad - d)])
    return w.reshape(lead + (heads * d_pad,))


def _head_pieces(w, axis, off, heads, d, d_pad):
    pieces = []
    for h in range(heads):
        sl = lax.slice_in_dim(w, off + h * d, off + (h + 1) * d, axis=axis)
        pieces.append(sl)
        zshape = list(w.shape)
        zshape[axis] = d_pad - d
        pieces.append(jnp.zeros(zshape, w.dtype))
    return pieces


def _layout_w_in(w_in):
    splits = [GMLP_WIDTH, GMLP_WIDTH, DIFF_WIDTH, DIFF_WIDTH, DIFF_WIDTH,
              GLA_KEY_WIDTH, GLA_KEY_WIDTH, GLA_WIDTH, GLA_WIDTH, GLA_RANK]
    idx = [0]
    for s in splits:
        idx.append(idx[-1] + s)
    o_u, o_v, o_dq, o_dk, o_dv, o_gq, o_gk, o_gv, o_gr, o_code, end = idx
    ax = w_in.ndim - 1
    pieces = (_head_pieces(w_in, ax, o_gv, GLA_HEADS, GLA_DV, GLA_DV_PAD)
              + _head_pieces(w_in, ax, o_gr, GLA_HEADS, GLA_DV, GLA_DV_PAD)
              + _head_pieces(w_in, ax, o_gq, GLA_HEADS, GLA_DK, GLA_DK_PAD)
              + _head_pieces(w_in, ax, o_gk, GLA_HEADS, GLA_DK, GLA_DK_PAD)
              + [lax.slice_in_dim(w_in, o_u, o_gq, axis=ax),
                 lax.slice_in_dim(w_in, o_code, end, axis=ax),
                 jnp.zeros(w_in.shape[:-1] + (PROJ_WIDTH - OFF_CODE - GLA_RANK,), w_in.dtype)])
    return jnp.concatenate(pieces, axis=ax).astype(BF16)


W_OUT_ROW_BLOCKS = (1024 // GMLP_WIDTH, 1536 // DIFF_WIDTH, 0)


def _layout_w_out(w_out):
    pieces = _head_pieces(w_out, 1, GMLP_WIDTH + DIFF_WIDTH, GLA_HEADS, GLA_DV, GLA_DV_PAD)
    pieces.append(w_out[:, :GMLP_WIDTH + DIFF_WIDTH, :])
    return jnp.concatenate(pieces, axis=1).astype(BF16)


def kernel(x, norm1_g, w_in, gmlp_ln_g, spatial_w, spatial_b, diff_lambdas, diff_norm_g, gla_gate_w2,
           gla_gate_b, gla_norm_g, w_out, norm2_g, w_ffn_in, w_ffn_out, final_g):
    b, t, d = x.shape
    assert b == 1 and d == D_MODEL
    xs = x.reshape(t, d)

    w2p = jnp.pad(_pad_heads(gla_gate_w2, GLA_HEADS, GLA_DK, GLA_DK_PAD),
                  [(0, 0), (0, LANES - GLA_RANK), (0, 0)]).astype(BF16)
    gbp = _pad_heads(gla_gate_b, GLA_HEADS, GLA_DK, GLA_DK_PAD)[:, None, :]
    ngp = jnp.pad(gla_norm_g, [(0, 0), (0, GLA_DV_PAD - GLA_DV)])[:, None, :]
    sb_t = jnp.swapaxes(spatial_b, 1, 2)

    col = jnp.arange(PROJ_WIDTH)
    q_scale = (DIFF_HEAD_DIM ** -0.5) * math.log2(math.e)
    col_scale = jnp.where((col >= OFF_DQ) & (col < OFF_DK), q_scale, 1.0).astype(F32)[None, :]

    wo = _layout_w_out(w_out)
    wf_out = w_ffn_out.astype(BF16)

    for l in range(DEPTH):
        lambda_init = 0.8 - 0.6 * math.exp(-0.3 * l)
        w1 = _layout_w_in(w_in[l])
        proj = _norm_matmul(xs, norm1_g[l][None, :], w1, col_scale, min(TM_IN_PROJ, t), TN_IN_PROJ)
        a_out = _gmlp_mixer(proj, gmlp_ln_g[l][None, :], spatial_w[l], sb_t[l], min(TILE_GMLP, t))
        b_out = _diff_attention(proj, diff_lambdas[l], diff_norm_g[l][None, :], lambda_init, min(BLK_ATTN, t))
        c_out = _gla_mixer(proj, w2p[l], gbp[l], ngp[l], min(TILE_GLA, t))
        xs, h2 = _matmul_residual_norm([a_out, b_out, c_out], wo, W_OUT_ROW_BLOCKS, l, xs, norm2_g[l][None, :],
                                       min(TM_OUT_PROJ, t), "out_proj")
        hid = _swiglu(h2, w_ffn_in, l, min(TM_FFN_IN, t), TN_FFN_IN)
        xs = _matmul_residual(hid, wf_out, l, xs, final_g[None, :], min(TM_FFN_OUT, t), final_norm=l == DEPTH - 1)
    return xs.reshape(b, t, d)
```

```python
import functools
import math

import jax
import jax.numpy as jnp
from jax import lax
from jax.experimental import pallas as pl
from jax.experimental.pallas import tpu as pltpu

F32 = jnp.float32
BF16 = jnp.bfloat16

D_MODEL = 2048
DEPTH = 4
EPS = 1e-6
LANES = 128

GMLP_WIDTH = 512
GMLP_GROUPS = 4
GMLP_CHUNK = 128
DIFF_WIDTH = 768
DIFF_HEADS = 6
DIFF_HEAD_DIM = 64
GLA_HEADS = 4
GLA_DK = 96
GLA_DV = 192
GLA_DK_PAD = 128
GLA_DV_PAD = 256
GLA_KEY_WIDTH = GLA_HEADS * GLA_DK
GLA_WIDTH = GLA_HEADS * GLA_DV
GLA_RANK = 16
GLA_TAU = 16.0
GLA_CHUNK = 64
GLA_SAFE_LOG_DECAY = 80.0

OFF_GV = 0
OFF_GR = OFF_GV + GLA_HEADS * GLA_DV_PAD
OFF_GQ = OFF_GR + GLA_HEADS * GLA_DV_PAD
OFF_GK = OFF_GQ + GLA_HEADS * GLA_DK_PAD
OFF_U = OFF_GK + GLA_HEADS * GLA_DK_PAD
OFF_V = OFF_U + GMLP_WIDTH
OFF_DQ = OFF_V + GMLP_WIDTH
OFF_DK = OFF_DQ + DIFF_WIDTH
OFF_DV = OFF_DK + DIFF_WIDTH
OFF_CODE = OFF_DV + DIFF_WIDTH
PROJ_TN = 512
PROJ_WIDTH = -(-(OFF_CODE + LANES) // PROJ_TN) * PROJ_TN
MIX_C_WIDTH = GLA_HEADS * GLA_DV_PAD

VMEM_LIMIT = 56 * 1024 * 1024

TM_IN_PROJ, TN_IN_PROJ = 512, PROJ_WIDTH // 2
TM_OUT_PROJ = 512
TM_FFN_IN, TN_FFN_IN = 2048, 512
TM_FFN_OUT = 256
TILE_GMLP = 512
TILE_GLA = 512
BLK_ATTN = 512


def _cparams(sem):
    return pltpu.CompilerParams(dimension_semantics=sem, vmem_limit_bytes=VMEM_LIMIT)


def _rmsnorm_rows(x_ref, g_ref, h_ref, rows):
    tm = x_ref.shape[0]
    g = g_ref[...]
    for r in range(0, tm, rows):
        x = x_ref[r:r + rows, :]
        ms = jnp.mean(x * x, axis=-1, keepdims=True)
        h_ref[r:r + rows, :] = (x * lax.rsqrt(ms + EPS) * g).astype(BF16)


def _norm_mm_kernel(x_ref, g_ref, w_ref, cs_ref, o_ref, h_ref):
    @pl.when(pl.program_id(1) == 0)
    def _():
        _rmsnorm_rows(x_ref, g_ref, h_ref, 256)

    acc = jnp.dot(h_ref[...], w_ref[...], preferred_element_type=F32)
    o_ref[...] = (acc * cs_ref[...]).astype(o_ref.dtype)


def _swiglu_kernel(h_ref, wg_ref, wu_ref, o_ref):
    h = h_ref[...]
    gate = jnp.dot(h, wg_ref[...].astype(BF16), preferred_element_type=F32)
    up = jnp.dot(h, wu_ref[...].astype(BF16), preferred_element_type=F32)
    o_ref[...] = (gate / (1.0 + jnp.exp(-gate)) * up).astype(o_ref.dtype)


def _norm_matmul(x, g, w, col_scale, tm, tn):
    t, d = x.shape
    n = w.shape[1]
    return pl.pallas_call(
        _norm_mm_kernel,
        out_shape=jax.ShapeDtypeStruct((t, n), BF16),
        grid=(t // tm, n // tn),
        in_specs=[
            pl.BlockSpec((tm, d), lambda i, j: (i, 0)),
            pl.BlockSpec((1, d), lambda i, j: (0, 0)),
            pl.BlockSpec((d, tn), lambda i, j: (0, j)),
            pl.BlockSpec((1, tn), lambda i, j: (0, j)),
        ],
        out_specs=pl.BlockSpec((tm, tn), lambda i, j: (i, j)),
        scratch_shapes=[pltpu.VMEM((tm, d), BF16)],
        compiler_params=_cparams(("parallel", "arbitrary")),
        name="norm_in_proj",
    )(x, g, w, col_scale)


def _swiglu(h, w, layer, tm, tn):
    t, d = h.shape
    hidden = w.shape[2] // 2
    nj = hidden // tn
    return pl.pallas_call(
        _swiglu_kernel,
        out_shape=jax.ShapeDtypeStruct((t, hidden), BF16),
        grid=(t // tm, nj),
        in_specs=[
            pl.BlockSpec((tm, d), lambda i, j: (i, 0)),
            pl.BlockSpec((None, d, tn), lambda i, j: (layer, 0, j)),
            pl.BlockSpec((None, d, tn), lambda i, j: (layer, 0, j + nj)),
        ],
        out_specs=pl.BlockSpec((tm, tn), lambda i, j: (i, j)),
        compiler_params=_cparams(("parallel", "arbitrary")),
        name="ffn_in",
    )(h, w, w)


def _mm_res_norm_kernel(*refs, n_a):
    a_refs, w_refs = refs[:n_a], refs[n_a:2 * n_a]
    r_ref, g_ref, o_ref, h_ref = refs[2 * n_a:2 * n_a + 4]
    acc = r_ref[...]
    for a_ref, w_ref in zip(a_refs, w_refs):
        acc = acc + jnp.dot(a_ref[...], w_ref[...], preferred_element_type=F32)
    o_ref[...] = acc
    ms = jnp.mean(acc * acc, axis=-1, keepdims=True)
    h_ref[...] = (acc * lax.rsqrt(ms + EPS) * g_ref[...]).astype(h_ref.dtype)


def _matmul_residual_norm(a_list, w, w_row_blocks, layer, res, g, tm, name):
    t, n = res.shape
    n_a = len(a_list)
    in_specs = [pl.BlockSpec((tm, a.shape[1]), lambda i: (i, 0)) for a in a_list]
    in_specs += [pl.BlockSpec((None, a.shape[1], n), lambda i, rb=rb: (layer, rb, 0))
                 for a, rb in zip(a_list, w_row_blocks)]
    in_specs += [pl.BlockSpec((tm, n), lambda i: (i, 0)), pl.BlockSpec((1, n), lambda i: (0, 0))]
    w_list = [w] * n_a
    return pl.pallas_call(
        functools.partial(_mm_res_norm_kernel, n_a=n_a),
        out_shape=(jax.ShapeDtypeStruct((t, n), F32), jax.ShapeDtypeStruct((t, n), BF16)),
        grid=(t // tm,),
        in_specs=in_specs,
        out_specs=(pl.BlockSpec((tm, n), lambda i: (i, 0)), pl.BlockSpec((tm, n), lambda i: (i, 0))),
        compiler_params=_cparams(("parallel",)),
        name=name,
    )(*a_list, *w_list, res, g)


def _mm_res_kernel(a_ref, w_ref, r_ref, g_ref, o_ref, *, final_norm):
    acc = r_ref[...] + jnp.dot(a_ref[...], w_ref[...], preferred_element_type=F32)
    if final_norm:
        ms = jnp.mean(acc * acc, axis=-1, keepdims=True)
        acc = acc * lax.rsqrt(ms + EPS) * g_ref[...]
    o_ref[...] = acc


def _matmul_residual(a, w, layer, res, g, tm, final_norm):
    t, n = res.shape
    k = a.shape[1]
    return pl.pallas_call(
        functools.partial(_mm_res_kernel, final_norm=final_norm),
        out_shape=jax.ShapeDtypeStruct((t, n), F32),
        grid=(t // tm,),
        in_specs=[
            pl.BlockSpec((tm, k), lambda i: (i, 0)),
            pl.BlockSpec((None, k, n), lambda i: (layer, 0, 0), pipeline_mode=pl.Buffered(1)),
            pl.BlockSpec((tm, n), lambda i: (i, 0)),
            pl.BlockSpec((1, n), lambda i: (0, 0)),
        ],
        out_specs=pl.BlockSpec((tm, n), lambda i: (i, 0)),
        compiler_params=_cparams(("parallel",)),
        name="ffn_out",
    )(a, w, res, g)


def _gelu_tanh(x):
    return 0.5 * x * (1.0 + jnp.tanh(math.sqrt(2.0 / math.pi) * (x + 0.044715 * (x * x * x))))


def _gmlp_kernel(u_ref, v_ref, lng_ref, w_ref, b_ref, o_ref):
    tile = u_ref.shape[0]
    gd = GMLP_WIDTH // GMLP_GROUPS
    v = _gelu_tanh(v_ref[...].astype(F32))
    mu = jnp.mean(v, axis=-1, keepdims=True)
    vc = v - mu
    var = jnp.mean(vc * vc, axis=-1, keepdims=True)
    vn = (vc * lax.rsqrt(var + EPS) * lng_ref[...]).astype(BF16)
    row = lax.broadcasted_iota(jnp.int32, (GMLP_CHUNK, GMLP_CHUNK), 0)
    col = lax.broadcasted_iota(jnp.int32, (GMLP_CHUNK, GMLP_CHUNK), 1)
    causal = col <= row
    bias = b_ref[...]
    for g in range(GMLP_GROUPS):
        wg = jnp.where(causal, w_ref[g], 0.0).astype(BF16)
        bg = bias[:, g:g + 1]
        for c in range(tile // GMLP_CHUNK):
            rs = slice(c * GMLP_CHUNK, (c + 1) * GMLP_CHUNK)
            cs = slice(g * gd, (g + 1) * gd)
            mixed = jnp.dot(wg, vn[rs, cs], preferred_element_type=F32) + bg
            u = _gelu_tanh(u_ref[rs, cs].astype(F32))
            o_ref[rs, cs] = (u * mixed).astype(o_ref.dtype)


def _gmlp_mixer(proj, ln_g, spatial_w, spatial_b_t, tile):
    t = proj.shape[0]
    return pl.pallas_call(
        _gmlp_kernel,
        out_shape=jax.ShapeDtypeStruct((t, GMLP_WIDTH), BF16),
        grid=(t // tile,),
        in_specs=[
            pl.BlockSpec((tile, GMLP_WIDTH), lambda i: (i, OFF_U // GMLP_WIDTH)),
            pl.BlockSpec((tile, GMLP_WIDTH), lambda i: (i, OFF_V // GMLP_WIDTH)),
            pl.BlockSpec((1, GMLP_WIDTH), lambda i: (0, 0)),
            pl.BlockSpec((GMLP_GROUPS, GMLP_CHUNK, GMLP_CHUNK), lambda i: (0, 0, 0)),
            pl.BlockSpec((GMLP_CHUNK, GMLP_GROUPS), lambda i: (0, 0)),
        ],
        out_specs=pl.BlockSpec((tile, GMLP_WIDTH), lambda i: (i, 0)),
        compiler_params=_cparams(("parallel",)),
        name="gmlp_mixer",
    )(proj, proj, ln_g, spatial_w, spatial_b_t)


def _diff_attn_kernel(lam_ref, q_ref, qn_ref, k_ref, v_ref, g_ref, o_ref, vt_ref, qt_ref, qtn_ref, sa_ref, sb_ref,
                      pa_ref, pb_ref, ca_ref, cb_ref, sta_ref, stb_ref, psa_ref, psb_ref, pc_ref, cc_ref, stc_ref,
                      psc_ref, m_ref, l_ref, acc_ref, *, blk, lambda_init):
    i = pl.program_id(1)
    n_blk = vt_ref.shape[0]

    def transposed_halves(q_blk, dst_ref):
        qt = q_blk.astype(F32).T
        sub = lax.broadcasted_iota(jnp.int32, (LANES, blk), 0)
        dst_ref[0] = jnp.where(sub < DIFF_HEAD_DIM, qt, 0.0).astype(BF16)
        dst_ref[1] = jnp.where(sub >= DIFF_HEAD_DIM, qt, 0.0).astype(BF16)

    @pl.when(i == 0)
    def _():
        def transpose_v(c, carry):
            r0 = pl.multiple_of(c * blk, blk)
            vt_ref[c] = v_ref[pl.ds(r0, blk), :].astype(F32).T.astype(BF16)
            return carry
        lax.fori_loop(0, n_blk, transpose_v, 0)

    transposed_halves(q_ref[...], qt_ref)

    def reset_state():
        m_ref[...] = jnp.full(m_ref.shape, -jnp.inf, F32)
        l_ref[...] = jnp.zeros(l_ref.shape, F32)
        acc_ref[...] = jnp.zeros(acc_ref.shape, F32)

    def k_block(j):
        return k_ref[pl.ds(pl.multiple_of(j * blk, blk), blk), :]

    def causal_mask(s):
        row = lax.broadcasted_iota(jnp.int32, (blk, blk), 0)
        col = lax.broadcasted_iota(jnp.int32, (blk, blk), 1)
        return jnp.where(row <= col, s, -jnp.inf)

    def probs(j, cur, prev, first=False, masked=False, qt_ref=qt_ref):
        p_ref, st_ref, lse_ref, ps_ref = cur
        kb = k_block(j)
        for h in range(2):
            s = jnp.dot(kb, qt_ref[h], preferred_element_type=F32)
            if masked:
                s = causal_mask(s)
            if first:
                stab = jnp.max(s, axis=0, keepdims=True)
            else:
                stab = jnp.maximum(prev[1][h], prev[2][h])
            p = jnp.exp2(s - stab)
            ps = jnp.sum(p, axis=0, keepdims=True)
            ps_ref[h] = ps
            p_ref[h] = p.astype(BF16)
            lse_ref[h] = stab + jnp.log2(ps)
            st_ref[h] = stab

    def fold(j, cur):
        p_ref, st_ref, _, ps_ref = cur
        vtb = vt_ref[j]
        for h in range(2):
            stab = st_ref[h]
            alpha = jnp.exp2(m_ref[h] - stab)
            l_ref[h] = alpha * l_ref[h] + ps_ref[h]
            acc_ref[h] = alpha * acc_ref[h] + jnp.dot(vtb, p_ref[h], preferred_element_type=F32)
            m_ref[h] = stab

    buf_a = (pa_ref, sta_ref, ca_ref, psa_ref)
    buf_b = (pb_ref, stb_ref, cb_ref, psb_ref)
    buf_c = (pc_ref, stc_ref, cc_ref, psc_ref)

    def fast_two_blocks(j):
        probs(j + 1, buf_b, buf_a)
        fold(j, buf_a)
        probs(j + 2, buf_a, buf_b)
        fold(j + 1, buf_b)

    reset_state()
    n_pairs = jnp.maximum(i - 1, 0) // 2

    def fast_oct(p, carry):
        for u in range(4):
            fast_two_blocks(8 * p + 2 * u)
        return carry

    def fast_quad(p, carry):
        fast_two_blocks(8 * (n_pairs // 4) + 4 * p)
        fast_two_blocks(8 * (n_pairs // 4) + 4 * p + 2)
        return carry

    def fast_pair(p, carry):
        fast_two_blocks(4 * (n_pairs // 2) + 2 * p)
        return carry

    lax.fori_loop(0, n_pairs // 4, fast_oct, 0)
    lax.fori_loop(0, (n_pairs % 4) // 2, fast_quad, 0)
    lax.fori_loop(0, n_pairs % 2, fast_pair, 0)
    j0 = 2 * n_pairs

    @pl.when(i == 0)
    def _():
        probs(0, buf_c, None, first=True, masked=True)
        fold(0, buf_c)

    @pl.when(i % 2 == 1)
    def _():
        probs(i, buf_b, buf_a, masked=True)
        fold(j0, buf_a)
        fold(i, buf_b)

    @pl.when((i % 2 == 0) & (i > 0))
    def _():
        probs(j0 + 1, buf_b, buf_a)
        fold(j0, buf_a)
        probs(i, buf_c, buf_b, masked=True)
        fold(j0 + 1, buf_b)
        fold(i, buf_c)

    def scores(j, s_ref, c_ref):
        kb = k_block(j)
        for h in range(2):
            s = jnp.dot(kb, qt_ref[h], preferred_element_type=F32)
            s_ref[h] = s
            c_ref[h] = jnp.max(s, axis=0, keepdims=True)

    def accumulate(j, s_ref, c_ref, masked):
        vtb = vt_ref[j]
        for h in range(2):
            s = s_ref[h]
            if masked:
                s = causal_mask(s)
                c = jnp.max(s, axis=0, keepdims=True)
            else:
                c = c_ref[h]
            m_prev = m_ref[h]
            m_new = jnp.maximum(m_prev, c)
            alpha = jnp.exp2(m_prev - m_new)
            p = jnp.exp2(s - m_new)
            l_ref[h] = alpha * l_ref[h] + jnp.sum(p, axis=0, keepdims=True)
            acc_ref[h] = alpha * acc_ref[h] + jnp.dot(vtb, p.astype(BF16), preferred_element_type=F32)
            m_ref[h] = m_new

    def finalize():
        lf = lam_ref[...]
        lam = (jnp.exp(jnp.sum(lf[0:1] * lf[1:2], axis=1, keepdims=True))
               - jnp.exp(jnp.sum(lf[2:3] * lf[3:4], axis=1, keepdims=True)) + lambda_init)
        o = (acc_ref[0] / l_ref[0] - lam * (acc_ref[1] / l_ref[1])).T
        ms = jnp.mean(o * o, axis=-1, keepdims=True)
        o = o * lax.rsqrt(ms + EPS) * g_ref[...] * (1.0 - lambda_init)
        o_ref[...] = o.astype(o_ref.dtype)

    finalize()
    transposed_halves(qn_ref[...], qtn_ref)
    probs(0, buf_a, None, first=True, qt_ref=qtn_ref)
    total = jnp.sum(l_ref[...]) + jnp.sum(acc_ref[...])
    overflowed = jnp.logical_not(jnp.abs(total) < jnp.inf)

    @pl.when(overflowed)
    def _():
        reset_state()
        scores(0, sa_ref, ca_ref)

        def slow_pair(p, carry):
            j = 2 * p
            scores(j + 1, sb_ref, cb_ref)
            accumulate(j, sa_ref, ca_ref, False)
            scores(j + 2, sa_ref, ca_ref)
            accumulate(j + 1, sb_ref, cb_ref, False)
            return carry

        lax.fori_loop(0, i // 2, slow_pair, 0)

        @pl.when(i % 2 == 1)
        def _():
            scores(i, sb_ref, cb_ref)
            accumulate(i - 1, sa_ref, ca_ref, False)
            accumulate(i, sb_ref, cb_ref, True)

        @pl.when(i % 2 == 0)
        def _():
            accumulate(i, sa_ref, ca_ref, True)

        finalize()


def _diff_attention(proj, lambdas, norm_g, lambda_init, blk):
    t = proj.shape[0]
    qo, ko, vo = OFF_DQ // LANES, OFF_DK // LANES, OFF_DV // LANES
    n_q = t // blk
    stat = pltpu.VMEM((2, 1, blk), F32)
    prob_buffer = [pltpu.VMEM((2, blk, blk), BF16), stat, stat, stat]
    return pl.pallas_call(
        functools.partial(_diff_attn_kernel, blk=blk, lambda_init=lambda_init),
        out_shape=jax.ShapeDtypeStruct((t, DIFF_WIDTH), BF16),
        grid=(DIFF_HEADS, t // blk),
        in_specs=[
            pl.BlockSpec((4, DIFF_HEAD_DIM), lambda h, i: (0, 0)),
            pl.BlockSpec((blk, LANES), lambda h, i: (i, qo + h)),
            pl.BlockSpec((blk, LANES), lambda h, i: (jnp.minimum(i + 1, n_q - 1), qo + h)),
            pl.BlockSpec((t, LANES), lambda h, i: (0, ko + h)),
            pl.BlockSpec((t, LANES), lambda h, i: (0, vo + h)),
            pl.BlockSpec((1, LANES), lambda h, i: (0, 0)),
        ],
        out_specs=pl.BlockSpec((blk, LANES), lambda h, i: (i, h)),
        scratch_shapes=[
            pltpu.VMEM((t // blk, LANES, blk), BF16),
            pltpu.VMEM((2, LANES, blk), BF16),
            pltpu.VMEM((2, LANES, blk), BF16),
            pltpu.VMEM((2, blk, blk), F32),
            pltpu.VMEM((2, blk, blk), F32),
            pltpu.VMEM((2, blk, blk), BF16),
            pltpu.VMEM((2, blk, blk), BF16),
            stat, stat,
            stat, stat,
            stat, stat,
            *prob_buffer,
            stat,
            stat,
            pltpu.VMEM((2, LANES, blk), F32),
        ],
        compiler_params=_cparams(("arbitrary", "arbitrary")),
        name="diff_attention",
    )(lambdas, proj, proj, proj, proj, norm_g)


def _gla_kernel(gv_ref, gr_ref, gq_ref, gk_ref, code_ref, w2_ref, gb_ref, ng_ref, o_ref, st_ref, g_ref, b_ref,
                kf_ref, st0_ref):
    tile = gq_ref.shape[0]
    c_len = GLA_CHUNK
    n_chunks = tile // c_len

    @pl.when(pl.program_id(0) == 0)
    def _():
        st_ref[...] = jnp.zeros(st_ref.shape, F32)

    z = jnp.dot(code_ref[...], w2_ref[...], preferred_element_type=F32) + gb_ref[...]
    g_ref[...] = (jnp.minimum(z, 0.0) - jnp.log1p(jnp.exp(-jnp.abs(z)))) * (1.0 / GLA_TAU)

    row = lax.broadcasted_iota(jnp.int32, (c_len, c_len), 0)
    col = lax.broadcasted_iota(jnp.int32, (c_len, c_len), 1)
    causal = col <= row
    tri = jnp.where(causal, 1.0, 0.0).astype(BF16)
    ng = ng_ref[...]

    def chunk(c, direct):
        r0 = c * c_len if isinstance(c, int) else pl.multiple_of(c * c_len, c_len)
        g = g_ref[pl.ds(r0, c_len), :]
        g_hi = g.astype(BF16)
        g_lo = (g - g_hi.astype(F32)).astype(BF16)
        b = (jnp.dot(tri, g_hi, preferred_element_type=F32)
             + jnp.dot(tri, g_lo, preferred_element_type=F32))
        b_last = b[c_len - 1:c_len, :]
        q = gq_ref[pl.ds(r0, c_len), :].astype(F32) * (GLA_DK ** -0.5)
        k = gk_ref[pl.ds(r0, c_len), :].astype(F32)
        qd = (q * jnp.exp(b)).astype(BF16)
        kl = (k * jnp.exp(b_last - b)).astype(BF16)
        e_last = jnp.exp(b_last)
        if direct:
            b_ref[...] = b
            kf_ref[...] = k
        else:
            kd = (k * jnp.exp(-b)).astype(BF16)
        for h in range(GLA_HEADS):
            ks = slice(h * GLA_DK_PAD, (h + 1) * GLA_DK_PAD)
            vs = slice(h * GLA_DV_PAD, (h + 1) * GLA_DV_PAD)
            vh = gv_ref[pl.ds(r0, c_len), vs]
            st = st_ref[h]
            if direct:
                qh, bh = q[:, ks], b[:, ks]

                def columns(grp, a):
                    g0 = pl.multiple_of(grp * 8, 8)
                    b8 = b_ref[pl.ds(g0, 8), ks]
                    k8 = kf_ref[pl.ds(g0, 8), ks]
                    for u in range(8):
                        w = qh * k8[u:u + 1] * jnp.exp(jnp.minimum(bh - b8[u:u + 1], 0.0))
                        a = jnp.where(col == g0 + u, jnp.sum(w, axis=1, keepdims=True), a)
                    return a

                a = lax.fori_loop(0, c_len // 8, columns, jnp.zeros((c_len, c_len), F32))
            else:
                a = lax.dot_general(qd[:, ks], kd[:, ks], (((1,), (1,)), ((), ())), preferred_element_type=F32)
            a = jnp.where(causal, a, 0.0).astype(BF16)
            o = jnp.dot(a, vh, preferred_element_type=F32)
            o = o + lax.dot_general(qd[:, ks], st.astype(BF16), (((1,), (1,)), ((), ())),
                                    preferred_element_type=F32)
            st_ref[h] = st * e_last[:, ks] + lax.dot_general(
                vh, kl[:, ks], (((0,), (0,)), ((), ())), preferred_element_type=F32)
            ms = jnp.sum(o * o, axis=-1, keepdims=True) * (1.0 / GLA_DV)
            on = o * lax.rsqrt(ms + EPS) * ng
            r = gr_ref[pl.ds(r0, c_len), vs].astype(F32)
            o_ref[pl.ds(r0, c_len), vs] = (on * (r / (1.0 + jnp.exp(-r)))).astype(o_ref.dtype)

    st0_ref[...] = st_ref[...]
    for c in range(n_chunks):
        chunk(c, direct=False)

    lowest = jnp.zeros((1, g_ref.shape[1]), F32)
    for c in range(n_chunks):
        lowest = jnp.minimum(lowest, jnp.sum(g_ref[c * c_len:(c + 1) * c_len, :], axis=0, keepdims=True))

    @pl.when(jnp.min(lowest) < -GLA_SAFE_LOG_DECAY)
    def _():
        st_ref[...] = st0_ref[...]

        def body(c, carry):
            chunk(c, direct=True)
            return carry
        lax.fori_loop(0, n_chunks, body, 0)


def _gla_mixer(proj, w2p, gbp, ngp, tile):
    t = proj.shape[0]
    kw = GLA_HEADS * GLA_DK_PAD
    state = pltpu.VMEM((GLA_HEADS, GLA_DV_PAD, GLA_DK_PAD), F32)
    return pl.pallas_call(
        _gla_kernel,
        out_shape=jax.ShapeDtypeStruct((t, MIX_C_WIDTH), BF16),
        grid=(t // tile,),
        in_specs=[
            pl.BlockSpec((tile, MIX_C_WIDTH), lambda i: (i, OFF_GV // MIX_C_WIDTH)),
            pl.BlockSpec((tile, MIX_C_WIDTH), lambda i: (i, OFF_GR // MIX_C_WIDTH)),
            pl.BlockSpec((tile, kw), lambda i: (i, OFF_GQ // kw)),
            pl.BlockSpec((tile, kw), lambda i: (i, OFF_GK // kw)),
            pl.BlockSpec((tile, LANES), lambda i: (i, OFF_CODE // LANES)),
            pl.BlockSpec((LANES, kw), lambda i: (0, 0)),
            pl.BlockSpec((1, kw), lambda i: (0, 0)),
            pl.BlockSpec((1, GLA_DV_PAD), lambda i: (0, 0)),
        ],
        out_specs=pl.BlockSpec((tile, MIX_C_WIDTH), lambda i: (i, 0)),
        scratch_shapes=[
            state,
            pltpu.VMEM((tile, kw), F32),
            pltpu.VMEM((GLA_CHUNK, kw), F32),
            pltpu.VMEM((GLA_CHUNK, kw), F32),
            state,
        ],
        compiler_params=_cparams(("arbitrary",)),
        name="gla_mixer",
    )(proj, proj, proj, proj, proj, w2p, gbp, ngp)


def _pad_heads(w, heads, d, d_pad):
    lead = w.shape[:-1]
    w = w.reshape(lead + (heads, d))
    w = jnp.pad(w, [(0, 0)] * len(lead) + [(0, 0), (0, d_pad - d)])
    return w.reshape(lead + (heads * d_pad,))


def _head_pieces(w, axis, off, heads, d, d_pad):
    pieces = []
    for h in range(heads):
        sl = lax.slice_in_dim(w, off + h * d, off + (h + 1) * d, axis=axis)
        pieces.append(sl)
        zshape = list(w.shape)
        zshape[axis] = d_pad - d
        pieces.append(jnp.zeros(zshape, w.dtype))
    return pieces


def _layout_w_in(w_in):
    splits = [GMLP_WIDTH, GMLP_WIDTH, DIFF_WIDTH, DIFF_WIDTH, DIFF_WIDTH,
              GLA_KEY_WIDTH, GLA_KEY_WIDTH, GLA_WIDTH, GLA_WIDTH, GLA_RANK]
    idx = [0]
    for s in splits:
        idx.append(idx[-1] + s)
    o_u, o_v, o_dq, o_dk, o_dv, o_gq, o_gk, o_gv, o_gr, o_code, end = idx
    ax = w_in.ndim - 1
    pieces = (_head_pieces(w_in, ax, o_gv, GLA_HEADS, GLA_DV, GLA_DV_PAD)
              + _head_pieces(w_in, ax, o_gr, GLA_HEADS, GLA_DV, GLA_DV_PAD)
              + _head_pieces(w_in, ax, o_gq, GLA_HEADS, GLA_DK, GLA_DK_PAD)
              + _head_pieces(w_in, ax, o_gk, GLA_HEADS, GLA_DK, GLA_DK_PAD)
              + [lax.slice_in_dim(w_in, o_u, o_gq, axis=ax),
                 lax.slice_in_dim(w_in, o_code, end, axis=ax),
                 jnp.zeros(w_in.shape[:-1] + (PROJ_WIDTH - OFF_CODE - GLA_RANK,), w_in.dtype)])
    return jnp.concatenate(pieces, axis=ax).astype(BF16)


W_OUT_ROW_BLOCKS = (1024 // GMLP_WIDTH, 1536 // DIFF_WIDTH, 0)


def _layout_w_out(w_out):
    pieces = _head_pieces(w_out, 1, GMLP_WIDTH + DIFF_WIDTH, GLA_HEADS, GLA_DV, GLA_DV_PAD)
    pieces.append(w_out[:, :GMLP_WIDTH + DIFF_WIDTH, :])
    return jnp.concatenate(pieces, axis=1).astype(BF16)


def kernel(x, norm1_g, w_in, gmlp_ln_g, spatial_w, spatial_b, diff_lambdas, diff_norm_g, gla_gate_w2,
           gla_gate_b, gla_norm_g, w_out, norm2_g, w_ffn_in, w_ffn_out, final_g):
    b, t, d = x.shape
    assert b == 1 and d == D_MODEL
    xs = x.reshape(t, d)

    w2p = jnp.pad(_pad_heads(gla_gate_w2, GLA_HEADS, GLA_DK, GLA_DK_PAD),
                  [(0, 0), (0, LANES - GLA_RANK), (0, 0)]).astype(BF16)
    gbp = _pad_heads(gla_gate_b, GLA_HEADS, GLA_DK, GLA_DK_PAD)[:, None, :]
    ngp = jnp.pad(gla_norm_g, [(0, 0), (0, GLA_DV_PAD - GLA_DV)])[:, None, :]
    sb_t = jnp.swapaxes(spatial_b, 1, 2)

    col = jnp.arange(PROJ_WIDTH)
    q_scale = (DIFF_HEAD_DIM ** -0.5) * math.log2(math.e)
    col_scale = jnp.where((col >= OFF_DQ) & (col < OFF_DK), q_scale, 1.0).astype(F32)[None, :]

    wo = _layout_w_out(w_out)
    wf_out = w_ffn_out.astype(BF16)

    for l in range(DEPTH):
        lambda_init = 0.8 - 0.6 * math.exp(-0.3 * l)
        w1 = _layout_w_in(w_in[l])
        proj = _norm_matmul(xs, norm1_g[l][None, :], w1, col_scale, min(TM_IN_PROJ, t), TN_IN_PROJ)
        a_out = _gmlp_mixer(proj, gmlp_ln_g[l][None, :], spatial_w[l], sb_t[l], min(TILE_GMLP, t))
        b_out = _diff_attention(proj, diff_lambdas[l], diff_norm_g[l][None, :], lambda_init, min(BLK_ATTN, t))
        c_out = _gla_mixer(proj, w2p[l], gbp[l], ngp[l], min(TILE_GLA, t))
        xs, h2 = _matmul_residual_norm([a_out, b_out, c_out], wo, W_OUT_ROW_BLOCKS, l, xs, norm2_g[l][None, :],
                                       min(TM_OUT_PROJ, t), "out_proj")
        hid = _swiglu(h2, w_ffn_in, l, min(TM_FFN_IN, t), TN_FFN_IN)
        xs = _matmul_residual(hid, wf_out, l, xs, final_g[None, :], min(TM_FFN_OUT, t), final_norm=l == DEPTH - 1)
    return xs.reshape(b, t, d)
```
